```python
import jax, jax.numpy as jnp
from jax import lax
import numpy as np

D_MODEL = 2048
BATCH = 1
SEQ = 8192
DEPTH = 1
DEC_BATCH = 16
DEC_SEQ = 16
PAST_LEN = 4096

CHUNK = 64
QBLOCK = 128
HEAD_DIM = 128
N_HEADS_SB = 8
N_HEADS_FOX = 8
W_SB = N_HEADS_SB * HEAD_DIM
W_FOX = N_HEADS_FOX * HEAD_DIM
RMS_EPS = 1e-6
FORGET_BIAS_INIT = 3.0
SPLITS = [W_SB, W_SB, W_SB, W_SB, W_FOX, W_FOX, W_FOX, W_FOX, N_HEADS_FOX, D_MODEL, D_MODEL]
D_IN = 4 * W_SB + 4 * W_FOX + N_HEADS_FOX + 2 * D_MODEL

kernel_name = "stickbreak_fox_gated_hybrid_step"


def _split_points():
    pts, acc = [], 0
    for w in SPLITS[:-1]:
        acc += w
        pts.append(acc)
    return pts


def rms_norm(x, w):
    xf = x.astype(jnp.float32)
    y = xf * lax.rsqrt(jnp.mean(xf * xf, axis=-1, keepdims=True) + RMS_EPS)
    return (y * w.astype(jnp.float32)).astype(x.dtype)


def _sweep(block_fn, q_pos, *q_arrays):
    tq = q_pos.shape[0]
    qb = tq if tq <= QBLOCK else QBLOCK
    nb = tq // qb

    def to_blocks(a):
        a = a.reshape(a.shape[0], nb, qb, *a.shape[2:])
        return jnp.moveaxis(a, 1, 0)

    blocks = (q_pos.reshape(nb, qb),) + tuple(to_blocks(a) for a in q_arrays)
    out = lax.map(lambda args: block_fn(*args), blocks)
    out = jnp.moveaxis(out, 0, 1)
    return out.reshape(out.shape[0], tq, *out.shape[3:])


def stick_breaking_block(qpos, q, k, v, kpos):
    z = jnp.einsum("bqhd,bkhd->bhqk", q.astype(jnp.float32), k.astype(jnp.float32)) * (HEAD_DIM ** -0.5)
    mask = (kpos[None, :] < qpos[:, None])[None, None]
    log_one_minus = jnp.where(mask, jax.nn.log_sigmoid(-z), 0.0)
    later = lax.cumsum(log_one_minus, axis=3, reverse=True) - log_one_minus
    w = jnp.where(mask, jnp.exp(jax.nn.log_sigmoid(z) + later), 0.0)
    return jnp.einsum("bhqk,bkhd->bqhd", w, v.astype(jnp.float32)).astype(v.dtype)


def forgetting_block(qpos, q, f_q, k, v, f_k, kpos):
    z = jnp.einsum("bqhd,bkhd->bhqk", q.astype(jnp.float32), k.astype(jnp.float32)) * (HEAD_DIM ** -0.5)
    decay = jnp.transpose(f_q, (0, 2, 1))[..., :, None] - jnp.transpose(f_k, (0, 2, 1))[..., None, :]
    mask = (kpos[None, :] <= qpos[:, None])[None, None]
    p = jax.nn.softmax(jnp.where(mask, z + decay, -jnp.inf), axis=-1)
    return jnp.einsum("bhqk,bkhd->bqhd", p, v.astype(jnp.float32)).astype(v.dtype)


def hybrid_layer(x, past_sb_k, past_sb_v, past_fox_k, past_fox_v, past_fox_logf,
                 norm_w, w_in, b_forget, q_norm_w, k_norm_w, w_branch_sb, w_branch_fox, w_out):
    bsz, t_new, _ = x.shape
    p_len = 0 if past_sb_k is None else past_sb_k.shape[1]
    h = rms_norm(x, norm_w)
    proj = jnp.einsum("btd,de->bte", h, w_in)
    (q_sb, k_sb, v_sb, g_sb, q_fx, k_fx, v_fx, g_fx,
     f_logit, m_sb, m_fx) = jnp.split(proj, _split_points(), axis=-1)

    heads_sb = lambda a: a.reshape(bsz, t_new, N_HEADS_SB, HEAD_DIM)
    heads_fx = lambda a: a.reshape(bsz, t_new, N_HEADS_FOX, HEAD_DIM)
    q_sb, k_sb, v_sb = heads_sb(q_sb), heads_sb(k_sb), heads_sb(v_sb)
    q_fx = rms_norm(heads_fx(q_fx), q_norm_w)
    k_fx = rms_norm(heads_fx(k_fx), k_norm_w)
    v_fx = heads_fx(v_fx)
    log_f = jax.nn.log_sigmoid(f_logit.astype(jnp.float32) + b_forget.astype(jnp.float32))

    q_pos = p_len + jnp.arange(t_new)
    k_pos = jnp.arange(p_len + t_new)
    if past_sb_k is None:
        ks_all, vs_all, kf_all, vf_all, lf_all = k_sb, v_sb, k_fx, v_fx, log_f
    else:
        ks_all = jnp.concatenate([past_sb_k.astype(k_sb.dtype), k_sb], axis=1)
        vs_all = jnp.concatenate([past_sb_v.astype(v_sb.dtype), v_sb], axis=1)
        kf_all = jnp.concatenate([past_fox_k.astype(k_fx.dtype), k_fx], axis=1)
        vf_all = jnp.concatenate([past_fox_v.astype(v_fx.dtype), v_fx], axis=1)
        lf_all = jnp.concatenate([past_fox_logf.astype(jnp.float32), log_f], axis=1)
    f_cum = lax.cumsum(lf_all, axis=1)
    f_q = f_cum[:, p_len:]

    o_sb = _sweep(lambda qp, qq: stick_breaking_block(qp, qq, ks_all, vs_all, k_pos), q_pos, q_sb)
    o_fx = _sweep(lambda qp, qq, fq: forgetting_block(qp, qq, fq, kf_all, vf_all, f_cum, k_pos),
                  q_pos, q_fx, f_q)

    u_sb = jnp.einsum("btc,cd->btd", o_sb.reshape(bsz, t_new, W_SB) * jax.nn.silu(g_sb), w_branch_sb)
    u_fx = jnp.einsum("btc,cd->btd", o_fx.reshape(bsz, t_new, W_FOX) * jax.nn.silu(g_fx), w_branch_fox)
    merged = jax.nn.sigmoid(m_sb) * u_sb + jax.nn.sigmoid(m_fx) * u_fx
    y = x + jnp.einsum("btd,de->bte", merged, w_out)
    return y, k_sb, v_sb, k_fx, v_fx, log_f


def setup_inputs(seed: int = 0) -> dict:
    key = jax.random.key(seed)
    ks = jax.random.split(key, 16)
    f32 = jnp.float32
    n = lambda k, shape, scale: jax.random.normal(k, shape, f32) * scale
    return {
        "x_prompt": n(ks[0], (BATCH, SEQ, D_MODEL), 1.0),
        "x_sample": n(ks[1], (DEC_BATCH, DEC_SEQ, D_MODEL), 1.0),
        "cache_sb_k": n(ks[2], (DEPTH, DEC_BATCH, PAST_LEN, N_HEADS_SB, HEAD_DIM), 1.0),
        "cache_sb_v": n(ks[3], (DEPTH, DEC_BATCH, PAST_LEN, N_HEADS_SB, HEAD_DIM), 1.0),
        "cache_fox_k": n(ks[4], (DEPTH, DEC_BATCH, PAST_LEN, N_HEADS_FOX, HEAD_DIM), 1.0),
        "cache_fox_v": n(ks[5], (DEPTH, DEC_BATCH, PAST_LEN, N_HEADS_FOX, HEAD_DIM), 1.0),
        "cache_fox_logf": jax.nn.log_sigmoid(FORGET_BIAS_INIT + n(ks[6], (DEPTH, DEC_BATCH, PAST_LEN, N_HEADS_FOX), 1.0)),
        "norm_w": 1.0 + n(ks[7], (DEPTH, D_MODEL), 0.02),
        "w_in": n(ks[8], (DEPTH, D_MODEL, D_IN), D_MODEL ** -0.5),
        "b_forget": FORGET_BIAS_INIT + n(ks[9], (DEPTH, N_HEADS_FOX), 0.1),
        "q_norm_w": 1.0 + n(ks[10], (DEPTH, HEAD_DIM), 0.02),
        "k_norm_w": 1.0 + n(ks[11], (DEPTH, HEAD_DIM), 0.02),
        "w_branch_sb": n(ks[12], (DEPTH, W_SB, D_MODEL), W_SB ** -0.5),
        "w_branch_fox": n(ks[13], (DEPTH, W_FOX, D_MODEL), W_FOX ** -0.5),
        "w_out": n(ks[14], (DEPTH, D_MODEL, D_MODEL), D_MODEL ** -0.5),
    }


def reference(x_prompt, x_sample, cache_sb_k, cache_sb_v, cache_fox_k, cache_fox_v, cache_fox_logf,
              norm_w, w_in, b_forget, q_norm_w, k_norm_w, w_branch_sb, w_branch_fox, w_out):
    y_prompt, y_sample = x_prompt, x_sample
    p_sbk, p_sbv, p_fxk, p_fxv, p_lf = [], [], [], [], []
    s_sbk, s_sbv, s_fxk, s_fxv, s_lf = [], [], [], [], []
    for l in range(DEPTH):
        wl = (norm_w[l], w_in[l], b_forget[l], q_norm_w[l], k_norm_w[l],
              w_branch_sb[l], w_branch_fox[l], w_out[l])
        y_prompt, a, b, c, d, e = hybrid_layer(y_prompt, None, None, None, None, None, *wl)
        p_sbk.append(a); p_sbv.append(b); p_fxk.append(c); p_fxv.append(d); p_lf.append(e)
        y_sample, a, b, c, d, e = hybrid_layer(y_sample, cache_sb_k[l], cache_sb_v[l], cache_fox_k[l],
                                               cache_fox_v[l], cache_fox_logf[l], *wl)
        s_sbk.append(a); s_sbv.append(b); s_fxk.append(c); s_fxv.append(d); s_lf.append(e)
    return (y_prompt, y_sample,
            jnp.stack(p_sbk), jnp.stack(p_sbv), jnp.stack(p_fxk), jnp.stack(p_fxv), jnp.stack(p_lf),
            jnp.stack(s_sbk), jnp.stack(s_sbv), jnp.stack(s_fxk), jnp.stack(s_fxv), jnp.stack(s_lf))
```

```python
import functools

import jax
import jax.numpy as jnp
from jax import lax
from jax.experimental import pallas as pl
from jax.experimental.pallas import tpu as pltpu

F32 = jnp.float32
BF16 = jnp.bfloat16

D_MODEL = 2048
N_HEADS = 8
HEAD_DIM = 128
W_ATT = N_HEADS * HEAD_DIM
RMS_EPS = 1e-6
QK_SCALE = HEAD_DIM ** -0.5
LANES = 128
SUBLANES = 8
SB_SKIP_LOG = -104.0
NEG_BIG = -1e30
MIB = 1024 * 1024


def _cparams(sem, vmem_mib=None):
    kw = dict(dimension_semantics=sem)
    if vmem_mib is not None:
        kw["vmem_limit_bytes"] = vmem_mib * MIB
    return pltpu.CompilerParams(**kw)


def _softplus_neg_abs(z):
    return jnp.log1p(jnp.exp(-jnp.abs(z)))


def _log_sigmoid(z):
    return jnp.minimum(z, 0.0) - _softplus_neg_abs(z)


def _split_bf16(x, n):
    parts = []
    r = x
    for _ in range(n - 1):
        p = r.astype(BF16)
        parts.append(p)
        r = r - p.astype(F32)
    parts.append(r.astype(BF16))
    return parts


def _dot(a, b):
    return jnp.dot(a, b, preferred_element_type=F32)


def _dot_nt(a, b):
    return lax.dot_general(a, b, (((1,), (1,)), ((), ())), preferred_element_type=F32)


def _dot_split_lhs(x, m, n):
    acc = None
    for p in _split_bf16(x, n):
        t = _dot(p, m)
        acc = t if acc is None else acc + t
    return acc


def _dot_split_rhs(m, x, n):
    acc = None
    for p in _split_bf16(x, n):
        t = _dot(m, p)
        acc = t if acc is None else acc + t
    return acc


def _tri(n, kind):
    r = lax.broadcasted_iota(jnp.int32, (n, n), 0)
    c = lax.broadcasted_iota(jnp.int32, (n, n), 1)
    if kind == "row_gt_col":
        m = r > c
    elif kind == "row_le_col":
        m = r <= c
    elif kind == "row_ge_col":
        m = r >= c
    else:
        raise ValueError(kind)
    return m.astype(BF16)


def _rmsnorm_body(x_ref, w_ref, o_ref):
    x = x_ref[...]
    ms = jnp.mean(x * x, axis=-1, keepdims=True)
    o_ref[...] = (x * lax.rsqrt(ms + RMS_EPS) * w_ref[...]).astype(o_ref.dtype)


def _rmsnorm(x2d, w_row):
    m, d = x2d.shape
    tm = min(m, 512)
    return pl.pallas_call(
        _rmsnorm_body,
        grid=(m // tm,),
        in_specs=[pl.BlockSpec((tm, d), lambda i: (i, 0)),
                  pl.BlockSpec((1, d), lambda i: (0, 0))],
        out_specs=pl.BlockSpec((tm, d), lambda i: (i, 0)),
        out_shape=jax.ShapeDtypeStruct((m, d), BF16),
        compiler_params=_cparams(("parallel",)),
        name="rmsnorm",
    )(x2d, w_row)


def _head_rmsnorm(acc, nw):
    parts = []
    for hh in range(N_HEADS):
        a = acc[:, hh * HEAD_DIM:(hh + 1) * HEAD_DIM]
        ms = jnp.mean(a * a, axis=-1, keepdims=True)
        parts.append(a * lax.rsqrt(ms + RMS_EPS))
    return jnp.concatenate(parts, axis=1) * nw


def _proj_body(*refs, kind):
    if kind in ("qnorm", "knorm"):
        h_ref, w_ref, nw_ref = refs[:3]
        outs = refs[3:]
    else:
        h_ref, w_ref = refs[:2]
        outs = refs[2:]
    acc = _dot(h_ref[...], w_ref[...])
    if kind == "scale":
        outs[0][...] = (acc * QK_SCALE).astype(BF16)
    elif kind == "kv":
        outs[0][...] = acc
        outs[1][...] = acc.astype(BF16)
    elif kind == "silu":
        outs[0][...] = (acc * jax.nn.sigmoid(acc)).astype(BF16)
    elif kind == "sigmoid":
        outs[0][...] = jax.nn.sigmoid(acc).astype(BF16)
    elif kind == "qnorm":
        outs[0][...] = (_head_rmsnorm(acc, nw_ref[...]) * QK_SCALE).astype(BF16)
    elif kind == "knorm":
        y = _head_rmsnorm(acc, nw_ref[...])
        outs[0][...] = y
        outs[1][...] = y.astype(BF16)
    else:
        raise ValueError(kind)


def _proj(h, w, col_block, kind, nw=None, n_tiles=1):
    m, d = h.shape
    tn = W_ATT
    tm = min(m, 1024)
    grid = (n_tiles, m // tm)
    in_specs = [pl.BlockSpec((tm, d), lambda j, i: (i, 0)),
                pl.BlockSpec((d, tn), lambda j, i: (0, col_block + j))]
    args = [h, w]
    if kind in ("qnorm", "knorm"):
        in_specs.append(pl.BlockSpec((1, tn), lambda j, i: (0, 0)))
        args.append(nw)
    n_out_cols = tn * n_tiles
    out_block = pl.BlockSpec((tm, tn), lambda j, i: (i, j))
    if kind in ("kv", "knorm"):
        out_shape = (jax.ShapeDtypeStruct((m, n_out_cols), F32),
                     jax.ShapeDtypeStruct((m, n_out_cols), BF16))
        out_specs = (out_block, out_block)
    else:
        out_shape = jax.ShapeDtypeStruct((m, n_out_cols), BF16)
        out_specs = out_block
    return pl.pallas_call(
        functools.partial(_proj_body, kind=kind),
        grid=grid,
        in_specs=in_specs,
        out_specs=out_specs,
        out_shape=out_shape,
        compiler_params=_cparams(("parallel", "parallel"), vmem_mib=48),
        name="proj_" + kind,
    )(*args)


def _logf_body(h_ref, wf_ref, wft_ref, brow_ref, bcol_ref, l_ref, u_ref,
               lf_ref, fcol_ref, frow_ref, ccol_s, crow_s):
    i = pl.program_id(0)
    tm = h_ref.shape[0]

    @pl.when(i == 0)
    def _():
        ccol_s[...] = jnp.zeros_like(ccol_s)
        crow_s[...] = jnp.zeros_like(crow_s)

    h = h_ref[...]
    lf = _log_sigmoid(_dot(h, wf_ref[...]) + brow_ref[...])
    lf_ref[...] = lf[:, :N_HEADS]
    f_col = _dot_split_rhs(l_ref[...], lf, 3) + ccol_s[...]
    fcol_ref[...] = f_col[:, :N_HEADS]
    ccol_s[...] = f_col[tm - 1:tm, :]
    lft = _log_sigmoid(_dot_nt(wft_ref[...], h) + bcol_ref[...])
    f_row = _dot_split_lhs(lft, u_ref[...], 3) + crow_s[...]
    frow_ref[...] = f_row[:N_HEADS, :]
    crow_s[...] = f_row[:, tm - 1:tm]


def _logf_prompt(h, wf_pad, wft_pad, b_row, b_col):
    m, d = h.shape
    tm = min(m, 512)
    l_mat = _tri(tm, "row_ge_col")
    u_mat = _tri(tm, "row_le_col")
    const = lambda i: (0, 0)
    return pl.pallas_call(
        _logf_body,
        grid=(m // tm,),
        in_specs=[pl.BlockSpec((tm, d), lambda i: (i, 0)),
                  pl.BlockSpec((d, LANES), const),
                  pl.BlockSpec((2 * SUBLANES, d), const),
                  pl.BlockSpec((1, LANES), const),
                  pl.BlockSpec((2 * SUBLANES, 1), const),
                  pl.BlockSpec((tm, tm), const),
                  pl.BlockSpec((tm, tm), const)],
        out_specs=(pl.BlockSpec((tm, N_HEADS), lambda i: (i, 0)),
                   pl.BlockSpec((tm, N_HEADS), lambda i: (i, 0)),
                   pl.BlockSpec((N_HEADS, tm), lambda i: (0, i))),
        out_shape=(jax.ShapeDtypeStruct((m, N_HEADS), F32),
                   jax.ShapeDtypeStruct((m, N_HEADS), F32),
                   jax.ShapeDtypeStruct((N_HEADS, m), F32)),
        scratch_shapes=[pltpu.VMEM((1, LANES), F32), pltpu.VMEM((2 * SUBLANES, 1), F32)],
        compiler_params=_cparams(("arbitrary",)),
        name="logf_prompt",
    )(h, wf_pad, wft_pad, b_row, b_col, l_mat, u_mat)


def _logf_sample_body(h_ref, wf_ref, wft_ref, brow_ref, bcol_ref, bu_ref, x_ref, ms_ref,
                      lf_ref, c_ref, r_ref, *, kb):
    h = h_ref[...]
    lf = _log_sigmoid(_dot(h, wf_ref[...]) + brow_ref[...])
    lf_ref[...] = lf[:, :N_HEADS]
    lft = _log_sigmoid(_dot_nt(wft_ref[...], h) + bcol_ref[...])
    c_ref[...] = _dot_split_lhs(lft, bu_ref[...], 3)[:N_HEADS, :]
    n_blocks = x_ref.shape[1] // kb
    carry = jnp.zeros((x_ref.shape[0], 1), F32)
    for blk in range(n_blocks - 1, -1, -1):
        x = x_ref[:, blk * kb:(blk + 1) * kb]
        cum = _dot_split_lhs(x, ms_ref[...], 3)
        r_ref[:, blk * kb:(blk + 1) * kb] = cum + carry
        carry = carry + cum[:, 0:1] + x[:, 0:1]


def _logf_sample(h_s, wf_pad, wft_pad, b_row, b_col, past_logf_t, dec_seq):
    n_rows, d = h_s.shape
    n_bh, p_len = past_logf_t.shape
    kb = min(p_len, 512)
    r = lax.broadcasted_iota(jnp.int32, (n_rows, n_rows), 0)
    c = lax.broadcasted_iota(jnp.int32, (n_rows, n_rows), 1)
    bu = ((r // dec_seq == c // dec_seq) & (r <= c)).astype(BF16)
    ms = _tri(kb, "row_gt_col")
    return pl.pallas_call(
        functools.partial(_logf_sample_body, kb=kb),
        out_shape=(jax.ShapeDtypeStruct((n_rows, N_HEADS), F32),
                   jax.ShapeDtypeStruct((N_HEADS, n_rows), F32),
                   jax.ShapeDtypeStruct((n_bh, p_len), F32)),
        compiler_params=pltpu.CompilerParams(vmem_limit_bytes=32 * MIB),
        name="logf_sample",
    )(h_s, wf_pad, wft_pad, b_row, b_col, bu, past_logf_t, ms)


SB_TQ = 256


def _sb_block(q, k, v, m_mat, carry, mask):
    z = _dot_nt(q, k)
    sp = _softplus_neg_abs(z)
    lsp = jnp.minimum(z, 0.0) - sp
    lsn = -jnp.maximum(z, 0.0) - sp
    if mask is not None:
        lsn = jnp.where(mask, lsn, 0.0)
    cum = _dot_split_lhs(lsn, m_mat, 2)
    if carry is not None:
        cum = cum + carry
    w = jnp.exp(lsp + cum)
    if mask is not None:
        w = jnp.where(mask, w, 0.0)
    o = _dot(w.astype(BF16), v)
    new_carry = cum[:, 0:1] + lsn[:, 0:1]
    return o, new_carry


def _sb_fast_body(q_ref, kd_ref, vd_ref, kp_ref, vp_ref, gs_ref, m_ref, a_ref, c_ref):
    i = pl.program_id(0)
    tq = q_ref.shape[0]
    has_prev = i > 0
    row = lax.broadcasted_iota(jnp.int32, (tq, tq), 0)
    col = lax.broadcasted_iota(jnp.int32, (tq, tq), 1)
    dmask = col < row
    pmask = jnp.logical_and(has_prev, col >= 0)
    lane = lax.broadcasted_iota(jnp.int32, (tq, LANES), 1)
    m_mat = m_ref[...]
    cacc = jnp.full((tq, LANES), NEG_BIG, F32)
    for hh in range(N_HEADS):
        sl = slice(hh * HEAD_DIM, (hh + 1) * HEAD_DIM)
        q = q_ref[:, sl]
        od, cd = _sb_block(q, kd_ref[:, sl], vd_ref[:, sl], m_mat, None, dmask)
        op, cp = _sb_block(q, kp_ref[:, sl], vp_ref[:, sl], m_mat, cd, pmask)
        a_ref[:, sl] = ((od + op) * gs_ref[:, sl].astype(F32)).astype(BF16)
        cacc = jnp.where(lane == hh, cp, cacc)
    c_ref[...] = cacc


def _sb_rest_body(q_ref, k_ref, v_ref, gs_ref, cin_ref, afast_ref, m_ref, a_ref, o_s, c_s):
    i = pl.program_id(0)
    s = pl.program_id(1)
    j = i - 2 - s
    tq = q_ref.shape[0]

    @pl.when(s == 0)
    def _():
        o_s[...] = jnp.zeros_like(o_s)
        c_s[...] = cin_ref[...]

    active = jnp.logical_and(j >= 0, jnp.max(c_s[...]) > SB_SKIP_LOG)

    @pl.when(active)
    def _():
        lane = lax.broadcasted_iota(jnp.int32, (tq, LANES), 1)
        m_mat = m_ref[...]
        c_all = c_s[...]
        cacc = c_all
        for hh in range(N_HEADS):
            sl = slice(hh * HEAD_DIM, (hh + 1) * HEAD_DIM)
            carry = jnp.sum(jnp.where(lane == hh, c_all, 0.0), axis=1, keepdims=True)
            o, cn = _sb_block(q_ref[:, sl], k_ref[:, sl], v_ref[:, sl], m_mat, carry, None)
            o_s[:, sl] += o
            cacc = jnp.where(lane == hh, cn, cacc)
        c_s[...] = cacc

    @pl.when(s == pl.num_programs(1) - 1)
    def _():
        a_ref[...] = (afast_ref[...].astype(F32) + o_s[...] * gs_ref[...].astype(F32)).astype(BF16)


def _sb_prompt(q, k, v, gs):
    t, w = q.shape
    tq = min(SB_TQ, t)
    nq = t // tq
    m_mat = _tri(tq, "row_gt_col")
    blk = lambda f: pl.BlockSpec((tq, w), f)
    a_fast, carry = pl.pallas_call(
        _sb_fast_body,
        grid=(nq,),
        in_specs=[blk(lambda i: (i, 0)), blk(lambda i: (i, 0)), blk(lambda i: (i, 0)),
                  blk(lambda i: (jnp.maximum(i - 1, 0), 0)), blk(lambda i: (jnp.maximum(i - 1, 0), 0)),
                  blk(lambda i: (i, 0)),
                  pl.BlockSpec((tq, tq), lambda i: (0, 0))],
        out_specs=(blk(lambda i: (i, 0)), pl.BlockSpec((tq, LANES), lambda i: (i, 0))),
        out_shape=(jax.ShapeDtypeStruct((t, w), BF16), jax.ShapeDtypeStruct((t, LANES), F32)),
        compiler_params=_cparams(("parallel",), vmem_mib=48),
        name="sb_prompt_fast",
    )(q, k, v, k, v, gs, m_mat)
    if nq <= 2:
        return a_fast

    def rest(a_fast, carry):
        kidx = lambda i, s: (jnp.maximum(i - 2 - s, 0), 0)
        return pl.pallas_call(
            _sb_rest_body,
            grid=(nq, nq - 2),
            in_specs=[blk(lambda i, s: (i, 0)), blk(kidx), blk(kidx), blk(lambda i, s: (i, 0)),
                      pl.BlockSpec((tq, LANES), lambda i, s: (i, 0)),
                      blk(lambda i, s: (i, 0)),
                      pl.BlockSpec((tq, tq), lambda i, s: (0, 0))],
            out_specs=blk(lambda i, s: (i, 0)),
            out_shape=jax.ShapeDtypeStruct((t, w), BF16),
            scratch_shapes=[pltpu.VMEM((tq, w), F32), pltpu.VMEM((tq, LANES), F32)],
            compiler_params=_cparams(("parallel", "arbitrary"), vmem_mib=48),
            name="sb_prompt_rest",
        )(q, k, v, gs, carry, a_fast, m_mat)

    need_rest = jnp.max(carry[2 * tq:, :]) > SB_SKIP_LOG
    return lax.cond(need_rest, rest, lambda a, c: a, a_fast, carry)


FOX_T = 512


def _fox_prompt_body(q_ref, k_ref, v_ref, fcol_ref, frow_ref, gs_ref, a_ref):
    hh = pl.program_id(0)
    qb = pl.program_id(1)
    tb = q_ref.shape[0]
    q = q_ref[...]
    lane8 = lax.broadcasted_iota(jnp.int32, fcol_ref.shape, 1)
    fq = jnp.sum(jnp.where(lane8 == hh, fcol_ref[...], 0.0), axis=1, keepdims=True)

    def scores(kb):
        start = pl.multiple_of(kb * tb, tb)
        k = k_ref[pl.ds(start, tb), :]
        fk = frow_ref[0, :, pl.ds(start, tb)]
        return _dot_nt(q, k) + (fq - fk), v_ref[pl.ds(start, tb), :]

    def update(s, v, carry):
        m, l, acc = carry
        m_new = jnp.maximum(m, jnp.max(s, axis=1, keepdims=True))
        alpha = jnp.exp(m - m_new)
        p = jnp.exp(s - m_new)
        l = alpha * l + jnp.sum(p, axis=1, keepdims=True)
        acc = alpha * acc + _dot(p.astype(BF16), v)
        return m_new, l, acc

    def body(kb, carry):
        s, v = scores(kb)
        return update(s, v, carry)

    init = (jnp.full((tb, 1), NEG_BIG, F32), jnp.zeros((tb, 1), F32), jnp.zeros((tb, HEAD_DIM), F32))
    carry = lax.fori_loop(0, qb, body, init)
    s, v = scores(qb)
    row = lax.broadcasted_iota(jnp.int32, (tb, tb), 0)
    col = lax.broadcasted_iota(jnp.int32, (tb, tb), 1)
    s = jnp.where(col <= row, s, NEG_BIG)
    _, l, acc = update(s, v, carry)
    a_ref[...] = ((acc / l) * gs_ref[...].astype(F32)).astype(BF16)


def _fox_prompt(q, k, v, f_col, f_row, gs):
    t, w = q.shape
    tb = min(FOX_T, t)
    f_row3 = f_row.reshape(N_HEADS, 1, t)
    return pl.pallas_call(
        _fox_prompt_body,
        grid=(N_HEADS, t // tb),
        in_specs=[pl.BlockSpec((tb, HEAD_DIM), lambda h, i: (i, h)),
                  pl.BlockSpec((t, HEAD_DIM), lambda h, i: (0, h)),
                  pl.BlockSpec((t, HEAD_DIM), lambda h, i: (0, h)),
                  pl.BlockSpec((tb, N_HEADS), lambda h, i: (i, 0)),
                  pl.BlockSpec((1, 1, t), lambda h, i: (h, 0, 0)),
                  pl.BlockSpec((tb, HEAD_DIM), lambda h, i: (i, h))],
        out_specs=pl.BlockSpec((tb, HEAD_DIM), lambda h, i: (i, h)),
        out_shape=jax.ShapeDtypeStruct((t, w), BF16),
        compiler_params=_cparams(("parallel", "parallel"), vmem_mib=48),
        name="fox_prompt",
    )(q, k, v, f_col, f_row3, gs)


DEC_KB = 512


def _expand_q(q_ref, qx_s, kn_ref, vn_ref, knp_s, vnp_s):
    n_tok = q_ref.shape[0]
    n_rows = n_tok * N_HEADS
    qf = q_ref[...].astype(F32)
    row = lax.broadcasted_iota(jnp.int32, (SUBLANES, W_ATT), 0)
    col = lax.broadcasted_iota(jnp.int32, (SUBLANES, W_ATT), 1)
    own = (col // HEAD_DIM) == row
    del n_rows
    for tkn in range(n_tok):
        rep = jnp.broadcast_to(qf[tkn:tkn + 1, :], (SUBLANES, W_ATT))
        qx_s[tkn * N_HEADS:(tkn + 1) * N_HEADS, :] = jnp.where(own, rep, 0.0)
    knp_s[...] = jnp.zeros_like(knp_s)
    vnp_s[...] = jnp.zeros_like(vnp_s)
    knp_s[0:n_tok, :] = kn_ref[...]
    vnp_s[0:n_tok, :] = vn_ref[...]


def _collapse_heads(o_full, out_s):
    n_tok = o_full.shape[0] // N_HEADS
    row = lax.broadcasted_iota(jnp.int32, (SUBLANES, W_ATT), 0)
    col = lax.broadcasted_iota(jnp.int32, (SUBLANES, W_ATT), 1)
    own = (col // HEAD_DIM) == row
    for tkn in range(n_tok):
        blk = jnp.where(own, o_full[tkn * N_HEADS:(tkn + 1) * N_HEADS, :], 0.0)
        out_s[tkn:tkn + 1, :] = jnp.sum(blk, axis=0, keepdims=True)


def _sb_dec_body(q_ref, kn_ref, vn_ref, kc_ref, vc_ref, gs_ref, mn_ref, mc_ref, a_ref,
                 qx_s, knp_s, vnp_s, acc_s, c_s, out_s):
    s = pl.program_id(1)
    n_rows = qx_s.shape[0]

    @pl.when(s == 0)
    def _():
        _expand_q(q_ref, qx_s, kn_ref, vn_ref, knp_s, vnp_s)
        row = lax.broadcasted_iota(jnp.int32, (n_rows, LANES), 0)
        col = lax.broadcasted_iota(jnp.int32, (n_rows, LANES), 1)
        mask = col < row // N_HEADS
        o, carry = _sb_block(qx_s[...].astype(BF16), knp_s[...], vnp_s[...], mn_ref[...], None, mask)
        acc_s[...] = o
        c_s[...] = carry

    @pl.when(jnp.max(c_s[...]) > SB_SKIP_LOG)
    def _():
        k = kc_ref[0].astype(BF16)
        v = vc_ref[0].astype(BF16)
        o, carry = _sb_block(qx_s[...].astype(BF16), k, v, mc_ref[...], c_s[...], None)
        acc_s[...] += o
        c_s[...] = carry

    @pl.when(s == pl.num_programs(1) - 1)
    def _():
        _collapse_heads(acc_s[...], out_s)
        a_ref[...] = (out_s[...] * gs_ref[...].astype(F32)).astype(BF16)


def _sb_decode(q, k_new, v_new, k_cache, v_cache, gs, dec_seq):
    n_rows_all, w = q.shape
    n_b = n_rows_all // dec_seq
    p_len = k_cache.shape[1]
    kb = min(DEC_KB, p_len)
    nkb = p_len // kb
    n_rows = dec_seq * N_HEADS
    tok = lambda b, s: (b, 0)
    cache = lambda b, s: (b, nkb - 1 - s, 0)
    return pl.pallas_call(
        _sb_dec_body,
        grid=(n_b, nkb),
        in_specs=[pl.BlockSpec((dec_seq, w), tok), pl.BlockSpec((dec_seq, w), tok),
                  pl.BlockSpec((dec_seq, w), tok),
                  pl.BlockSpec((1, kb, w), cache), pl.BlockSpec((1, kb, w), cache),
                  pl.BlockSpec((dec_seq, w), tok),
                  pl.BlockSpec((LANES, LANES), lambda b, s: (0, 0)),
                  pl.BlockSpec((kb, kb), lambda b, s: (0, 0))],
        out_specs=pl.BlockSpec((dec_seq, w), tok),
        out_shape=jax.ShapeDtypeStruct((n_rows_all, w), BF16),
        scratch_shapes=[pltpu.VMEM((n_rows, w), F32), pltpu.VMEM((LANES, w), BF16),
                        pltpu.VMEM((LANES, w), BF16), pltpu.VMEM((n_rows, w), F32),
                        pltpu.VMEM((n_rows, 1), F32), pltpu.VMEM((dec_seq, w), F32)],
        compiler_params=_cparams(("parallel", "arbitrary"), vmem_mib=48),
        name="sb_decode",
    )(q, k_new, v_new, k_cache, v_cache, gs, _tri(LANES, "row_gt_col"), _tri(kb, "row_gt_col"))


def _fox_dec_body(q_ref, kn_ref, vn_ref, kc_ref, vc_ref, gs_ref, cn_ref, r_ref, a_ref,
                  qx_s, knp_s, vnp_s, acc_s, m_s, l_s, fq_s, out_s):
    s = pl.program_id(1)
    n_rows = qx_s.shape[0]
    n_tok = n_rows // N_HEADS

    def update(sc, v):
        m = m_s[...]
        m_new = jnp.maximum(m, jnp.max(sc, axis=1, keepdims=True))
        alpha = jnp.exp(m - m_new)
        p = jnp.exp(sc - m_new)
        l_s[...] = alpha * l_s[...] + jnp.sum(p, axis=1, keepdims=True)
        acc_s[...] = alpha * acc_s[...] + _dot(p.astype(BF16), v)
        m_s[...] = m_new

    @pl.when(s == 0)
    def _():
        _expand_q(q_ref, qx_s, kn_ref, vn_ref, knp_s, vnp_s)
        m_s[...] = jnp.full_like(m_s, NEG_BIG)
        l_s[...] = jnp.zeros_like(l_s)
        acc_s[...] = jnp.zeros_like(acc_s)
        row = lax.broadcasted_iota(jnp.int32, (n_rows, LANES), 0)
        col = lax.broadcasted_iota(jnp.int32, (n_rows, LANES), 1)
        cn = jnp.concatenate([cn_ref[0]] * n_tok, axis=0)
        fq = jnp.sum(jnp.where(col == row // N_HEADS, cn, 0.0), axis=1, keepdims=True)
        fq_s[...] = fq
        sc = _dot_nt(qx_s[...].astype(BF16), knp_s[...]) + (fq - cn)
        sc = jnp.where(col <= row // N_HEADS, sc, NEG_BIG)
        update(sc, vnp_s[...])

    k = kc_ref[0].astype(BF16)
    v = vc_ref[0].astype(BF16)
    r = jnp.concatenate([r_ref[...]] * n_tok, axis=0)
    update(_dot_nt(qx_s[...].astype(BF16), k) + (fq_s[...] + r), v)

    @pl.when(s == pl.num_programs(1) - 1)
    def _():
        _collapse_heads(acc_s[...] / l_s[...], out_s)
        a_ref[...] = (out_s[...] * gs_ref[...].astype(F32)).astype(BF16)


def _fox_decode(q, k_new, v_new, k_cache, v_cache, gs, c_new, r_past, dec_seq):
    n_rows_all, w = q.shape
    n_b = n_rows_all // dec_seq
    p_len = k_cache.shape[1]
    kb = min(DEC_KB, p_len)
    nkb = p_len // kb
    n_rows = dec_seq * N_HEADS
    tok = lambda b, s: (b, 0)
    cache = lambda b, s: (b, s, 0)
    return pl.pallas_call(
        _fox_dec_body,
        grid=(n_b, nkb),
        in_specs=[pl.BlockSpec((dec_seq, w), tok), pl.BlockSpec((dec_seq, w), tok),
                  pl.BlockSpec((dec_seq, w), tok),
                  pl.BlockSpec((1, kb, w), cache), pl.BlockSpec((1, kb, w), cache),
                  pl.BlockSpec((dec_seq, w), tok),
                  pl.BlockSpec((1, N_HEADS, LANES), lambda b, s: (b, 0, 0)),
                  pl.BlockSpec((N_HEADS, kb), lambda b, s: (b, s))],
        out_specs=pl.BlockSpec((dec_seq, w), tok),
        out_shape=jax.ShapeDtypeStruct((n_rows_all, w), BF16),
        scratch_shapes=[pltpu.VMEM((n_rows, w), F32), pltpu.VMEM((LANES, w), BF16),
                        pltpu.VMEM((LANES, w), BF16), pltpu.VMEM((n_rows, w), F32),
                        pltpu.VMEM((n_rows, 1), F32), pltpu.VMEM((n_rows, 1), F32),
                        pltpu.VMEM((n_rows, 1), F32), pltpu.VMEM((dec_seq, w), F32)],
        compiler_params=_cparams(("parallel", "arbitrary"), vmem_mib=48),
        name="fox_decode",
    )(q, k_new, v_new, k_cache, v_cache, gs, c_new, r_past)


def _post_body(x_ref, asb_ref, afx_ref, msb_ref, mfx_ref, wsb_ref, wfx_ref, wo_ref, y_ref):
    u_sb = _dot(asb_ref[...], wsb_ref[...])
    u_fx = _dot(afx_ref[...], wfx_ref[...])
    merged = msb_ref[...].astype(F32) * u_sb + mfx_ref[...].astype(F32) * u_fx
    y_ref[...] = x_ref[...] + _dot(merged.astype(BF16), wo_ref[...])


def _post(x2d, a_sb, a_fx, m_sig, w_sb, w_fx, w_o):
    m, d = x2d.shape
    w = a_sb.shape[1]
    tm = min(m, 256)
    row = lambda i: (i, 0)
    const = lambda i: (0, 0)
    resident = functools.partial(pl.BlockSpec, index_map=const, pipeline_mode=pl.Buffered(1))
    return pl.pallas_call(
        _post_body,
        grid=(m // tm,),
        in_specs=[pl.BlockSpec((tm, d), row), pl.BlockSpec((tm, w), row), pl.BlockSpec((tm, w), row),
                  pl.BlockSpec((tm, d), lambda i: (i, 0)), pl.BlockSpec((tm, d), lambda i: (i, 1)),
                  resident((w, d)), resident((w, d)), resident((d, d))],
        out_specs=pl.BlockSpec((tm, d), row),
        out_shape=jax.ShapeDtypeStruct((m, d), F32),
        compiler_params=_cparams(("parallel",), vmem_mib=56),
        name="post",
    )(x2d, a_sb, a_fx, m_sig, m_sig, w_sb, w_fx, w_o)


def _project_all(h, w_main, w_m, qnw, knw):
    q_sb = _proj(h, w_main, 0, "scale")
    k_sb, k_sb_b = _proj(h, w_main, 1, "kv")
    v_sb, v_sb_b = _proj(h, w_main, 2, "kv")
    g_sb = _proj(h, w_main, 3, "silu")
    q_fx = _proj(h, w_main, 4, "qnorm", nw=qnw)
    k_fx, k_fx_b = _proj(h, w_main, 5, "knorm", nw=knw)
    v_fx, v_fx_b = _proj(h, w_main, 6, "kv")
    g_fx = _proj(h, w_main, 7, "silu")
    m_sig = _proj(h, w_m, 0, "sigmoid", n_tiles=w_m.shape[1] // W_ATT)
    return (q_sb, k_sb, k_sb_b, v_sb, v_sb_b, g_sb, q_fx, k_fx, k_fx_b, v_fx, v_fx_b, g_fx, m_sig)


def kernel(x_prompt, x_sample, cache_sb_k, cache_sb_v, cache_fox_k, cache_fox_v, cache_fox_logf,
           norm_w, w_in, b_forget, q_norm_w, k_norm_w, w_branch_sb, w_branch_fox, w_out):
    depth = norm_w.shape[0]
    assert depth == 1, "single-layer step"
    bsz, seq, d = x_prompt.shape
    assert bsz == 1
    n_dec, dec_seq, _ = x_sample.shape
    p_len = cache_sb_k.shape[2]
    n_main = 8 * W_ATT

    w_l = w_in[0]
    w_main = w_l[:, :n_main].astype(BF16)
    w_f = w_l[:, n_main:n_main + N_HEADS].astype(BF16)
    w_m = w_l[:, n_main + N_HEADS:].astype(BF16)
    wf_pad = jnp.pad(w_f, ((0, 0), (0, LANES - N_HEADS)))
    wft_pad = jnp.pad(w_f.T, ((0, 2 * SUBLANES - N_HEADS), (0, 0)))
    b_row = jnp.pad(b_forget[0].astype(F32)[None, :], ((0, 0), (0, LANES - N_HEADS)))
    b_col = jnp.pad(b_forget[0].astype(F32)[:, None], ((0, 2 * SUBLANES - N_HEADS), (0, 0)))
    qnw = jnp.tile(q_norm_w[0].astype(F32), N_HEADS)[None, :]
    knw = jnp.tile(k_norm_w[0].astype(F32), N_HEADS)[None, :]
    nw_row = norm_w[0].astype(F32)[None, :]
    w_sb = w_branch_sb[0].astype(BF16)
    w_fx = w_branch_fox[0].astype(BF16)
    w_o = w_out[0].astype(BF16)

    xp = x_prompt.reshape(seq, d)
    hp = _rmsnorm(xp, nw_row)
    (q_sb, k_sb, k_sb_b, v_sb, v_sb_b, g_sb, q_fx, k_fx, k_fx_b, v_fx, v_fx_b, g_fx,
     m_sig) = _project_all(hp, w_main, w_m, qnw, knw)
    logf, f_col, f_row = _logf_prompt(hp, wf_pad, wft_pad, b_row, b_col)
    a_sb = _sb_prompt(q_sb, k_sb_b, v_sb_b, g_sb)
    a_fx = _fox_prompt(q_fx, k_fx_b, v_fx_b, f_col, f_row, g_fx)
    y_prompt = _post(xp, a_sb, a_fx, m_sig, w_sb, w_fx, w_o).reshape(bsz, seq, d)

    xs = x_sample.reshape(n_dec * dec_seq, d)
    hs = _rmsnorm(xs, nw_row)
    (sq_sb, sk_sb, sk_sb_b, sv_sb, sv_sb_b, sg_sb, sq_fx, sk_fx, sk_fx_b, sv_fx, sv_fx_b, sg_fx,
     sm_sig) = _project_all(hs, w_main, w_m, qnw, knw)
    past_logf_t = jnp.transpose(cache_fox_logf[0].astype(F32), (0, 2, 1)).reshape(n_dec * N_HEADS, p_len)
    s_logf, c_all, r_past = _logf_sample(hs, wf_pad, wft_pad, b_row, b_col, past_logf_t, dec_seq)
    c_new = jnp.transpose(c_all.reshape(N_HEADS, n_dec, dec_seq), (1, 0, 2))
    c_new = jnp.pad(c_new, ((0, 0), (0, 0), (0, LANES - dec_seq)))
    kc_sb = cache_sb_k[0].reshape(n_dec, p_len, W_ATT)
    vc_sb = cache_sb_v[0].reshape(n_dec, p_len, W_ATT)
    kc_fx = cache_fox_k[0].reshape(n_dec, p_len, W_ATT)
    vc_fx = cache_fox_v[0].reshape(n_dec, p_len, W_ATT)
    sa_sb = _sb_decode(sq_sb, sk_sb_b, sv_sb_b, kc_sb, vc_sb, sg_sb, dec_seq)
    sa_fx = _fox_decode(sq_fx, sk_fx_b, sv_fx_b, kc_fx, vc_fx, sg_fx, c_new, r_past, dec_seq)
    y_sample = _post(xs, sa_sb, sa_fx, sm_sig, w_sb, w_fx, w_o).reshape(n_dec, dec_seq, d)

    hd = (N_HEADS, HEAD_DIM)
    return (y_prompt, y_sample,
            k_sb.reshape(1, bsz, seq, *hd), v_sb.reshape(1, bsz, seq, *hd),
            k_fx.reshape(1, bsz, seq, *hd), v_fx.reshape(1, bsz, seq, *hd),
            logf.reshape(1, bsz, seq, N_HEADS),
            sk_sb.reshape(1, n_dec, dec_seq, *hd), sv_sb.reshape(1, n_dec, dec_seq, *hd),
            sk_fx.reshape(1, n_dec, dec_seq, *hd), sv_fx.reshape(1, n_dec, dec_seq, *hd),
            s_logf.reshape(1, n_dec, dec_seq, N_HEADS))
```

```python
import functools

import jax
import jax.numpy as jnp
from jax import lax
from jax.experimental import pallas as pl
from jax.experimental.pallas import tpu as pltpu

F32 = jnp.float32
BF16 = jnp.bfloat16

D_MODEL = 2048
N_HEADS = 8
HEAD_DIM = 128
W_ATT = N_HEADS * HEAD_DIM
RMS_EPS = 1e-6
QK_SCALE = HEAD_DIM ** -0.5
LOG2E = 1.4426950408889634
LANES = 128
SUBLANES = 8
SB_SKIP_LOG = -104.0
NEG_BIG = -1e30
MIB = 1024 * 1024


def _cparams(sem, vmem_mib=None):
    kw = dict(dimension_semantics=sem)
    if vmem_mib is not None:
        kw["vmem_limit_bytes"] = vmem_mib * MIB
    return pltpu.CompilerParams(**kw)


def _softplus_neg_abs(z):
    return jnp.log1p(jnp.exp(-jnp.abs(z)))


def _log_sigmoid(z):
    return jnp.minimum(z, 0.0) - _softplus_neg_abs(z)


def _split_bf16(x, n):
    parts = []
    r = x
    for _ in range(n - 1):
        p = r.astype(BF16)
        parts.append(p)
        r = r - p.astype(F32)
    parts.append(r.astype(BF16))
    return parts


def _dot(a, b):
    return jnp.dot(a, b, preferred_element_type=F32)


def _dot_nt(a, b):
    return lax.dot_general(a, b, (((1,), (1,)), ((), ())), preferred_element_type=F32)


def _dot_split_lhs(x, m, n):
    acc = None
    for p in _split_bf16(x, n):
        t = _dot(p, m)
        acc = t if acc is None else acc + t
    return acc


def _dot_split_rhs(m, x, n):
    acc = None
    for p in _split_bf16(x, n):
        t = _dot(m, p)
        acc = t if acc is None else acc + t
    return acc


def _tri(n, kind):
    r = lax.broadcasted_iota(jnp.int32, (n, n), 0)
    c = lax.broadcasted_iota(jnp.int32, (n, n), 1)
    if kind == "row_gt_col":
        m = r > c
    elif kind == "row_le_col":
        m = r <= c
    elif kind == "row_ge_col":
        m = r >= c
    else:
        raise ValueError(kind)
    return m.astype(BF16)


def _rmsnorm_body(x_ref, w_ref, o_ref):
    x = x_ref[...]
    ms = jnp.mean(x * x, axis=-1, keepdims=True)
    o_ref[...] = (x * lax.rsqrt(ms + RMS_EPS) * w_ref[...]).astype(o_ref.dtype)


def _rmsnorm(x2d, w_row):
    m, d = x2d.shape
    tm = min(m, 512)
    return pl.pallas_call(
        _rmsnorm_body,
        grid=(m // tm,),
        in_specs=[pl.BlockSpec((tm, d), lambda i: (i, 0)),
                  pl.BlockSpec((1, d), lambda i: (0, 0))],
        out_specs=pl.BlockSpec((tm, d), lambda i: (i, 0)),
        out_shape=jax.ShapeDtypeStruct((m, d), BF16),
        compiler_params=_cparams(("parallel",)),
        name="rmsnorm",
    )(x2d, w_row)


def _head_rmsnorm(acc, nw):
    parts = []
    for hh in range(N_HEADS):
        a = acc[:, hh * HEAD_DIM:(hh + 1) * HEAD_DIM]
        ms = jnp.mean(a * a, axis=-1, keepdims=True)
        parts.append(a * lax.rsqrt(ms + RMS_EPS))
    return jnp.concatenate(parts, axis=1) * nw


def _proj_body(*refs, kind):
    if kind in ("qnorm", "knorm"):
        h_ref, w_ref, nw_ref = refs[:3]
        outs = refs[3:-1]
    else:
        h_ref, w_ref = refs[:2]
        outs = refs[2:-1]
    wb_s = refs[-1]

    @pl.when(pl.program_id(1) == 0)
    def _():
        wb_s[...] = w_ref[...].astype(BF16)

    acc = _dot(h_ref[...], wb_s[...])
    if kind == "scale":
        outs[0][...] = (acc * QK_SCALE).astype(BF16)
    elif kind == "kv":
        outs[0][...] = acc
        outs[1][...] = acc.astype(BF16)
    elif kind == "silu":
        outs[0][...] = (acc * jax.nn.sigmoid(acc)).astype(BF16)
    elif kind == "sigmoid":
        outs[0][...] = jax.nn.sigmoid(acc).astype(BF16)
    elif kind == "qnorm":
        outs[0][...] = (_head_rmsnorm(acc, nw_ref[...]) * (QK_SCALE * LOG2E)).astype(BF16)
    elif kind == "knorm":
        y = _head_rmsnorm(acc, nw_ref[...])
        outs[0][...] = y
        outs[1][...] = y.astype(BF16)
    else:
        raise ValueError(kind)


def _proj(h, w, col_block, kind, nw=None, n_tiles=1):
    m, d = h.shape
    tn = W_ATT
    tm = min(m, 1024)
    grid = (n_tiles, m // tm)
    w_mode = {} if n_tiles > 1 else dict(pipeline_mode=pl.Buffered(1))
    in_specs = [pl.BlockSpec((tm, d), lambda j, i: (i, 0)),
                pl.BlockSpec((d, tn), lambda j, i: (0, col_block + j), **w_mode)]
    args = [h, w]
    if kind in ("qnorm", "knorm"):
        in_specs.append(pl.BlockSpec((1, tn), lambda j, i: (0, 0)))
        args.append(nw)
    n_out_cols = tn * n_tiles
    out_block = pl.BlockSpec((tm, tn), lambda j, i: (i, j))
    if kind in ("kv", "knorm"):
        out_shape = (jax.ShapeDtypeStruct((m, n_out_cols), F32),
                     jax.ShapeDtypeStruct((m, n_out_cols), BF16))
        out_specs = (out_block, out_block)
    else:
        out_shape = jax.ShapeDtypeStruct((m, n_out_cols), BF16)
        out_specs = out_block
    return pl.pallas_call(
        functools.partial(_proj_body, kind=kind),
        grid=grid,
        in_specs=in_specs,
        out_specs=out_specs,
        out_shape=out_shape,
        scratch_shapes=[pltpu.VMEM((d, tn), BF16)],
        compiler_params=_cparams(("parallel", "arbitrary"), vmem_mib=56),
        name="proj_" + kind,
    )(*args)


def _logf_body(h_ref, wf_ref, wft_ref, brow_ref, bcol_ref, l_ref, u_ref,
               lf_ref, fcol_ref, frow_ref, ccol_s, crow_s):
    i = pl.program_id(0)
    tm = h_ref.shape[0]

    @pl.when(i == 0)
    def _():
        ccol_s[...] = jnp.zeros_like(ccol_s)
        crow_s[...] = jnp.zeros_like(crow_s)

    h = h_ref[...]
    lf = _log_sigmoid(_dot(h, wf_ref[...]) + brow_ref[...])
    lf_ref[...] = lf[:, :N_HEADS]
    f_col = _dot_split_rhs(l_ref[...], lf, 3) + ccol_s[...]
    fcol_ref[...] = f_col[:, :N_HEADS]
    ccol_s[...] = f_col[tm - 1:tm, :]
    lft = _log_sigmoid(_dot_nt(wft_ref[...], h) + bcol_ref[...])
    f_row = _dot_split_lhs(lft, u_ref[...], 3) + crow_s[...]
    frow_ref[...] = f_row[:N_HEADS, :]
    crow_s[...] = f_row[:, tm - 1:tm]


def _logf_prompt(h, wf_pad, wft_pad, b_row, b_col):
    m, d = h.shape
    tm = min(m, 512)
    l_mat = _tri(tm, "row_ge_col")
    u_mat = _tri(tm, "row_le_col")
    const = lambda i: (0, 0)
    return pl.pallas_call(
        _logf_body,
        grid=(m // tm,),
        in_specs=[pl.BlockSpec((tm, d), lambda i: (i, 0)),
                  pl.BlockSpec((d, LANES), const),
                  pl.BlockSpec((2 * SUBLANES, d), const),
                  pl.BlockSpec((1, LANES), const),
                  pl.BlockSpec((2 * SUBLANES, 1), const),
                  pl.BlockSpec((tm, tm), const),
                  pl.BlockSpec((tm, tm), const)],
        out_specs=(pl.BlockSpec((tm, N_HEADS), lambda i: (i, 0)),
                   pl.BlockSpec((tm, N_HEADS), lambda i: (i, 0)),
                   pl.BlockSpec((N_HEADS, tm), lambda i: (0, i))),
        out_shape=(jax.ShapeDtypeStruct((m, N_HEADS), F32),
                   jax.ShapeDtypeStruct((m, N_HEADS), F32),
                   jax.ShapeDtypeStruct((N_HEADS, m), F32)),
        scratch_shapes=[pltpu.VMEM((1, LANES), F32), pltpu.VMEM((2 * SUBLANES, 1), F32)],
        compiler_params=_cparams(("arbitrary",)),
        name="logf_prompt",
    )(h, wf_pad, wft_pad, b_row, b_col, l_mat, u_mat)


def _logf_sample_body(h_ref, wf_ref, wft_ref, brow_ref, bcol_ref, bu_ref, x_ref, ms_ref,
                      lf_ref, c_ref, r_ref, *, kb):
    h = h_ref[...]
    lf = _log_sigmoid(_dot(h, wf_ref[...]) + brow_ref[...])
    lf_ref[...] = lf[:, :N_HEADS]
    lft = _log_sigmoid(_dot_nt(wft_ref[...], h) + bcol_ref[...])
    c_ref[...] = _dot_split_lhs(lft, bu_ref[...], 3)[:N_HEADS, :]
    n_blocks = x_ref.shape[1] // kb
    carry = jnp.zeros((x_ref.shape[0], 1), F32)
    for blk in range(n_blocks - 1, -1, -1):
        x = x_ref[:, blk * kb:(blk + 1) * kb]
        cum = _dot_split_lhs(x, ms_ref[...], 3)
        r_ref[:, blk * kb:(blk + 1) * kb] = cum + carry
        carry = carry + cum[:, 0:1] + x[:, 0:1]


def _logf_sample(h_s, wf_pad, wft_pad, b_row, b_col, past_logf_t, dec_seq):
    n_rows, d = h_s.shape
    n_bh, p_len = past_logf_t.shape
    kb = min(p_len, 512)
    r = lax.broadcasted_iota(jnp.int32, (n_rows, n_rows), 0)
    c = lax.broadcasted_iota(jnp.int32, (n_rows, n_rows), 1)
    bu = ((r // dec_seq == c // dec_seq) & (r <= c)).astype(BF16)
    ms = _tri(kb, "row_gt_col")
    return pl.pallas_call(
        functools.partial(_logf_sample_body, kb=kb),
        out_shape=(jax.ShapeDtypeStruct((n_rows, N_HEADS), F32),
                   jax.ShapeDtypeStruct((N_HEADS, n_rows), F32),
                   jax.ShapeDtypeStruct((n_bh, p_len), F32)),
        compiler_params=pltpu.CompilerParams(vmem_limit_bytes=32 * MIB),
        name="logf_sample",
    )(h_s, wf_pad, wft_pad, b_row, b_col, bu, past_logf_t, ms)


SB_TQ = 256


def _sb_block(q, k, v, m_mat, carry, mask):
    z = _dot_nt(q, k)
    sp = _softplus_neg_abs(z)
    lsp = jnp.minimum(z, 0.0) - sp
    lsn = -jnp.maximum(z, 0.0) - sp
    if mask is not None:
        lsn = jnp.where(mask, lsn, 0.0)
    cum = _dot_split_lhs(lsn, m_mat, 2)
    if carry is not None:
        cum = cum + carry
    w = jnp.exp(lsp + cum)
    if mask is not None:
        w = jnp.where(mask, w, 0.0)
    o = _dot(w.astype(BF16), v)
    new_carry = cum[:, 0:1] + lsn[:, 0:1]
    return o, new_carry


def _sb_fast_body(q_ref, kd_ref, vd_ref, kp_ref, vp_ref, gs_ref, m_ref, a_ref, c_ref):
    i = pl.program_id(0)
    tq = q_ref.shape[0]
    has_prev = i > 0
    row = lax.broadcasted_iota(jnp.int32, (tq, tq), 0)
    col = lax.broadcasted_iota(jnp.int32, (tq, tq), 1)
    dmask = col < row
    pmask = jnp.logical_and(has_prev, col >= 0)
    lane = lax.broadcasted_iota(jnp.int32, (tq, LANES), 1)
    m_mat = m_ref[...]
    cacc = jnp.full((tq, LANES), NEG_BIG, F32)
    for hh in range(N_HEADS):
        sl = slice(hh * HEAD_DIM, (hh + 1) * HEAD_DIM)
        q = q_ref[:, sl]
        od, cd = _sb_block(q, kd_ref[:, sl], vd_ref[:, sl], m_mat, None, dmask)
        op, cp = _sb_block(q, kp_ref[:, sl], vp_ref[:, sl], m_mat, cd, pmask)
        a_ref[:, sl] = ((od + op) * gs_ref[:, sl].astype(F32)).astype(BF16)
        cacc = jnp.where(lane == hh, cp, cacc)
    c_ref[...] = cacc


def _sb_rest_body(q_ref, k_ref, v_ref, gs_ref, cin_ref, afast_ref, m_ref, a_ref, o_s, c_s):
    i = pl.program_id(0)
    s = pl.program_id(1)
    j = i - 2 - s
    tq = q_ref.shape[0]

    @pl.when(s == 0)
    def _():
        o_s[...] = jnp.zeros_like(o_s)
        c_s[...] = cin_ref[...]

    active = jnp.logical_and(j >= 0, jnp.max(c_s[...]) > SB_SKIP_LOG)

    @pl.when(active)
    def _():
        lane = lax.broadcasted_iota(jnp.int32, (tq, LANES), 1)
        m_mat = m_ref[...]
        c_all = c_s[...]
        cacc = c_all
        for hh in range(N_HEADS):
            sl = slice(hh * HEAD_DIM, (hh + 1) * HEAD_DIM)
            carry = jnp.sum(jnp.where(lane == hh, c_all, 0.0), axis=1, keepdims=True)
            o, cn = _sb_block(q_ref[:, sl], k_ref[:, sl], v_ref[:, sl], m_mat, carry, None)
            o_s[:, sl] += o
            cacc = jnp.where(lane == hh, cn, cacc)
        c_s[...] = cacc

    @pl.when(s == pl.num_programs(1) - 1)
    def _():
        a_ref[...] = (afast_ref[...].astype(F32) + o_s[...] * gs_ref[...].astype(F32)).astype(BF16)


def _sb_prompt(q, k, v, gs):
    t, w = q.shape
    tq = min(SB_TQ, t)
    nq = t // tq
    m_mat = _tri(tq, "row_gt_col")
    blk = lambda f: pl.BlockSpec((tq, w), f)
    a_fast, carry = pl.pallas_call(
        _sb_fast_body,
        grid=(nq,),
        in_specs=[blk(lambda i: (i, 0)), blk(lambda i: (i, 0)), blk(lambda i: (i, 0)),
                  blk(lambda i: (jnp.maximum(i - 1, 0), 0)), blk(lambda i: (jnp.maximum(i - 1, 0), 0)),
                  blk(lambda i: (i, 0)),
                  pl.BlockSpec((tq, tq), lambda i: (0, 0))],
        out_specs=(blk(lambda i: (i, 0)), pl.BlockSpec((tq, LANES), lambda i: (i, 0))),
        out_shape=(jax.ShapeDtypeStruct((t, w), BF16), jax.ShapeDtypeStruct((t, LANES), F32)),
        compiler_params=_cparams(("parallel",), vmem_mib=48),
        name="sb_prompt_fast",
    )(q, k, v, k, v, gs, m_mat)
    if nq <= 2:
        return a_fast

    def rest(a_fast, carry):
        kidx = lambda i, s: (jnp.maximum(i - 2 - s, 0), 0)
        return pl.pallas_call(
            _sb_rest_body,
            grid=(nq, nq - 2),
            in_specs=[blk(lambda i, s: (i, 0)), blk(kidx), blk(kidx), blk(lambda i, s: (i, 0)),
                      pl.BlockSpec((tq, LANES), lambda i, s: (i, 0)),
                      blk(lambda i, s: (i, 0)),
                      pl.BlockSpec((tq, tq), lambda i, s: (0, 0))],
            out_specs=blk(lambda i, s: (i, 0)),
            out_shape=jax.ShapeDtypeStruct((t, w), BF16),
            scratch_shapes=[pltpu.VMEM((tq, w), F32), pltpu.VMEM((tq, LANES), F32)],
            compiler_params=_cparams(("parallel", "arbitrary"), vmem_mib=48),
            name="sb_prompt_rest",
        )(q, k, v, gs, carry, a_fast, m_mat)

    need_rest = jnp.max(carry[2 * tq:, :]) > SB_SKIP_LOG
    return lax.cond(need_rest, rest, lambda a, c: a, a_fast, carry)


FOX_T = 512
FOX_ZB_MAX = 40.0
FOX_SKIP_LOG = -104.0


def _head_column(blk, hh):
    lane8 = lax.broadcasted_iota(jnp.int32, blk.shape, 1)
    return jnp.sum(jnp.where(lane8 == hh, blk, 0.0), axis=1, keepdims=True)


def _aug_tile(f, f_first, lane):
    hi = f.astype(BF16).astype(F32)
    r1 = f - hi
    mid = r1.astype(BF16).astype(F32)
    low = (r1 - mid).astype(BF16).astype(F32)
    fb = 0 if f_first else 3
    ob = 3 if f_first else 0
    out = jnp.where(lane == fb, hi, jnp.where(lane == fb + 1, mid, jnp.where(lane == fb + 2, low, 0.0)))
    out = jnp.where(jnp.logical_and(lane >= ob, lane < ob + 3), 1.0, out)
    return out.astype(BF16)


def _fox_fast_body(lo_ref, q_ref, k_ref, v_ref, fcol_ref, c2_ref, gs_ref, a_ref, kaug_s):
    hh = pl.program_id(0)
    qb = pl.program_id(1)
    nq = pl.num_programs(1)
    tb = q_ref.shape[0]
    lane = lax.broadcasted_iota(jnp.int32, (tb, LANES), 1)

    def f2_col(start):
        return _head_column(fcol_ref[pl.ds(start, tb), :], hh) * LOG2E

    @pl.when(qb == 0)
    def _():
        def build(c, carry):
            start = pl.multiple_of(c * tb, tb)
            kaug_s[pl.ds(start, tb), :] = _aug_tile(-f2_col(start), False, lane)
            return carry
        lax.fori_loop(0, nq, build, 0)

    q_start = pl.multiple_of(qb * tb, tb)
    q2 = jnp.concatenate([q_ref[...], _aug_tile(f2_col(q_start) - c2_ref[...], True, lane)], axis=1)
    ones_blk = jnp.where(lane == 0, 1.0, 0.0).astype(BF16)

    row = lax.broadcasted_iota(jnp.int32, (tb, tb), 0)
    col = lax.broadcasted_iota(jnp.int32, (tb, tb), 1)
    causal = col <= row

    def probs(kb):
        start = pl.multiple_of(kb * tb, tb)
        k2 = jnp.concatenate([k_ref[pl.ds(start, tb), :], kaug_s[pl.ds(start, tb), :]], axis=1)
        p = jnp.exp2(_dot_nt(q2, k2))
        return jnp.where(jnp.logical_or(causal, kb < qb), p, 0.0).astype(BF16)

    def weighted(p, kb):
        start = pl.multiple_of(kb * tb, tb)
        v2 = jnp.concatenate([v_ref[pl.ds(start, tb), :], ones_blk], axis=1)
        return _dot(p, v2)

    lo = lo_ref[hh * nq + qb]

    def body(kb, carry):
        acc, p = carry
        p_next = probs(kb + 1)
        return acc + weighted(p, kb), p_next

    acc, p = lax.fori_loop(lo, qb, body, (jnp.zeros((tb, 2 * HEAD_DIM), F32), probs(lo)))
    acc = acc + weighted(p, qb)
    o = acc[:, :HEAD_DIM] / acc[:, HEAD_DIM:HEAD_DIM + 1]
    a_ref[...] = (o * gs_ref[...].astype(F32)).astype(BF16)


def _fox_slow_body(q_ref, k_ref, v_ref, fcol_ref, frow_ref, gs_ref, a_ref):
    hh = pl.program_id(0)
    qb = pl.program_id(1)
    tb = q_ref.shape[0]
    q = q_ref[...]
    fq = _head_column(fcol_ref[...], hh) * LOG2E

    def scores(kb):
        start = pl.multiple_of(kb * tb, tb)
        k = k_ref[pl.ds(start, tb), :]
        fk = frow_ref[0, :, pl.ds(start, tb)] * LOG2E
        return _dot_nt(q, k) + (fq - fk), v_ref[pl.ds(start, tb), :]

    def update(s, v, carry):
        m, l, acc = carry
        m_new = jnp.maximum(m, jnp.max(s, axis=1, keepdims=True))
        alpha = jnp.exp2(m - m_new)
        p = jnp.exp2(s - m_new)
        l = alpha * l + jnp.sum(p, axis=1, keepdims=True)
        acc = alpha * acc + _dot(p.astype(BF16), v)
        return m_new, l, acc

    def body(kb, carry):
        s, v = scores(kb)
        return update(s, v, carry)

    init = (jnp.full((tb, 1), NEG_BIG, F32), jnp.zeros((tb, 1), F32), jnp.zeros((tb, HEAD_DIM), F32))
    carry = lax.fori_loop(0, qb, body, init)
    s, v = scores(qb)
    row = lax.broadcasted_iota(jnp.int32, (tb, tb), 0)
    col = lax.broadcasted_iota(jnp.int32, (tb, tb), 1)
    s = jnp.where(col <= row, s, NEG_BIG)
    _, l, acc = update(s, v, carry)
    a_ref[...] = ((acc / l) * gs_ref[...].astype(F32)).astype(BF16)


def _fox_prompt(q, k, v, f_col, f_row, gs, zb):
    t, w = q.shape
    tb = min(FOX_T, t)
    nq = t // tb
    out_shape = jax.ShapeDtypeStruct((t, w), BF16)

    def fast(q, k, v, f_col, f_row, gs, zb):
        f_start = f_row[:, ::tb]
        f_end = f_row[:, tb - 1::tb]
        dead = (f_start[:, :, None] - f_end[:, None, :]) < FOX_SKIP_LOG
        lo = jnp.sum(dead, axis=2).astype(jnp.int32).reshape(-1)
        c2 = (zb * LOG2E).reshape(1, 1)
        head_blk = lambda h, i, lo: (i, h)
        head_all = lambda h, i, lo: (0, h)
        whole = lambda h, i, lo: (0, 0)
        grid_spec = pltpu.PrefetchScalarGridSpec(
            num_scalar_prefetch=1,
            grid=(N_HEADS, nq),
            in_specs=[pl.BlockSpec((tb, HEAD_DIM), head_blk),
                      pl.BlockSpec((t, HEAD_DIM), head_all),
                      pl.BlockSpec((t, HEAD_DIM), head_all),
                      pl.BlockSpec((t, N_HEADS), whole, pipeline_mode=pl.Buffered(1)),
                      pl.BlockSpec((1, 1), whole),
                      pl.BlockSpec((tb, HEAD_DIM), head_blk)],
            out_specs=pl.BlockSpec((tb, HEAD_DIM), head_blk),
            scratch_shapes=[pltpu.VMEM((t, LANES), BF16)])
        return pl.pallas_call(
            _fox_fast_body, grid_spec=grid_spec, out_shape=out_shape,
            compiler_params=_cparams(("parallel", "arbitrary"), vmem_mib=48),
            name="fox_prompt_fast",
        )(lo, q, k, v, f_col, c2, gs)

    def slow(q, k, v, f_col, f_row, gs, zb):
        f_row3 = f_row.reshape(N_HEADS, 1, t)
        return pl.pallas_call(
            _fox_slow_body,
            grid=(N_HEADS, nq),
            in_specs=[pl.BlockSpec((tb, HEAD_DIM), lambda h, i: (i, h)),
                      pl.BlockSpec((t, HEAD_DIM), lambda h, i: (0, h)),
                      pl.BlockSpec((t, HEAD_DIM), lambda h, i: (0, h)),
                      pl.BlockSpec((tb, N_HEADS), lambda h, i: (i, 0)),
                      pl.BlockSpec((1, 1, t), lambda h, i: (h, 0, 0)),
                      pl.BlockSpec((tb, HEAD_DIM), lambda h, i: (i, h))],
            out_specs=pl.BlockSpec((tb, HEAD_DIM), lambda h, i: (i, h)),
            out_shape=out_shape,
            compiler_params=_cparams(("parallel", "parallel"), vmem_mib=48),
            name="fox_prompt_slow",
        )(q, k, v, f_col, f_row3, gs)

    return lax.cond(zb <= FOX_ZB_MAX, fast, slow, q, k, v, f_col, f_row, gs, zb)


DEC_KB = 512
SB_DEC_WIN = 256


def _own_head_mask():
    row = lax.broadcasted_iota(jnp.int32, (SUBLANES, W_ATT), 0)
    col = lax.broadcasted_iota(jnp.int32, (SUBLANES, W_ATT), 1)
    return (col // HEAD_DIM) == row


def _expand_q(q_ref, qx_s):
    n_tok = q_ref.shape[0]
    qf = q_ref[...].astype(F32)
    own = _own_head_mask()
    for tkn in range(n_tok):
        rep = jnp.broadcast_to(qf[tkn:tkn + 1, :], (SUBLANES, W_ATT))
        qx_s[tkn * N_HEADS:(tkn + 1) * N_HEADS, :] = jnp.where(own, rep, 0.0)


def _pad_new(kn_ref, vn_ref, knp_s, vnp_s):
    n_tok = kn_ref.shape[0]
    knp_s[...] = jnp.zeros_like(knp_s)
    vnp_s[...] = jnp.zeros_like(vnp_s)
    knp_s[0:n_tok, :] = kn_ref[...]
    vnp_s[0:n_tok, :] = vn_ref[...]


def _gather_heads(c_ref, dst_s):
    n_keys = dst_s.shape[0]
    for hh in range(N_HEADS):
        dst_s[:, hh * HEAD_DIM:(hh + 1) * HEAD_DIM] = (
            c_ref[0, pl.ds(hh, n_keys, stride=N_HEADS), :].astype(BF16))


def _collapse_heads(o_full, out_s):
    n_tok = o_full.shape[0] // N_HEADS
    own = _own_head_mask()
    for tkn in range(n_tok):
        blk = jnp.where(own, o_full[tkn * N_HEADS:(tkn + 1) * N_HEADS, :], 0.0)
        out_s[tkn:tkn + 1, :] = jnp.sum(blk, axis=0, keepdims=True)


def _sb_dec_fast_body(q_ref, kn_ref, vn_ref, kc_ref, vc_ref, gs_ref, mn_ref, mc_ref, a_ref, c_ref,
                      qx_s, knp_s, vnp_s, kx_s, vx_s, out_s):
    n_rows = qx_s.shape[0]
    _expand_q(q_ref, qx_s)
    _pad_new(kn_ref, vn_ref, knp_s, vnp_s)
    qx = qx_s[...].astype(BF16)
    row = lax.broadcasted_iota(jnp.int32, (n_rows, LANES), 0)
    col = lax.broadcasted_iota(jnp.int32, (n_rows, LANES), 1)
    mask = col < row // N_HEADS
    o_new, carry = _sb_block(qx, knp_s[...], vnp_s[...], mn_ref[...], None, mask)
    _gather_heads(kc_ref, kx_s)
    _gather_heads(vc_ref, vx_s)
    o_win, carry = _sb_block(qx, kx_s[...], vx_s[...], mc_ref[...], carry, None)
    _collapse_heads(o_new + o_win, out_s)
    a_ref[...] = (out_s[...] * gs_ref[...].astype(F32)).astype(BF16)
    c_ref[...] = jnp.broadcast_to(carry, c_ref.shape)


def _sb_dec_rest_body(q_ref, kc_ref, vc_ref, gs_ref, cin_ref, afast_ref, mc_ref, a_ref,
                      qx_s, kx_s, vx_s, acc_s, c_s, out_s):
    s = pl.program_id(1)

    @pl.when(s == 0)
    def _():
        _expand_q(q_ref, qx_s)
        acc_s[...] = jnp.zeros_like(acc_s)
        c_s[...] = cin_ref[:, 0:1]

    @pl.when(jnp.max(c_s[...]) > SB_SKIP_LOG)
    def _():
        _gather_heads(kc_ref, kx_s)
        _gather_heads(vc_ref, vx_s)
        o, carry = _sb_block(qx_s[...].astype(BF16), kx_s[...], vx_s[...], mc_ref[...], c_s[...], None)
        acc_s[...] += o
        c_s[...] = carry

    @pl.when(s == pl.num_programs(1) - 1)
    def _():
        _collapse_heads(acc_s[...], out_s)
        a_ref[...] = (afast_ref[...].astype(F32) + out_s[...] * gs_ref[...].astype(F32)).astype(BF16)


def _sb_decode(q, k_new, v_new, k_cache, v_cache, gs, dec_seq):
    n_rows_all, w = q.shape
    n_b = n_rows_all // dec_seq
    p_len = k_cache.shape[1] // N_HEADS
    win = min(SB_DEC_WIN, p_len)
    n_win = p_len // win
    n_rows = dec_seq * N_HEADS
    m_new = _tri(LANES, "row_gt_col")
    m_win = _tri(win, "row_gt_col")
    tok = lambda b: (b, 0)
    a_fast, carry = pl.pallas_call(
        _sb_dec_fast_body,
        grid=(n_b,),
        in_specs=[pl.BlockSpec((dec_seq, w), tok), pl.BlockSpec((dec_seq, w), tok),
                  pl.BlockSpec((dec_seq, w), tok),
                  pl.BlockSpec((1, win * N_HEADS, HEAD_DIM), lambda b: (b, n_win - 1, 0)),
                  pl.BlockSpec((1, win * N_HEADS, HEAD_DIM), lambda b: (b, n_win - 1, 0)),
                  pl.BlockSpec((dec_seq, w), tok),
                  pl.BlockSpec((LANES, LANES), lambda b: (0, 0)),
                  pl.BlockSpec((win, win), lambda b: (0, 0))],
        out_specs=(pl.BlockSpec((dec_seq, w), tok), pl.BlockSpec((n_rows, LANES), tok)),
        out_shape=(jax.ShapeDtypeStruct((n_rows_all, w), BF16),
                   jax.ShapeDtypeStruct((n_b * n_rows, LANES), F32)),
        scratch_shapes=[pltpu.VMEM((n_rows, w), F32), pltpu.VMEM((LANES, w), BF16),
                        pltpu.VMEM((LANES, w), BF16), pltpu.VMEM((win, w), BF16),
                        pltpu.VMEM((win, w), BF16), pltpu.VMEM((dec_seq, w), F32)],
        compiler_params=_cparams(("parallel",), vmem_mib=48),
        name="sb_decode_fast",
    )(q, k_new, v_new, k_cache, v_cache, gs, m_new, m_win)
    if n_win <= 1:
        return a_fast

    def rest(a_fast, carry):
        tok2 = lambda b, s: (b, 0)
        older = lambda b, s: (b, n_win - 2 - s, 0)
        return pl.pallas_call(
            _sb_dec_rest_body,
            grid=(n_b, n_win - 1),
            in_specs=[pl.BlockSpec((dec_seq, w), tok2),
                      pl.BlockSpec((1, win * N_HEADS, HEAD_DIM), older),
                      pl.BlockSpec((1, win * N_HEADS, HEAD_DIM), older),
                      pl.BlockSpec((dec_seq, w), tok2),
                      pl.BlockSpec((n_rows, LANES), tok2),
                      pl.BlockSpec((dec_seq, w), tok2),
                      pl.BlockSpec((win, win), lambda b, s: (0, 0))],
            out_specs=pl.BlockSpec((dec_seq, w), tok2),
            out_shape=jax.ShapeDtypeStruct((n_rows_all, w), BF16),
            scratch_shapes=[pltpu.VMEM((n_rows, w), F32), pltpu.VMEM((win, w), BF16),
                            pltpu.VMEM((win, w), BF16), pltpu.VMEM((n_rows, w), F32),
                            pltpu.VMEM((n_rows, 1), F32), pltpu.VMEM((dec_seq, w), F32)],
            compiler_params=_cparams(("parallel", "arbitrary"), vmem_mib=48),
            name="sb_decode_rest",
        )(q, k_cache, v_cache, gs, carry, a_fast, m_win)

    need_rest = jnp.max(carry) > SB_SKIP_LOG
    return lax.cond(need_rest, rest, lambda a, c: a, a_fast, carry)


def _fox_dec_body(q_ref, kn_ref, vn_ref, kc_ref, vc_ref, gs_ref, cn_ref, r_ref, a_ref,
                  qx_s, knp_s, vnp_s, kx_s, vx_s, acc_s, m_s, l_s, fq_s, out_s):
    s = pl.program_id(1)
    n_rows = qx_s.shape[0]
    n_tok = n_rows // N_HEADS

    def update(sc, v):
        m = m_s[...]
        m_new = jnp.maximum(m, jnp.max(sc, axis=1, keepdims=True))
        alpha = jnp.exp2(m - m_new)
        p = jnp.exp2(sc - m_new)
        l_s[...] = alpha * l_s[...] + jnp.sum(p, axis=1, keepdims=True)
        acc_s[...] = alpha * acc_s[...] + _dot(p.astype(BF16), v)
        m_s[...] = m_new

    @pl.when(s == 0)
    def _():
        _expand_q(q_ref, qx_s)
        _pad_new(kn_ref, vn_ref, knp_s, vnp_s)
        m_s[...] = jnp.full_like(m_s, NEG_BIG)
        l_s[...] = jnp.zeros_like(l_s)
        acc_s[...] = jnp.zeros_like(acc_s)
        row = lax.broadcasted_iota(jnp.int32, (n_rows, LANES), 0)
        col = lax.broadcasted_iota(jnp.int32, (n_rows, LANES), 1)
        cn = jnp.concatenate([cn_ref[0]] * n_tok, axis=0) * LOG2E
        fq = jnp.sum(jnp.where(col == row // N_HEADS, cn, 0.0), axis=1, keepdims=True)
        fq_s[...] = fq
        sc = _dot_nt(qx_s[...].astype(BF16), knp_s[...]) + (fq - cn)
        sc = jnp.where(col <= row // N_HEADS, sc, NEG_BIG)
        update(sc, vnp_s[...])

    _gather_heads(kc_ref, kx_s)
    _gather_heads(vc_ref, vx_s)
    r = jnp.concatenate([r_ref[...]] * n_tok, axis=0) * LOG2E
    update(_dot_nt(qx_s[...].astype(BF16), kx_s[...]) + (fq_s[...] + r), vx_s[...])

    @pl.when(s == pl.num_programs(1) - 1)
    def _():
        _collapse_heads(acc_s[...] / l_s[...], out_s)
        a_ref[...] = (out_s[...] * gs_ref[...].astype(F32)).astype(BF16)


def _fox_decode(q, k_new, v_new, k_cache, v_cache, gs, c_new, r_past, dec_seq):
    n_rows_all, w = q.shape
    n_b = n_rows_all // dec_seq
    p_len = k_cache.shape[1] // N_HEADS
    kb = min(DEC_KB, p_len)
    nkb = p_len // kb
    n_rows = dec_seq * N_HEADS
    tok = lambda b, s: (b, 0)
    cache = lambda b, s: (b, s, 0)
    return pl.pallas_call(
        _fox_dec_body,
        grid=(n_b, nkb),
        in_specs=[pl.BlockSpec((dec_seq, w), tok), pl.BlockSpec((dec_seq, w), tok),
                  pl.BlockSpec((dec_seq, w), tok),
                  pl.BlockSpec((1, kb * N_HEADS, HEAD_DIM), cache),
                  pl.BlockSpec((1, kb * N_HEADS, HEAD_DIM), cache),
                  pl.BlockSpec((dec_seq, w), tok),
                  pl.BlockSpec((1, N_HEADS, LANES), lambda b, s: (b, 0, 0)),
                  pl.BlockSpec((N_HEADS, kb), lambda b, s: (b, s))],
        out_specs=pl.BlockSpec((dec_seq, w), tok),
        out_shape=jax.ShapeDtypeStruct((n_rows_all, w), BF16),
        scratch_shapes=[pltpu.VMEM((n_rows, w), F32), pltpu.VMEM((LANES, w), BF16),
                        pltpu.VMEM((LANES, w), BF16), pltpu.VMEM((kb, w), BF16),
                        pltpu.VMEM((kb, w), BF16), pltpu.VMEM((n_rows, w), F32),
                        pltpu.VMEM((n_rows, 1), F32), pltpu.VMEM((n_rows, 1), F32),
                        pltpu.VMEM((n_rows, 1), F32), pltpu.VMEM((dec_seq, w), F32)],
        compiler_params=_cparams(("parallel", "arbitrary"), vmem_mib=48),
        name="fox_decode",
    )(q, k_new, v_new, k_cache, v_cache, gs, c_new, r_past)


def _post_body(x_ref, asb_ref, afx_ref, msb_ref, mfx_ref, wsb_ref, wfx_ref, wo_ref, y_ref):
    u_sb = _dot(asb_ref[...], wsb_ref[...])
    u_fx = _dot(afx_ref[...], wfx_ref[...])
    merged = msb_ref[...].astype(F32) * u_sb + mfx_ref[...].astype(F32) * u_fx
    y_ref[...] = x_ref[...] + _dot(merged.astype(BF16), wo_ref[...])


def _post(x2d, a_sb, a_fx, m_sig, w_sb, w_fx, w_o):
    m, d = x2d.shape
    w = a_sb.shape[1]
    tm = min(m, 256)
    row = lambda i: (i, 0)
    const = lambda i: (0, 0)
    resident = functools.partial(pl.BlockSpec, index_map=const, pipeline_mode=pl.Buffered(1))
    return pl.pallas_call(
        _post_body,
        grid=(m // tm,),
        in_specs=[pl.BlockSpec((tm, d), row), pl.BlockSpec((tm, w), row), pl.BlockSpec((tm, w), row),
                  pl.BlockSpec((tm, d), lambda i: (i, 0)), pl.BlockSpec((tm, d), lambda i: (i, 1)),
                  resident((w, d)), resident((w, d)), resident((d, d))],
        out_specs=pl.BlockSpec((tm, d), row),
        out_shape=jax.ShapeDtypeStruct((m, d), F32),
        compiler_params=_cparams(("parallel",), vmem_mib=56),
        name="post",
    )(x2d, a_sb, a_fx, m_sig, m_sig, w_sb, w_fx, w_o)


def _project_all(h, w_main, w_m, qnw, knw):
    q_sb = _proj(h, w_main, 0, "scale")
    k_sb, k_sb_b = _proj(h, w_main, 1, "kv")
    v_sb, v_sb_b = _proj(h, w_main, 2, "kv")
    g_sb = _proj(h, w_main, 3, "silu")
    q_fx = _proj(h, w_main, 4, "qnorm", nw=qnw)
    k_fx, k_fx_b = _proj(h, w_main, 5, "knorm", nw=knw)
    v_fx, v_fx_b = _proj(h, w_main, 6, "kv")
    g_fx = _proj(h, w_main, 7, "silu")
    m_sig = _proj(h, w_m, 0, "sigmoid", n_tiles=w_m.shape[1] // W_ATT)
    return (q_sb, k_sb, k_sb_b, v_sb, v_sb_b, g_sb, q_fx, k_fx, k_fx_b, v_fx, v_fx_b, g_fx, m_sig)


def kernel(x_prompt, x_sample, cache_sb_k, cache_sb_v, cache_fox_k, cache_fox_v, cache_fox_logf,
           norm_w, w_in, b_forget, q_norm_w, k_norm_w, w_branch_sb, w_branch_fox, w_out):
    depth = norm_w.shape[0]
    assert depth == 1, "single-layer step"
    bsz, seq, d = x_prompt.shape
    assert bsz == 1
    n_dec, dec_seq, _ = x_sample.shape
    p_len = cache_sb_k.shape[2]
    n_main = 8 * W_ATT

    w_l = w_in[0]
    w_main = w_l
    w_f = w_l[:, n_main:n_main + N_HEADS].astype(BF16)
    w_m = w_l[:, n_main + N_HEADS:].astype(BF16)
    wf_pad = jnp.pad(w_f, ((0, 0), (0, LANES - N_HEADS)))
    wft_pad = jnp.pad(w_f.T, ((0, 2 * SUBLANES - N_HEADS), (0, 0)))
    b_row = jnp.pad(b_forget[0].astype(F32)[None, :], ((0, 0), (0, LANES - N_HEADS)))
    b_col = jnp.pad(b_forget[0].astype(F32)[:, None], ((0, 2 * SUBLANES - N_HEADS), (0, 0)))
    qnw = jnp.tile(q_norm_w[0].astype(F32), N_HEADS)[None, :]
    knw = jnp.tile(k_norm_w[0].astype(F32), N_HEADS)[None, :]
    nw_row = norm_w[0].astype(F32)[None, :]
    w_sb = w_branch_sb[0].astype(BF16)
    w_fx = w_branch_fox[0].astype(BF16)
    w_o = w_out[0].astype(BF16)

    xp = x_prompt.reshape(seq, d)
    hp = _rmsnorm(xp, nw_row)
    (q_sb, k_sb, k_sb_b, v_sb, v_sb_b, g_sb, q_fx, k_fx, k_fx_b, v_fx, v_fx_b, g_fx,
     m_sig) = _project_all(hp, w_main, w_m, qnw, knw)
    logf, f_col, f_row = _logf_prompt(hp, wf_pad, wft_pad, b_row, b_col)
    a_sb = _sb_prompt(q_sb, k_sb_b, v_sb_b, g_sb)
    zb = (1.02 * HEAD_DIM * QK_SCALE) * jnp.max(jnp.abs(q_norm_w[0])) * jnp.max(jnp.abs(k_norm_w[0]))
    a_fx = _fox_prompt(q_fx, k_fx_b, v_fx_b, f_col, f_row, g_fx, zb.astype(F32))
    y_prompt = _post(xp, a_sb, a_fx, m_sig, w_sb, w_fx, w_o).reshape(bsz, seq, d)

    xs = x_sample.reshape(n_dec * dec_seq, d)
    hs = _rmsnorm(xs, nw_row)
    (sq_sb, sk_sb, sk_sb_b, sv_sb, sv_sb_b, sg_sb, sq_fx, sk_fx, sk_fx_b, sv_fx, sv_fx_b, sg_fx,
     sm_sig) = _project_all(hs, w_main, w_m, qnw, knw)
    past_logf_t = jnp.transpose(cache_fox_logf[0].astype(F32), (0, 2, 1)).reshape(n_dec * N_HEADS, p_len)
    s_logf, c_all, r_past = _logf_sample(hs, wf_pad, wft_pad, b_row, b_col, past_logf_t, dec_seq)
    c_new = jnp.transpose(c_all.reshape(N_HEADS, n_dec, dec_seq), (1, 0, 2))
    c_new = jnp.pad(c_new, ((0, 0), (0, 0), (0, LANES - dec_seq)))
    kc_sb = cache_sb_k[0].reshape(n_dec, p_len * N_HEADS, HEAD_DIM)
    vc_sb = cache_sb_v[0].reshape(n_dec, p_len * N_HEADS, HEAD_DIM)
    kc_fx = cache_fox_k[0].reshape(n_dec, p_len * N_HEADS, HEAD_DIM)
    vc_fx = cache_fox_v[0].reshape(n_dec, p_len * N_HEADS, HEAD_DIM)
    sa_sb = _sb_decode(sq_sb, sk_sb_b, sv_sb_b, kc_sb, vc_sb, sg_sb, dec_seq)
    sa_fx = _fox_decode(sq_fx, sk_fx_b, sv_fx_b, kc_fx, vc_fx, sg_fx, c_new, r_past, dec_seq)
    y_sample = _post(xs, sa_sb, sa_fx, sm_sig, w_sb, w_fx, w_o).reshape(n_dec, dec_seq, d)

    hd = (N_HEADS, HEAD_DIM)
    return (y_prompt, y_sample,
            k_sb.reshape(1, bsz, seq, *hd), v_sb.reshape(1, bsz, seq, *hd),
            k_fx.reshape(1, bsz, seq, *hd), v_fx.reshape(1, bsz, seq, *hd),
            logf.reshape(1, bsz, seq, N_HEADS),
            sk_sb.reshape(1, n_dec, dec_seq, *hd), sv_sb.reshape(1, n_dec, dec_seq, *hd),
            sk_fx.reshape(1, n_dec, dec_seq, *hd), sv_fx.reshape(1, n_dec, dec_seq, *hd),
            s_logf.reshape(1, n_dec, dec_seq, N_HEADS))
```

```python
import functools

import jax
import jax.numpy as jnp
from jax import lax
from jax.experimental import pallas as pl
from jax.experimental.pallas import tpu as pltpu

F32 = jnp.float32
BF16 = jnp.bfloat16

D_MODEL = 2048
N_HEADS = 8
HEAD_DIM = 128
W_ATT = N_HEADS * HEAD_DIM
RMS_EPS = 1e-6
QK_SCALE = HEAD_DIM ** -0.5
LOG2E = 1.4426950408889634
LANES = 128
SUBLANES = 8
SB_SKIP_LOG = -104.0
NEG_BIG = -1e30
MIB = 1024 * 1024


def _cparams(sem, vmem_mib=None):
    kw = dict(dimension_semantics=sem)
    if vmem_mib is not None:
        kw["vmem_limit_bytes"] = vmem_mib * MIB
    return pltpu.CompilerParams(**kw)


def _softplus_neg_abs(z):
    return jnp.log1p(jnp.exp(-jnp.abs(z)))


def _log_sigmoid(z):
    return jnp.minimum(z, 0.0) - _softplus_neg_abs(z)


def _split_bf16(x, n):
    parts = []
    r = x
    for _ in range(n - 1):
        p = r.astype(BF16)
        parts.append(p)
        r = r - p.astype(F32)
    parts.append(r.astype(BF16))
    return parts


def _dot(a, b):
    return jnp.dot(a, b, preferred_element_type=F32)


def _dot_nt(a, b):
    return lax.dot_general(a, b, (((1,), (1,)), ((), ())), preferred_element_type=F32)


def _dot_split_lhs(x, m, n):
    acc = None
    for p in _split_bf16(x, n):
        t = _dot(p, m)
        acc = t if acc is None else acc + t
    return acc


def _dot_split_rhs(m, x, n):
    acc = None
    for p in _split_bf16(x, n):
        t = _dot(m, p)
        acc = t if acc is None else acc + t
    return acc


def _tri(n, kind):
    r = lax.broadcasted_iota(jnp.int32, (n, n), 0)
    c = lax.broadcasted_iota(jnp.int32, (n, n), 1)
    if kind == "row_gt_col":
        m = r > c
    elif kind == "row_le_col":
        m = r <= c
    elif kind == "row_ge_col":
        m = r >= c
    else:
        raise ValueError(kind)
    return m.astype(BF16)


def _rmsnorm_body(x_ref, w_ref, o_ref):
    x = x_ref[...]
    ms = jnp.mean(x * x, axis=-1, keepdims=True)
    o_ref[...] = (x * lax.rsqrt(ms + RMS_EPS) * w_ref[...]).astype(o_ref.dtype)


def _rmsnorm(x2d, w_row):
    m, d = x2d.shape
    tm = min(m, 512)
    return pl.pallas_call(
        _rmsnorm_body,
        grid=(m // tm,),
        in_specs=[pl.BlockSpec((tm, d), lambda i: (i, 0)),
                  pl.BlockSpec((1, d), lambda i: (0, 0))],
        out_specs=pl.BlockSpec((tm, d), lambda i: (i, 0)),
        out_shape=jax.ShapeDtypeStruct((m, d), BF16),
        compiler_params=_cparams(("parallel",)),
        name="rmsnorm",
    )(x2d, w_row)


def _head_rmsnorm(acc, nw):
    parts = []
    for hh in range(N_HEADS):
        a = acc[:, hh * HEAD_DIM:(hh + 1) * HEAD_DIM]
        ms = jnp.mean(a * a, axis=-1, keepdims=True)
        parts.append(a * lax.rsqrt(ms + RMS_EPS))
    return jnp.concatenate(parts, axis=1) * nw


def _proj_epilogue(acc, kind, nw_ref, outs):
    if kind == "scale":
        outs[0][...] = (acc * QK_SCALE).astype(BF16)
    elif kind == "kv":
        outs[0][...] = acc
        outs[1][...] = acc.astype(BF16)
    elif kind == "silu":
        outs[0][...] = (acc * jax.nn.sigmoid(acc)).astype(BF16)
    elif kind == "sigmoid":
        outs[0][...] = jax.nn.sigmoid(acc).astype(BF16)
    elif kind == "qnorm":
        outs[0][...] = (_head_rmsnorm(acc, nw_ref[...]) * (QK_SCALE * LOG2E)).astype(BF16)
    elif kind == "knorm":
        y = _head_rmsnorm(acc, nw_ref[...])
        outs[0][...] = y
        outs[1][...] = y.astype(BF16)
    else:
        raise ValueError(kind)


def _proj_body(*refs, kind, n_out, shift):
    hp_ref, hs_ref, w_ref = refs[:3]
    pos = 3
    wn_ref = nw_ref = None
    if shift:
        wn_ref = refs[pos]
        pos += 1
    if kind in ("qnorm", "knorm"):
        nw_ref = refs[pos]
        pos += 1
    outs_p = refs[pos:pos + n_out]
    outs_s = refs[pos + n_out:pos + 2 * n_out]
    wb_s = refs[-1]
    i = pl.program_id(1)
    n_prompt = pl.num_programs(1) - 1

    @pl.when(i == 0)
    def _():
        if shift:
            wt = jnp.concatenate([w_ref[shift:, :], wn_ref[...]], axis=0)
        else:
            wt = w_ref[...]
        wb_s[...] = wt.T.astype(BF16)

    @pl.when(i < n_prompt)
    def _():
        _proj_epilogue(_dot(hp_ref[...], wb_s[...]), kind, nw_ref, outs_p)

    @pl.when(i == n_prompt)
    def _():
        _proj_epilogue(_dot(hs_ref[...], wb_s[...]), kind, nw_ref, outs_s)


def _proj(hp, hs, wt, col0, kind, nw=None, n_tiles=1):
    mp, d = hp.shape
    ms = hs.shape[0]
    tn = W_ATT
    tm = min(mp, 1024)
    n_prompt = mp // tm
    blk0 = col0 // tn
    shift = col0 - blk0 * tn
    assert shift in (0, SUBLANES)
    grid = (n_tiles, n_prompt + 1)
    prow = lambda j, i: (jnp.minimum(i, n_prompt - 1), 0)
    w_mode = {} if n_tiles > 1 else dict(pipeline_mode=pl.Buffered(1))
    in_specs = [pl.BlockSpec((tm, d), prow),
                pl.BlockSpec((ms, d), lambda j, i: (0, 0)),
                pl.BlockSpec((None, tn, d), lambda j, i: (0, blk0 + j, 0), **w_mode)]
    args = [hp, hs, wt]
    if shift:
        per = tn // shift
        in_specs.append(pl.BlockSpec((None, shift, d), lambda j, i: (0, (blk0 + j + 1) * per, 0)))
        args.append(wt)
    if kind in ("qnorm", "knorm"):
        in_specs.append(pl.BlockSpec((1, tn), lambda j, i: (0, 0)))
        args.append(nw)
    n_cols = tn * n_tiles
    dtypes = (F32, BF16) if kind in ("kv", "knorm") else (BF16,)
    p_block = pl.BlockSpec((tm, tn), lambda j, i: (jnp.minimum(i, n_prompt - 1), j))
    s_block = pl.BlockSpec((ms, tn), lambda j, i: (0, j))
    out_shape = tuple(jax.ShapeDtypeStruct((mp, n_cols), t) for t in dtypes) + \
        tuple(jax.ShapeDtypeStruct((ms, n_cols), t) for t in dtypes)
    out_specs = (p_block,) * len(dtypes) + (s_block,) * len(dtypes)
    return pl.pallas_call(
        functools.partial(_proj_body, kind=kind, n_out=len(dtypes), shift=shift),
        grid=grid,
        in_specs=in_specs,
        out_specs=out_specs,
        out_shape=out_shape,
        scratch_shapes=[pltpu.VMEM((d, tn), BF16)],
        compiler_params=_cparams(("parallel", "arbitrary"), vmem_mib=56),
        name="proj_" + kind,
    )(*args)


def _forget_weight(wft_ref):
    wft = wft_ref[...]
    pad = jnp.zeros((LANES - wft.shape[0], wft.shape[1]), F32)
    return jnp.concatenate([wft, pad], axis=0).T.astype(BF16)


def _logf_body(h_ref, wft_ref, brow_ref, l_ref, lf_ref, fcol_ref, frow_ref, ccol_s, wf_s):
    i = pl.program_id(0)
    tm = h_ref.shape[0]

    @pl.when(i == 0)
    def _():
        ccol_s[...] = jnp.zeros_like(ccol_s)
        wf_s[...] = _forget_weight(wft_ref)

    lf = _log_sigmoid(_dot(h_ref[...], wf_s[...]) + brow_ref[...])
    lf_ref[...] = lf[:, :N_HEADS]
    f_col = _dot_split_rhs(l_ref[...], lf, 3) + ccol_s[...]
    fcol_ref[...] = f_col[:, :N_HEADS]
    ccol_s[...] = f_col[tm - 1:tm, :]
    frow_ref[...] = f_col.T[:N_HEADS, :]


def _logf_prompt(h, wt, f_row0, b_row):
    m, d = h.shape
    tm = min(m, 512)
    l_mat = _tri(tm, "row_ge_col")
    const = lambda i: (0, 0)
    return pl.pallas_call(
        _logf_body,
        grid=(m // tm,),
        in_specs=[pl.BlockSpec((tm, d), lambda i: (i, 0)),
                  pl.BlockSpec((None, N_HEADS, d), lambda i: (0, f_row0 // N_HEADS, 0)),
                  pl.BlockSpec((1, LANES), const),
                  pl.BlockSpec((tm, tm), const)],
        out_specs=(pl.BlockSpec((tm, N_HEADS), lambda i: (i, 0)),
                   pl.BlockSpec((tm, N_HEADS), lambda i: (i, 0)),
                   pl.BlockSpec((N_HEADS, tm), lambda i: (0, i))),
        out_shape=(jax.ShapeDtypeStruct((m, N_HEADS), F32),
                   jax.ShapeDtypeStruct((m, N_HEADS), F32),
                   jax.ShapeDtypeStruct((N_HEADS, m), F32)),
        scratch_shapes=[pltpu.VMEM((1, LANES), F32), pltpu.VMEM((d, LANES), BF16)],
        compiler_params=_cparams(("arbitrary",)),
        name="logf_prompt",
    )(h, wt, b_row, l_mat)


def _logf_sample_body(h_ref, wft_ref, brow_ref, bu_ref, x_ref, ms_ref, lf_ref, c_ref, r_ref, *, kb):
    lf = _log_sigmoid(_dot(h_ref[...], _forget_weight(wft_ref)) + brow_ref[...])
    lf_ref[...] = lf[:, :N_HEADS]
    lft = lf.T[:2 * SUBLANES, :]
    c_ref[...] = _dot_split_lhs(lft, bu_ref[...], 3)[:N_HEADS, :]
    n_blocks = x_ref.shape[1] // kb
    carry = jnp.zeros((x_ref.shape[0], 1), F32)
    for blk in range(n_blocks - 1, -1, -1):
        x = x_ref[:, blk * kb:(blk + 1) * kb]
        cum = _dot_split_lhs(x, ms_ref[...], 3)
        r_ref[:, blk * kb:(blk + 1) * kb] = cum + carry
        carry = carry + cum[:, 0:1] + x[:, 0:1]


def _logf_sample(h_s, wt, f_row0, b_row, past_logf_t, dec_seq):
    n_rows, d = h_s.shape
    n_bh, p_len = past_logf_t.shape
    kb = min(p_len, 512)
    r = lax.broadcasted_iota(jnp.int32, (n_rows, n_rows), 0)
    c = lax.broadcasted_iota(jnp.int32, (n_rows, n_rows), 1)
    bu = ((r // dec_seq == c // dec_seq) & (r <= c)).astype(BF16)
    ms = _tri(kb, "row_gt_col")
    whole = lambda shape: pl.BlockSpec(shape, lambda i: (0,) * len(shape))
    return pl.pallas_call(
        functools.partial(_logf_sample_body, kb=kb),
        grid=(1,),
        in_specs=[whole((n_rows, d)),
                  pl.BlockSpec((None, N_HEADS, d), lambda i: (0, f_row0 // N_HEADS, 0)),
                  whole((1, LANES)), whole((n_rows, n_rows)), whole((n_bh, p_len)), whole((kb, kb))],
        out_specs=(whole((n_rows, N_HEADS)), whole((N_HEADS, n_rows)), whole((n_bh, p_len))),
        out_shape=(jax.ShapeDtypeStruct((n_rows, N_HEADS), F32),
                   jax.ShapeDtypeStruct((N_HEADS, n_rows), F32),
                   jax.ShapeDtypeStruct((n_bh, p_len), F32)),
        compiler_params=_cparams(("arbitrary",), vmem_mib=32),
        name="logf_sample",
    )(h_s, wt, b_row, bu, past_logf_t, ms)


SB_TQ = 256


def _sb_block(q, k, v, m_mat, carry, mask):
    z = _dot_nt(q, k)
    sp = _softplus_neg_abs(z)
    lsp = jnp.minimum(z, 0.0) - sp
    lsn = -jnp.maximum(z, 0.0) - sp
    if mask is not None:
        lsn = jnp.where(mask, lsn, 0.0)
    cum = _dot_split_lhs(lsn, m_mat, 2)
    if carry is not None:
        cum = cum + carry
    w = jnp.exp(lsp + cum)
    if mask is not None:
        w = jnp.where(mask, w, 0.0)
    o = _dot(w.astype(BF16), v)
    new_carry = cum[:, 0:1] + lsn[:, 0:1]
    return o, new_carry


def _sb_fast_body(q_ref, kd_ref, vd_ref, kp_ref, vp_ref, gs_ref, m_ref, a_ref, c_ref):
    i = pl.program_id(0)
    tq = q_ref.shape[0]
    has_prev = i > 0
    row = lax.broadcasted_iota(jnp.int32, (tq, tq), 0)
    col = lax.broadcasted_iota(jnp.int32, (tq, tq), 1)
    dmask = col < row
    pmask = jnp.logical_and(has_prev, col >= 0)
    lane = lax.broadcasted_iota(jnp.int32, (tq, LANES), 1)
    m_mat = m_ref[...]
    cacc = jnp.full((tq, LANES), NEG_BIG, F32)
    for hh in range(N_HEADS):
        sl = slice(hh * HEAD_DIM, (hh + 1) * HEAD_DIM)
        q = q_ref[:, sl]
        od, cd = _sb_block(q, kd_ref[:, sl], vd_ref[:, sl], m_mat, None, dmask)
        op, cp = _sb_block(q, kp_ref[:, sl], vp_ref[:, sl], m_mat, cd, pmask)
        a_ref[:, sl] = ((od + op) * gs_ref[:, sl].astype(F32)).astype(BF16)
        cacc = jnp.where(lane == hh, cp, cacc)
    c_ref[...] = cacc


def _sb_rest_body(q_ref, k_ref, v_ref, gs_ref, cin_ref, afast_ref, m_ref, a_ref, o_s, c_s):
    i = pl.program_id(0)
    s = pl.program_id(1)
    j = i - 2 - s
    tq = q_ref.shape[0]

    @pl.when(s == 0)
    def _():
        o_s[...] = jnp.zeros_like(o_s)
        c_s[...] = cin_ref[...]

    active = jnp.logical_and(j >= 0, jnp.max(c_s[...]) > SB_SKIP_LOG)

    @pl.when(active)
    def _():
        lane = lax.broadcasted_iota(jnp.int32, (tq, LANES), 1)
        m_mat = m_ref[...]
        c_all = c_s[...]
        cacc = c_all
        for hh in range(N_HEADS):
            sl = slice(hh * HEAD_DIM, (hh + 1) * HEAD_DIM)
            carry = jnp.sum(jnp.where(lane == hh, c_all, 0.0), axis=1, keepdims=True)
            o, cn = _sb_block(q_ref[:, sl], k_ref[:, sl], v_ref[:, sl], m_mat, carry, None)
            o_s[:, sl] += o
            cacc = jnp.where(lane == hh, cn, cacc)
        c_s[...] = cacc

    @pl.when(s == pl.num_programs(1) - 1)
    def _():
        a_ref[...] = (afast_ref[...].astype(F32) + o_s[...] * gs_ref[...].astype(F32)).astype(BF16)


def _sb_prompt(q, k, v, gs):
    t, w = q.shape
    tq = min(SB_TQ, t)
    nq = t // tq
    m_mat = _tri(tq, "row_gt_col")
    blk = lambda f: pl.BlockSpec((tq, w), f)
    a_fast, carry = pl.pallas_call(
        _sb_fast_body,
        grid=(nq,),
        in_specs=[blk(lambda i: (i, 0)), blk(lambda i: (i, 0)), blk(lambda i: (i, 0)),
                  blk(lambda i: (jnp.maximum(i - 1, 0), 0)), blk(lambda i: (jnp.maximum(i - 1, 0), 0)),
                  blk(lambda i: (i, 0)),
                  pl.BlockSpec((tq, tq), lambda i: (0, 0))],
        out_specs=(blk(lambda i: (i, 0)), pl.BlockSpec((tq, LANES), lambda i: (i, 0))),
        out_shape=(jax.ShapeDtypeStruct((t, w), BF16), jax.ShapeDtypeStruct((t, LANES), F32)),
        compiler_params=_cparams(("parallel",), vmem_mib=48),
        name="sb_prompt_fast",
    )(q, k, v, k, v, gs, m_mat)
    if nq <= 2:
        return a_fast

    def rest(a_fast, carry):
        kidx = lambda i, s: (jnp.maximum(i - 2 - s, 0), 0)
        return pl.pallas_call(
            _sb_rest_body,
            grid=(nq, nq - 2),
            in_specs=[blk(lambda i, s: (i, 0)), blk(kidx), blk(kidx), blk(lambda i, s: (i, 0)),
                      pl.BlockSpec((tq, LANES), lambda i, s: (i, 0)),
                      blk(lambda i, s: (i, 0)),
                      pl.BlockSpec((tq, tq), lambda i, s: (0, 0))],
            out_specs=blk(lambda i, s: (i, 0)),
            out_shape=jax.ShapeDtypeStruct((t, w), BF16),
            scratch_shapes=[pltpu.VMEM((tq, w), F32), pltpu.VMEM((tq, LANES), F32)],
            compiler_params=_cparams(("parallel", "arbitrary"), vmem_mib=48),
            name="sb_prompt_rest",
        )(q, k, v, gs, carry, a_fast, m_mat)

    need_rest = jnp.max(carry[2 * tq:, :]) > SB_SKIP_LOG
    return lax.cond(need_rest, rest, lambda a, c: a, a_fast, carry)


FOX_T = 512
FOX_WIN_TILES = 6
FOX_ZB_MAX = 40.0
FOX_SKIP_LOG = -104.0


def _head_column(blk, hh):
    lane8 = lax.broadcasted_iota(jnp.int32, blk.shape, 1)
    return jnp.sum(jnp.where(lane8 == hh, blk, 0.0), axis=1, keepdims=True)


def _aug_tile(f, f_first, lane):
    hi = f.astype(BF16).astype(F32)
    r1 = f - hi
    mid = r1.astype(BF16).astype(F32)
    low = (r1 - mid).astype(BF16).astype(F32)
    fb = 0 if f_first else 3
    ob = 3 if f_first else 0
    out = jnp.where(lane == fb, hi, jnp.where(lane == fb + 1, mid, jnp.where(lane == fb + 2, low, 0.0)))
    out = jnp.where(jnp.logical_and(lane >= ob, lane < ob + 3), 1.0, out)
    return out.astype(BF16)


def _fox_fast_body(nwin_ref, q_ref, k_ref, v_ref, fcol_ref, c2_ref, gs_ref, a_ref, kaug_s):
    hh = pl.program_id(0)
    qb = pl.program_id(1)
    nq = pl.num_programs(1)
    tb = q_ref.shape[0]
    lane = lax.broadcasted_iota(jnp.int32, (tb, LANES), 1)

    def f2_col(start):
        return _head_column(fcol_ref[pl.ds(start, tb), :], hh) * LOG2E

    @pl.when(qb == 0)
    def _():
        def build(c, carry):
            start = pl.multiple_of(c * tb, tb)
            kaug_s[pl.ds(start, tb), :] = _aug_tile(-f2_col(start), False, lane)
            return carry
        lax.fori_loop(0, nq, build, 0)

    q_start = pl.multiple_of(qb * tb, tb)
    q2 = jnp.concatenate([q_ref[...], _aug_tile(f2_col(q_start) - c2_ref[...], True, lane)], axis=1)
    span = min(FOX_WIN_TILES * tb, k_ref.shape[0])
    lane_w = lax.broadcasted_iota(jnp.int32, (span, LANES), 1)
    ones_blk = jnp.where(lane_w == 0, 1.0, 0.0).astype(BF16)
    col = lax.broadcasted_iota(jnp.int32, (1, span), 1)
    row = lax.broadcasted_iota(jnp.int32, (tb, 1), 0)
    q_end = q_start + tb

    def window(j, acc):
        upper = q_end - j * span
        start = pl.multiple_of(jnp.maximum(upper - span, 0), tb)
        k2 = jnp.concatenate([k_ref[pl.ds(start, span), :], kaug_s[pl.ds(start, span), :]], axis=1)
        p = jnp.exp2(_dot_nt(q2, k2))
        valid = jnp.logical_and(col - (q_start - start) <= row, col < upper - start)
        p = jnp.where(valid, p, 0.0).astype(BF16)
        v2 = jnp.concatenate([v_ref[pl.ds(start, span), :], ones_blk], axis=1)
        return acc + _dot(p, v2)

    acc = lax.fori_loop(0, nwin_ref[hh * nq + qb], window, jnp.zeros((tb, 2 * HEAD_DIM), F32))
    o = acc[:, :HEAD_DIM] / acc[:, HEAD_DIM:HEAD_DIM + 1]
    a_ref[...] = (o * gs_ref[...].astype(F32)).astype(BF16)


def _fox_slow_body(q_ref, k_ref, v_ref, fcol_ref, frow_ref, gs_ref, a_ref):
    hh = pl.program_id(0)
    qb = pl.program_id(1)
    tb = q_ref.shape[0]
    q = q_ref[...]
    fq = _head_column(fcol_ref[...], hh) * LOG2E

    def scores(kb):
        start = pl.multiple_of(kb * tb, tb)
        k = k_ref[pl.ds(start, tb), :]
        fk = frow_ref[0, :, pl.ds(start, tb)] * LOG2E
        return _dot_nt(q, k) + (fq - fk), v_ref[pl.ds(start, tb), :]

    def update(s, v, carry):
        m, l, acc = carry
        m_new = jnp.maximum(m, jnp.max(s, axis=1, keepdims=True))
        alpha = jnp.exp2(m - m_new)
        p = jnp.exp2(s - m_new)
        l = alpha * l + jnp.sum(p, axis=1, keepdims=True)
        acc = alpha * acc + _dot(p.astype(BF16), v)
        return m_new, l, acc

    def body(kb, carry):
        s, v = scores(kb)
        return update(s, v, carry)

    init = (jnp.full((tb, 1), NEG_BIG, F32), jnp.zeros((tb, 1), F32), jnp.zeros((tb, HEAD_DIM), F32))
    carry = lax.fori_loop(0, qb, body, init)
    s, v = scores(qb)
    row = lax.broadcasted_iota(jnp.int32, (tb, tb), 0)
    col = lax.broadcasted_iota(jnp.int32, (tb, tb), 1)
    s = jnp.where(col <= row, s, NEG_BIG)
    _, l, acc = update(s, v, carry)
    a_ref[...] = ((acc / l) * gs_ref[...].astype(F32)).astype(BF16)


def _fox_prompt(q, k, v, f_col, f_row, gs, zb):
    t, w = q.shape
    tb = min(FOX_T, t)
    nq = t // tb
    out_shape = jax.ShapeDtypeStruct((t, w), BF16)

    def fast(q, k, v, f_col, f_row, gs, zb):
        f_start = f_row[:, ::tb]
        f_end = f_row[:, tb - 1::tb]
        dead = (f_start[:, :, None] - f_end[:, None, :]) < FOX_SKIP_LOG
        lo = jnp.sum(dead, axis=2).astype(jnp.int32)
        live_blocks = jnp.arange(1, nq + 1, dtype=jnp.int32)[None, :] - lo
        win_tiles = min(FOX_WIN_TILES, nq)
        nwin = ((live_blocks + win_tiles - 1) // win_tiles).reshape(-1)
        c2 = (zb * LOG2E).reshape(1, 1)
        head_blk = lambda h, i, lo: (i, h)
        head_all = lambda h, i, lo: (0, h)
        whole = lambda h, i, lo: (0, 0)
        grid_spec = pltpu.PrefetchScalarGridSpec(
            num_scalar_prefetch=1,
            grid=(N_HEADS, nq),
            in_specs=[pl.BlockSpec((tb, HEAD_DIM), head_blk),
                      pl.BlockSpec((t, HEAD_DIM), head_all),
                      pl.BlockSpec((t, HEAD_DIM), head_all),
                      pl.BlockSpec((t, N_HEADS), whole, pipeline_mode=pl.Buffered(1)),
                      pl.BlockSpec((1, 1), whole),
                      pl.BlockSpec((tb, HEAD_DIM), head_blk)],
            out_specs=pl.BlockSpec((tb, HEAD_DIM), head_blk),
            scratch_shapes=[pltpu.VMEM((t, LANES), BF16)])
        return pl.pallas_call(
            _fox_fast_body, grid_spec=grid_spec, out_shape=out_shape,
            compiler_params=_cparams(("parallel", "arbitrary"), vmem_mib=48),
            name="fox_prompt_fast",
        )(nwin, q, k, v, f_col, c2, gs)

    def slow(q, k, v, f_col, f_row, gs, zb):
        f_row3 = f_row.reshape(N_HEADS, 1, t)
        return pl.pallas_call(
            _fox_slow_body,
            grid=(N_HEADS, nq),
            in_specs=[pl.BlockSpec((tb, HEAD_DIM), lambda h, i: (i, h)),
                      pl.BlockSpec((t, HEAD_DIM), lambda h, i: (0, h)),
                      pl.BlockSpec((t, HEAD_DIM), lambda h, i: (0, h)),
                      pl.BlockSpec((tb, N_HEADS), lambda h, i: (i, 0)),
                      pl.BlockSpec((1, 1, t), lambda h, i: (h, 0, 0)),
                      pl.BlockSpec((tb, HEAD_DIM), lambda h, i: (i, h))],
            out_specs=pl.BlockSpec((tb, HEAD_DIM), lambda h, i: (i, h)),
            out_shape=out_shape,
            compiler_params=_cparams(("parallel", "parallel"), vmem_mib=48),
            name="fox_prompt_slow",
        )(q, k, v, f_col, f_row3, gs)

    return lax.cond(zb <= FOX_ZB_MAX, fast, slow, q, k, v, f_col, f_row, gs, zb)


DEC_KB = 512
SB_DEC_WIN = 256


def _own_head_mask():
    row = lax.broadcasted_iota(jnp.int32, (SUBLANES, W_ATT), 0)
    col = lax.broadcasted_iota(jnp.int32, (SUBLANES, W_ATT), 1)
    return (col // HEAD_DIM) == row


def _expand_q(q_ref, qx_s):
    n_tok = q_ref.shape[0]
    qf = q_ref[...].astype(F32)
    own = _own_head_mask()
    for tkn in range(n_tok):
        rep = jnp.broadcast_to(qf[tkn:tkn + 1, :], (SUBLANES, W_ATT))
        qx_s[tkn * N_HEADS:(tkn + 1) * N_HEADS, :] = jnp.where(own, rep, 0.0)


def _pad_new(kn_ref, vn_ref, knp_s, vnp_s):
    n_tok = kn_ref.shape[0]
    knp_s[...] = jnp.zeros_like(knp_s)
    vnp_s[...] = jnp.zeros_like(vnp_s)
    knp_s[0:n_tok, :] = kn_ref[...]
    vnp_s[0:n_tok, :] = vn_ref[...]


def _gather_heads(c_ref, dst_s):
    n_keys = dst_s.shape[0]
    for hh in range(N_HEADS):
        dst_s[:, hh * HEAD_DIM:(hh + 1) * HEAD_DIM] = (
            c_ref[0, pl.ds(hh, n_keys, stride=N_HEADS), :].astype(BF16))


def _collapse_heads(o_full, out_s):
    n_tok = o_full.shape[0] // N_HEADS
    own = _own_head_mask()
    for tkn in range(n_tok):
        blk = jnp.where(own, o_full[tkn * N_HEADS:(tkn + 1) * N_HEADS, :], 0.0)
        out_s[tkn:tkn + 1, :] = jnp.sum(blk, axis=0, keepdims=True)


def _sb_dec_fast_body(q_ref, kn_ref, vn_ref, kc_ref, vc_ref, gs_ref, mn_ref, mc_ref, a_ref, c_ref,
                      qx_s, knp_s, vnp_s, kx_s, vx_s, out_s):
    n_rows = qx_s.shape[0]
    _expand_q(q_ref, qx_s)
    _pad_new(kn_ref, vn_ref, knp_s, vnp_s)
    qx = qx_s[...].astype(BF16)
    row = lax.broadcasted_iota(jnp.int32, (n_rows, LANES), 0)
    col = lax.broadcasted_iota(jnp.int32, (n_rows, LANES), 1)
    mask = col < row // N_HEADS
    o_new, carry = _sb_block(qx, knp_s[...], vnp_s[...], mn_ref[...], None, mask)
    _gather_heads(kc_ref, kx_s)
    _gather_heads(vc_ref, vx_s)
    o_win, carry = _sb_block(qx, kx_s[...], vx_s[...], mc_ref[...], carry, None)
    _collapse_heads(o_new + o_win, out_s)
    a_ref[...] = (out_s[...] * gs_ref[...].astype(F32)).astype(BF16)
    c_ref[...] = jnp.broadcast_to(carry, c_ref.shape)


def _sb_dec_rest_body(q_ref, kc_ref, vc_ref, gs_ref, cin_ref, afast_ref, mc_ref, a_ref,
                      qx_s, kx_s, vx_s, acc_s, c_s, out_s):
    s = pl.program_id(1)

    @pl.when(s == 0)
    def _():
        _expand_q(q_ref, qx_s)
        acc_s[...] = jnp.zeros_like(acc_s)
        c_s[...] = cin_ref[:, 0:1]

    @pl.when(jnp.max(c_s[...]) > SB_SKIP_LOG)
    def _():
        _gather_heads(kc_ref, kx_s)
        _gather_heads(vc_ref, vx_s)
        o, carry = _sb_block(qx_s[...].astype(BF16), kx_s[...], vx_s[...], mc_ref[...], c_s[...], None)
        acc_s[...] += o
        c_s[...] = carry

    @pl.when(s == pl.num_programs(1) - 1)
    def _():
        _collapse_heads(acc_s[...], out_s)
        a_ref[...] = (afast_ref[...].astype(F32) + out_s[...] * gs_ref[...].astype(F32)).astype(BF16)


def _sb_decode(q, k_new, v_new, k_cache, v_cache, gs, dec_seq):
    n_rows_all, w = q.shape
    n_b = n_rows_all // dec_seq
    p_len = k_cache.shape[1] // N_HEADS
    win = min(SB_DEC_WIN, p_len)
    n_win = p_len // win
    n_rows = dec_seq * N_HEADS
    m_new = _tri(LANES, "row_gt_col")
    m_win = _tri(win, "row_gt_col")
    tok = lambda b: (b, 0)
    a_fast, carry = pl.pallas_call(
        _sb_dec_fast_body,
        grid=(n_b,),
        in_specs=[pl.BlockSpec((dec_seq, w), tok), pl.BlockSpec((dec_seq, w), tok),
                  pl.BlockSpec((dec_seq, w), tok),
                  pl.BlockSpec((1, win * N_HEADS, HEAD_DIM), lambda b: (b, n_win - 1, 0)),
                  pl.BlockSpec((1, win * N_HEADS, HEAD_DIM), lambda b: (b, n_win - 1, 0)),
                  pl.BlockSpec((dec_seq, w), tok),
                  pl.BlockSpec((LANES, LANES), lambda b: (0, 0)),
                  pl.BlockSpec((win, win), lambda b: (0, 0))],
        out_specs=(pl.BlockSpec((dec_seq, w), tok), pl.BlockSpec((n_rows, LANES), tok)),
        out_shape=(jax.ShapeDtypeStruct((n_rows_all, w), BF16),
                   jax.ShapeDtypeStruct((n_b * n_rows, LANES), F32)),
        scratch_shapes=[pltpu.VMEM((n_rows, w), F32), pltpu.VMEM((LANES, w), BF16),
                        pltpu.VMEM((LANES, w), BF16), pltpu.VMEM((win, w), BF16),
                        pltpu.VMEM((win, w), BF16), pltpu.VMEM((dec_seq, w), F32)],
        compiler_params=_cparams(("parallel",), vmem_mib=48),
        name="sb_decode_fast",
    )(q, k_new, v_new, k_cache, v_cache, gs, m_new, m_win)
    if n_win <= 1:
        return a_fast

    def rest(a_fast, carry):
        tok2 = lambda b, s: (b, 0)
        older = lambda b, s: (b, n_win - 2 - s, 0)
        return pl.pallas_call(
            _sb_dec_rest_body,
            grid=(n_b, n_win - 1),
            in_specs=[pl.BlockSpec((dec_seq, w), tok2),
                      pl.BlockSpec((1, win * N_HEADS, HEAD_DIM), older),
                      pl.BlockSpec((1, win * N_HEADS, HEAD_DIM), older),
                      pl.BlockSpec((dec_seq, w), tok2),
                      pl.BlockSpec((n_rows, LANES), tok2),
                      pl.BlockSpec((dec_seq, w), tok2),
                      pl.BlockSpec((win, win), lambda b, s: (0, 0))],
            out_specs=pl.BlockSpec((dec_seq, w), tok2),
            out_shape=jax.ShapeDtypeStruct((n_rows_all, w), BF16),
            scratch_shapes=[pltpu.VMEM((n_rows, w), F32), pltpu.VMEM((win, w), BF16),
                            pltpu.VMEM((win, w), BF16), pltpu.VMEM((n_rows, w), F32),
                            pltpu.VMEM((n_rows, 1), F32), pltpu.VMEM((dec_seq, w), F32)],
            compiler_params=_cparams(("parallel", "arbitrary"), vmem_mib=48),
            name="sb_decode_rest",
        )(q, k_cache, v_cache, gs, carry, a_fast, m_win)

    need_rest = jnp.max(carry) > SB_SKIP_LOG
    return lax.cond(need_rest, rest, lambda a, c: a, a_fast, carry)


def _fox_dec_body(q_ref, kn_ref, vn_ref, kc_ref, vc_ref, gs_ref, cn_ref, r_ref, a_ref,
                  qx_s, knp_s, vnp_s, kx_s, vx_s, acc_s, m_s, l_s, fq_s, out_s):
    s = pl.program_id(1)
    n_rows = qx_s.shape[0]
    n_tok = n_rows // N_HEADS

    def update(sc, v):
        m = m_s[...]
        m_new = jnp.maximum(m, jnp.max(sc, axis=1, keepdims=True))
        alpha = jnp.exp2(m - m_new)
        p = jnp.exp2(sc - m_new)
        l_s[...] = alpha * l_s[...] + jnp.sum(p, axis=1, keepdims=True)
        acc_s[...] = alpha * acc_s[...] + _dot(p.astype(BF16), v)
        m_s[...] = m_new

    @pl.when(s == 0)
    def _():
        _expand_q(q_ref, qx_s)
        _pad_new(kn_ref, vn_ref, knp_s, vnp_s)
        m_s[...] = jnp.full_like(m_s, NEG_BIG)
        l_s[...] = jnp.zeros_like(l_s)
        acc_s[...] = jnp.zeros_like(acc_s)
        row = lax.broadcasted_iota(jnp.int32, (n_rows, LANES), 0)
        col = lax.broadcasted_iota(jnp.int32, (n_rows, LANES), 1)
        cn = jnp.concatenate([cn_ref[0]] * n_tok, axis=0) * LOG2E
        fq = jnp.sum(jnp.where(col == row // N_HEADS, cn, 0.0), axis=1, keepdims=True)
        fq_s[...] = fq
        sc = _dot_nt(qx_s[...].astype(BF16), knp_s[...]) + (fq - cn)
        sc = jnp.where(col <= row // N_HEADS, sc, NEG_BIG)
        update(sc, vnp_s[...])

    _gather_heads(kc_ref, kx_s)
    _gather_heads(vc_ref, vx_s)
    r = jnp.concatenate([r_ref[...]] * n_tok, axis=0) * LOG2E
    update(_dot_nt(qx_s[...].astype(BF16), kx_s[...]) + (fq_s[...] + r), vx_s[...])

    @pl.when(s == pl.num_programs(1) - 1)
    def _():
        _collapse_heads(acc_s[...] / l_s[...], out_s)
        a_ref[...] = (out_s[...] * gs_ref[...].astype(F32)).astype(BF16)


def _fox_decode(q, k_new, v_new, k_cache, v_cache, gs, c_new, r_past, dec_seq):
    n_rows_all, w = q.shape
    n_b = n_rows_all // dec_seq
    p_len = k_cache.shape[1] // N_HEADS
    kb = min(DEC_KB, p_len)
    nkb = p_len // kb
    n_rows = dec_seq * N_HEADS
    tok = lambda b, s: (b, 0)
    cache = lambda b, s: (b, s, 0)
    return pl.pallas_call(
        _fox_dec_body,
        grid=(n_b, nkb),
        in_specs=[pl.BlockSpec((dec_seq, w), tok), pl.BlockSpec((dec_seq, w), tok),
                  pl.BlockSpec((dec_seq, w), tok),
                  pl.BlockSpec((1, kb * N_HEADS, HEAD_DIM), cache),
                  pl.BlockSpec((1, kb * N_HEADS, HEAD_DIM), cache),
                  pl.BlockSpec((dec_seq, w), tok),
                  pl.BlockSpec((1, N_HEADS, LANES), lambda b, s: (b, 0, 0)),
                  pl.BlockSpec((N_HEADS, kb), lambda b, s: (b, s))],
        out_specs=pl.BlockSpec((dec_seq, w), tok),
        out_shape=jax.ShapeDtypeStruct((n_rows_all, w), BF16),
        scratch_shapes=[pltpu.VMEM((n_rows, w), F32), pltpu.VMEM((LANES, w), BF16),
                        pltpu.VMEM((LANES, w), BF16), pltpu.VMEM((kb, w), BF16),
                        pltpu.VMEM((kb, w), BF16), pltpu.VMEM((n_rows, w), F32),
                        pltpu.VMEM((n_rows, 1), F32), pltpu.VMEM((n_rows, 1), F32),
                        pltpu.VMEM((n_rows, 1), F32), pltpu.VMEM((dec_seq, w), F32)],
        compiler_params=_cparams(("parallel", "arbitrary"), vmem_mib=48),
        name="fox_decode",
    )(q, k_new, v_new, k_cache, v_cache, gs, c_new, r_past)


def _post_body(x_ref, asb_ref, afx_ref, msb_ref, mfx_ref, wsb_ref, wfx_ref, wo_ref, y_ref):
    u_sb = _dot(asb_ref[...], wsb_ref[...])
    u_fx = _dot(afx_ref[...], wfx_ref[...])
    merged = msb_ref[...].astype(F32) * u_sb + mfx_ref[...].astype(F32) * u_fx
    y_ref[...] = x_ref[...] + _dot(merged.astype(BF16), wo_ref[...])


def _post(x2d, a_sb, a_fx, m_sig, w_sb, w_fx, w_o):
    m, d = x2d.shape
    w = a_sb.shape[1]
    tm = min(m, 256)
    row = lambda i: (i, 0)
    const = lambda i: (0, 0)
    resident = functools.partial(pl.BlockSpec, index_map=const, pipeline_mode=pl.Buffered(1))
    return pl.pallas_call(
        _post_body,
        grid=(m // tm,),
        in_specs=[pl.BlockSpec((tm, d), row), pl.BlockSpec((tm, w), row), pl.BlockSpec((tm, w), row),
                  pl.BlockSpec((tm, d), lambda i: (i, 0)), pl.BlockSpec((tm, d), lambda i: (i, 1)),
                  resident((w, d)), resident((w, d)), resident((d, d))],
        out_specs=pl.BlockSpec((tm, d), row),
        out_shape=jax.ShapeDtypeStruct((m, d), F32),
        compiler_params=_cparams(("parallel",), vmem_mib=56),
        name="post",
    )(x2d, a_sb, a_fx, m_sig, m_sig, w_sb, w_fx, w_o)


def _project_all(hp, hs, w, qnw, knw):
    seg = lambda c: c * W_ATT
    q_sb = _proj(hp, hs, w, seg(0), "scale")
    k_sb = _proj(hp, hs, w, seg(1), "kv")
    v_sb = _proj(hp, hs, w, seg(2), "kv")
    g_sb = _proj(hp, hs, w, seg(3), "silu")
    q_fx = _proj(hp, hs, w, seg(4), "qnorm", nw=qnw)
    k_fx = _proj(hp, hs, w, seg(5), "knorm", nw=knw)
    v_fx = _proj(hp, hs, w, seg(6), "kv")
    g_fx = _proj(hp, hs, w, seg(7), "silu")
    m_sig = _proj(hp, hs, w, seg(8) + N_HEADS, "sigmoid", n_tiles=2 * D_MODEL // W_ATT)
    calls = (q_sb, k_sb, v_sb, g_sb, q_fx, k_fx, v_fx, g_fx, m_sig)
    prompt = tuple(o for r in calls for o in r[:len(r) // 2])
    decode = tuple(o for r in calls for o in r[len(r) // 2:])
    return prompt, decode


def kernel(x_prompt, x_sample, cache_sb_k, cache_sb_v, cache_fox_k, cache_fox_v, cache_fox_logf,
           norm_w, w_in, b_forget, q_norm_w, k_norm_w, w_branch_sb, w_branch_fox, w_out):
    depth = norm_w.shape[0]
    assert depth == 1, "single-layer step"
    bsz, seq, d = x_prompt.shape
    assert bsz == 1
    n_dec, dec_seq, _ = x_sample.shape
    p_len = cache_sb_k.shape[2]
    n_main = 8 * W_ATT

    w_t = jnp.swapaxes(w_in, 1, 2)
    b_row = jnp.pad(b_forget[0].astype(F32)[None, :], ((0, 0), (0, LANES - N_HEADS)))
    qnw = jnp.tile(q_norm_w[0].astype(F32), N_HEADS)[None, :]
    knw = jnp.tile(k_norm_w[0].astype(F32), N_HEADS)[None, :]
    nw_row = norm_w[0].astype(F32)[None, :]
    w_sb = w_branch_sb[0].astype(BF16)
    w_fx = w_branch_fox[0].astype(BF16)
    w_o = w_out[0].astype(BF16)

    xp = x_prompt.reshape(seq, d)
    xs = x_sample.reshape(n_dec * dec_seq, d)
    hp = _rmsnorm(xp, nw_row)
    hs = _rmsnorm(xs, nw_row)
    prompt_proj, decode_proj = _project_all(hp, hs, w_t, qnw, knw)

    (q_sb, k_sb, k_sb_b, v_sb, v_sb_b, g_sb, q_fx, k_fx, k_fx_b, v_fx, v_fx_b, g_fx,
     m_sig) = prompt_proj
    logf, f_col, f_row = _logf_prompt(hp, w_t, n_main, b_row)
    a_sb = _sb_prompt(q_sb, k_sb_b, v_sb_b, g_sb)
    zb = (1.02 * HEAD_DIM * QK_SCALE) * jnp.max(jnp.abs(q_norm_w[0])) * jnp.max(jnp.abs(k_norm_w[0]))
    a_fx = _fox_prompt(q_fx, k_fx_b, v_fx_b, f_col, f_row, g_fx, zb.astype(F32))
    y_prompt = _post(xp, a_sb, a_fx, m_sig, w_sb, w_fx, w_o).reshape(bsz, seq, d)

    (sq_sb, sk_sb, sk_sb_b, sv_sb, sv_sb_b, sg_sb, sq_fx, sk_fx, sk_fx_b, sv_fx, sv_fx_b, sg_fx,
     sm_sig) = decode_proj
    past_logf_t = jnp.transpose(cache_fox_logf[0].astype(F32), (0, 2, 1)).reshape(n_dec * N_HEADS, p_len)
    s_logf, c_all, r_past = _logf_sample(hs, w_t, n_main, b_row, past_logf_t, dec_seq)
    c_new = jnp.transpose(c_all.reshape(N_HEADS, n_dec, dec_seq), (1, 0, 2))
    c_new = jnp.pad(c_new, ((0, 0), (0, 0), (0, LANES - dec_seq)))
    kc_sb = cache_sb_k[0].reshape(n_dec, p_len * N_HEADS, HEAD_DIM)
    vc_sb = cache_sb_v[0].reshape(n_dec, p_len * N_HEADS, HEAD_DIM)
    kc_fx = cache_fox_k[0].reshape(n_dec, p_len * N_HEADS, HEAD_DIM)
    vc_fx = cache_fox_v[0].reshape(n_dec, p_len * N_HEADS, HEAD_DIM)
    sa_sb = _sb_decode(sq_sb, sk_sb_b, sv_sb_b, kc_sb, vc_sb, sg_sb, dec_seq)
    sa_fx = _fox_decode(sq_fx, sk_fx_b, sv_fx_b, kc_fx, vc_fx, sg_fx, c_new, r_past, dec_seq)
    y_sample = _post(xs, sa_sb, sa_fx, sm_sig, w_sb, w_fx, w_o).reshape(n_dec, dec_seq, d)

    hd = (N_HEADS, HEAD_DIM)
    return (y_prompt, y_sample,
            k_sb.reshape(1, bsz, seq, *hd), v_sb.reshape(1, bsz, seq, *hd),
            k_fx.reshape(1, bsz, seq, *hd), v_fx.reshape(1, bsz, seq, *hd),
            logf.reshape(1, bsz, seq, N_HEADS),
            sk_sb.reshape(1, n_dec, dec_seq, *hd), sv_sb.reshape(1, n_dec, dec_seq, *hd),
            sk_fx.reshape(1, n_dec, dec_seq, *hd), sv_fx.reshape(1, n_dec, dec_seq, *hd),
            s_logf.reshape(1, n_dec, dec_seq, N_HEADS))
```

```python
import functools

import jax
import jax.numpy as jnp
from jax import lax
from jax.experimental import pallas as pl
from jax.experimental.pallas import tpu as pltpu

F32 = jnp.float32
BF16 = jnp.bfloat16

D_MODEL = 2048
N_HEADS = 8
HEAD_DIM = 128
W_ATT = N_HEADS * HEAD_DIM
RMS_EPS = 1e-6
QK_SCALE = HEAD_DIM ** -0.5
LOG2E = 1.4426950408889634
LANES = 128
SUBLANES = 8
SB_SKIP_LOG2 = -150.0
NEG_BIG = -1e30
MIB = 1024 * 1024


def _cparams(sem, vmem_mib=None):
    kw = dict(dimension_semantics=sem)
    if vmem_mib is not None:
        kw["vmem_limit_bytes"] = vmem_mib * MIB
    return pltpu.CompilerParams(**kw)


def _softplus_neg_abs(z):
    return jnp.log1p(jnp.exp(-jnp.abs(z)))


def _log_sigmoid(z):
    return jnp.minimum(z, 0.0) - _softplus_neg_abs(z)


def _split_bf16(x, n):
    parts = []
    r = x
    for _ in range(n - 1):
        p = r.astype(BF16)
        parts.append(p)
        r = r - p.astype(F32)
    parts.append(r.astype(BF16))
    return parts


def _dot(a, b):
    return jnp.dot(a, b, preferred_element_type=F32)


def _dot_nt(a, b):
    return lax.dot_general(a, b, (((1,), (1,)), ((), ())), preferred_element_type=F32)


def _dot_split_lhs(x, m, n):
    acc = None
    for p in _split_bf16(x, n):
        t = _dot(p, m)
        acc = t if acc is None else acc + t
    return acc


def _dot_split_rhs(m, x, n):
    acc = None
    for p in _split_bf16(x, n):
        t = _dot(m, p)
        acc = t if acc is None else acc + t
    return acc


def _tri(n, kind):
    r = lax.broadcasted_iota(jnp.int32, (n, n), 0)
    c = lax.broadcasted_iota(jnp.int32, (n, n), 1)
    if kind == "row_gt_col":
        m = r > c
    elif kind == "row_le_col":
        m = r <= c
    elif kind == "row_ge_col":
        m = r >= c
    else:
        raise ValueError(kind)
    return m.astype(BF16)


def _rmsnorm_body(x_ref, w_ref, o_ref):
    x = x_ref[...]
    ms = jnp.mean(x * x, axis=-1, keepdims=True)
    o_ref[...] = (x * lax.rsqrt(ms + RMS_EPS) * w_ref[...]).astype(o_ref.dtype)


def _rmsnorm(x2d, w_row):
    m, d = x2d.shape
    tm = min(m, 512)
    return pl.pallas_call(
        _rmsnorm_body,
        grid=(m // tm,),
        in_specs=[pl.BlockSpec((tm, d), lambda i: (i, 0)),
                  pl.BlockSpec((1, d), lambda i: (0, 0))],
        out_specs=pl.BlockSpec((tm, d), lambda i: (i, 0)),
        out_shape=jax.ShapeDtypeStruct((m, d), BF16),
        compiler_params=_cparams(("parallel",)),
        name="rmsnorm",
    )(x2d, w_row)


def _head_rmsnorm(acc, nw):
    parts = []
    for hh in range(N_HEADS):
        a = acc[:, hh * HEAD_DIM:(hh + 1) * HEAD_DIM]
        ms = jnp.mean(a * a, axis=-1, keepdims=True)
        parts.append(a * lax.rsqrt(ms + RMS_EPS))
    return jnp.concatenate(parts, axis=1) * nw


def _proj_epilogue(acc, kind, nw_ref, outs):
    if kind == "scale":
        outs[0][...] = (acc * (QK_SCALE * LOG2E)).astype(BF16)
    elif kind == "kv":
        outs[0][...] = acc
        outs[1][...] = acc.astype(BF16)
    elif kind == "silu":
        outs[0][...] = (acc * jax.nn.sigmoid(acc)).astype(BF16)
    elif kind == "sigmoid":
        outs[0][...] = jax.nn.sigmoid(acc).astype(BF16)
    elif kind == "qnorm":
        outs[0][...] = (_head_rmsnorm(acc, nw_ref[...]) * (QK_SCALE * LOG2E)).astype(BF16)
    elif kind == "knorm":
        y = _head_rmsnorm(acc, nw_ref[...])
        outs[0][...] = y
        outs[1][...] = y.astype(BF16)
    else:
        raise ValueError(kind)


def _proj_body(*refs, kind, n_out, shift):
    hp_ref, hs_ref, w_ref = refs[:3]
    pos = 3
    wn_ref = nw_ref = None
    if shift:
        wn_ref = refs[pos]
        pos += 1
    if kind in ("qnorm", "knorm"):
        nw_ref = refs[pos]
        pos += 1
    outs_p = refs[pos:pos + n_out]
    outs_s = refs[pos + n_out:pos + 2 * n_out]
    wb_s = refs[-1]
    i = pl.program_id(1)
    n_prompt = pl.num_programs(1) - 1

    @pl.when(i == 0)
    def _():
        if shift:
            wt = jnp.concatenate([w_ref[shift:, :], wn_ref[...]], axis=0)
        else:
            wt = w_ref[...]
        wb_s[...] = wt.T.astype(BF16)

    @pl.when(i < n_prompt)
    def _():
        _proj_epilogue(_dot(hp_ref[...], wb_s[...]), kind, nw_ref, outs_p)

    @pl.when(i == n_prompt)
    def _():
        _proj_epilogue(_dot(hs_ref[...], wb_s[...]), kind, nw_ref, outs_s)


def _proj(hp, hs, wt, col0, kind, nw=None, n_tiles=1):
    mp, d = hp.shape
    ms = hs.shape[0]
    tn = W_ATT
    tm = min(mp, 1024)
    n_prompt = mp // tm
    blk0 = col0 // tn
    shift = col0 - blk0 * tn
    assert shift in (0, SUBLANES)
    grid = (n_tiles, n_prompt + 1)
    prow = lambda j, i: (jnp.minimum(i, n_prompt - 1), 0)
    w_mode = {} if n_tiles > 1 else dict(pipeline_mode=pl.Buffered(1))
    in_specs = [pl.BlockSpec((tm, d), prow),
                pl.BlockSpec((ms, d), lambda j, i: (0, 0)),
                pl.BlockSpec((None, tn, d), lambda j, i: (0, blk0 + j, 0), **w_mode)]
    args = [hp, hs, wt]
    if shift:
        per = tn // shift
        in_specs.append(pl.BlockSpec((None, shift, d), lambda j, i: (0, (blk0 + j + 1) * per, 0)))
        args.append(wt)
    if kind in ("qnorm", "knorm"):
        in_specs.append(pl.BlockSpec((1, tn), lambda j, i: (0, 0)))
        args.append(nw)
    n_cols = tn * n_tiles
    dtypes = (F32, BF16) if kind in ("kv", "knorm") else (BF16,)
    p_block = pl.BlockSpec((tm, tn), lambda j, i: (jnp.minimum(i, n_prompt - 1), j))
    s_block = pl.BlockSpec((ms, tn), lambda j, i: (0, j))
    out_shape = tuple(jax.ShapeDtypeStruct((mp, n_cols), t) for t in dtypes) + \
        tuple(jax.ShapeDtypeStruct((ms, n_cols), t) for t in dtypes)
    out_specs = (p_block,) * len(dtypes) + (s_block,) * len(dtypes)
    return pl.pallas_call(
        functools.partial(_proj_body, kind=kind, n_out=len(dtypes), shift=shift),
        grid=grid,
        in_specs=in_specs,
        out_specs=out_specs,
        out_shape=out_shape,
        scratch_shapes=[pltpu.VMEM((d, tn), BF16)],
        compiler_params=_cparams(("parallel", "arbitrary"), vmem_mib=56),
        name="proj_" + kind,
    )(*args)


def _forget_weight(wft_ref):
    wft = wft_ref[...]
    pad = jnp.zeros((LANES - wft.shape[0], wft.shape[1]), F32)
    return jnp.concatenate([wft, pad], axis=0).T.astype(BF16)


def _logf_body(h_ref, wft_ref, brow_ref, l_ref, c2_ref, lf_ref, fcol_ref, frow_ref, qbias_ref, kbias_ref,
               ccol_s, wf_s):
    i = pl.program_id(0)
    tm = h_ref.shape[0]

    @pl.when(i == 0)
    def _():
        ccol_s[...] = jnp.zeros_like(ccol_s)
        wf_s[...] = _forget_weight(wft_ref)

    lf = _log_sigmoid(_dot(h_ref[...], wf_s[...]) + brow_ref[...])
    lf_ref[...] = lf[:, :N_HEADS]
    f_col = _dot_split_rhs(l_ref[...], lf, 3) + ccol_s[...]
    fcol_ref[...] = f_col[:, :N_HEADS]
    ccol_s[...] = f_col[tm - 1:tm, :]
    frow_ref[...] = f_col.T[:N_HEADS, :]
    lane = lax.broadcasted_iota(jnp.int32, f_col.shape, 1)
    f2 = f_col * LOG2E
    qbias_ref[...] = _split_bias_rows(f2 - c2_ref[...], lane)
    kbias_ref[...] = _split_bias_rows(-f2, lane)


def _logf_prompt(h, wt, f_row0, b_row, c2):
    m, d = h.shape
    tm = min(m, 512)
    l_mat = _tri(tm, "row_ge_col")
    const = lambda i: (0, 0)
    return pl.pallas_call(
        _logf_body,
        grid=(m // tm,),
        in_specs=[pl.BlockSpec((tm, d), lambda i: (i, 0)),
                  pl.BlockSpec((None, N_HEADS, d), lambda i: (0, f_row0 // N_HEADS, 0)),
                  pl.BlockSpec((1, LANES), const),
                  pl.BlockSpec((tm, tm), const),
                  pl.BlockSpec((1, 1), const)],
        out_specs=(pl.BlockSpec((tm, N_HEADS), lambda i: (i, 0)),
                   pl.BlockSpec((tm, N_HEADS), lambda i: (i, 0)),
                   pl.BlockSpec((N_HEADS, tm), lambda i: (0, i)),
                   pl.BlockSpec((tm, LANES), lambda i: (i, 0)),
                   pl.BlockSpec((tm, LANES), lambda i: (i, 0))),
        out_shape=(jax.ShapeDtypeStruct((m, N_HEADS), F32),
                   jax.ShapeDtypeStruct((m, N_HEADS), F32),
                   jax.ShapeDtypeStruct((N_HEADS, m), F32),
                   jax.ShapeDtypeStruct((m, LANES), BF16),
                   jax.ShapeDtypeStruct((m, LANES), BF16)),
        scratch_shapes=[pltpu.VMEM((1, LANES), F32), pltpu.VMEM((d, LANES), BF16)],
        compiler_params=_cparams(("arbitrary",)),
        name="logf_prompt",
    )(h, wt, b_row, l_mat, c2)


def _logf_sample_body(h_ref, wft_ref, brow_ref, bu_ref, x_ref, ms_ref, lf_ref, c_ref, r_ref, *, kb):
    lf = _log_sigmoid(_dot(h_ref[...], _forget_weight(wft_ref)) + brow_ref[...])
    lf_ref[...] = lf[:, :N_HEADS]
    lft = lf.T[:2 * SUBLANES, :]
    c_ref[...] = _dot_split_lhs(lft, bu_ref[...], 3)[:N_HEADS, :]
    n_blocks = x_ref.shape[1] // kb
    carry = jnp.zeros((x_ref.shape[0], 1), F32)
    for blk in range(n_blocks - 1, -1, -1):
        x = x_ref[:, blk * kb:(blk + 1) * kb]
        cum = _dot_split_lhs(x, ms_ref[...], 3)
        r_ref[:, blk * kb:(blk + 1) * kb] = cum + carry
        carry = carry + cum[:, 0:1] + x[:, 0:1]


def _logf_sample(h_s, wt, f_row0, b_row, past_logf_t, dec_seq):
    n_rows, d = h_s.shape
    n_bh, p_len = past_logf_t.shape
    kb = min(p_len, 512)
    r = lax.broadcasted_iota(jnp.int32, (n_rows, n_rows), 0)
    c = lax.broadcasted_iota(jnp.int32, (n_rows, n_rows), 1)
    bu = ((r // dec_seq == c // dec_seq) & (r <= c)).astype(BF16)
    ms = _tri(kb, "row_gt_col")
    whole = lambda shape: pl.BlockSpec(shape, lambda i: (0,) * len(shape))
    return pl.pallas_call(
        functools.partial(_logf_sample_body, kb=kb),
        grid=(1,),
        in_specs=[whole((n_rows, d)),
                  pl.BlockSpec((None, N_HEADS, d), lambda i: (0, f_row0 // N_HEADS, 0)),
                  whole((1, LANES)), whole((n_rows, n_rows)), whole((n_bh, p_len)), whole((kb, kb))],
        out_specs=(whole((n_rows, N_HEADS)), whole((N_HEADS, n_rows)), whole((n_bh, p_len))),
        out_shape=(jax.ShapeDtypeStruct((n_rows, N_HEADS), F32),
                   jax.ShapeDtypeStruct((N_HEADS, n_rows), F32),
                   jax.ShapeDtypeStruct((n_bh, p_len), F32)),
        compiler_params=_cparams(("arbitrary",), vmem_mib=32),
        name="logf_sample",
    )(h_s, wt, b_row, bu, past_logf_t, ms)


SB_TQ = 256


def _sb_block(q, k, v, m_mat, carry, mask):
    z = _dot_nt(q, k)
    lsn = jnp.minimum(-z, 0.0) - jnp.log(1.0 + jnp.exp2(-jnp.abs(z))) * LOG2E
    lsp = lsn + z
    if mask is not None:
        lsn = jnp.where(mask, lsn, 0.0)
    cum = _dot_split_lhs(lsn, m_mat, 2)
    if carry is not None:
        cum = cum + carry
    w = jnp.exp2(lsp + cum)
    if mask is not None:
        w = jnp.where(mask, w, 0.0)
    o = _dot(w.astype(BF16), v)
    new_carry = cum[:, 0:1] + lsn[:, 0:1]
    return o, new_carry


def _sb_fast_body(q_ref, kd_ref, vd_ref, kp_ref, vp_ref, gs_ref, m_ref, a_ref, c_ref):
    i = pl.program_id(0)
    tq = q_ref.shape[0]
    has_prev = i > 0
    row = lax.broadcasted_iota(jnp.int32, (tq, tq), 0)
    col = lax.broadcasted_iota(jnp.int32, (tq, tq), 1)
    dmask = col < row
    pmask = jnp.logical_and(has_prev, col >= 0)
    lane = lax.broadcasted_iota(jnp.int32, (tq, LANES), 1)
    m_mat = m_ref[...]
    cacc = jnp.full((tq, LANES), NEG_BIG, F32)
    for hh in range(N_HEADS):
        sl = slice(hh * HEAD_DIM, (hh + 1) * HEAD_DIM)
        q = q_ref[:, sl]
        od, cd = _sb_block(q, kd_ref[:, sl], vd_ref[:, sl], m_mat, None, dmask)
        op, cp = _sb_block(q, kp_ref[:, sl], vp_ref[:, sl], m_mat, cd, pmask)
        a_ref[:, sl] = ((od + op) * gs_ref[:, sl].astype(F32)).astype(BF16)
        cacc = jnp.where(lane == hh, cp, cacc)
    c_ref[...] = cacc


def _sb_rest_body(q_ref, k_ref, v_ref, gs_ref, cin_ref, afast_ref, m_ref, a_ref, o_s, c_s):
    i = pl.program_id(0)
    s = pl.program_id(1)
    j = i - 2 - s
    tq = q_ref.shape[0]

    @pl.when(s == 0)
    def _():
        o_s[...] = jnp.zeros_like(o_s)
        c_s[...] = cin_ref[...]

    active = jnp.logical_and(j >= 0, jnp.max(c_s[...]) > SB_SKIP_LOG2)

    @pl.when(active)
    def _():
        lane = lax.broadcasted_iota(jnp.int32, (tq, LANES), 1)
        m_mat = m_ref[...]
        c_all = c_s[...]
        cacc = c_all
        for hh in range(N_HEADS):
            sl = slice(hh * HEAD_DIM, (hh + 1) * HEAD_DIM)
            carry = jnp.sum(jnp.where(lane == hh, c_all, 0.0), axis=1, keepdims=True)
            o, cn = _sb_block(q_ref[:, sl], k_ref[:, sl], v_ref[:, sl], m_mat, carry, None)
            o_s[:, sl] += o
            cacc = jnp.where(lane == hh, cn, cacc)
        c_s[...] = cacc

    @pl.when(s == pl.num_programs(1) - 1)
    def _():
        a_ref[...] = (afast_ref[...].astype(F32) + o_s[...] * gs_ref[...].astype(F32)).astype(BF16)


def _sb_prompt(q, k, v, gs):
    t, w = q.shape
    tq = min(SB_TQ, t)
    nq = t // tq
    m_mat = _tri(tq, "row_gt_col")
    blk = lambda f: pl.BlockSpec((tq, w), f)
    a_fast, carry = pl.pallas_call(
        _sb_fast_body,
        grid=(nq,),
        in_specs=[blk(lambda i: (i, 0)), blk(lambda i: (i, 0)), blk(lambda i: (i, 0)),
                  blk(lambda i: (jnp.maximum(i - 1, 0), 0)), blk(lambda i: (jnp.maximum(i - 1, 0), 0)),
                  blk(lambda i: (i, 0)),
                  pl.BlockSpec((tq, tq), lambda i: (0, 0))],
        out_specs=(blk(lambda i: (i, 0)), pl.BlockSpec((tq, LANES), lambda i: (i, 0))),
        out_shape=(jax.ShapeDtypeStruct((t, w), BF16), jax.ShapeDtypeStruct((t, LANES), F32)),
        compiler_params=_cparams(("parallel",), vmem_mib=48),
        name="sb_prompt_fast",
    )(q, k, v, k, v, gs, m_mat)
    if nq <= 2:
        return a_fast

    def rest(a_fast, carry):
        kidx = lambda i, s: (jnp.maximum(i - 2 - s, 0), 0)
        return pl.pallas_call(
            _sb_rest_body,
            grid=(nq, nq - 2),
            in_specs=[blk(lambda i, s: (i, 0)), blk(kidx), blk(kidx), blk(lambda i, s: (i, 0)),
                      pl.BlockSpec((tq, LANES), lambda i, s: (i, 0)),
                      blk(lambda i, s: (i, 0)),
                      pl.BlockSpec((tq, tq), lambda i, s: (0, 0))],
            out_specs=blk(lambda i, s: (i, 0)),
            out_shape=jax.ShapeDtypeStruct((t, w), BF16),
            scratch_shapes=[pltpu.VMEM((tq, w), F32), pltpu.VMEM((tq, LANES), F32)],
            compiler_params=_cparams(("parallel", "arbitrary"), vmem_mib=48),
            name="sb_prompt_rest",
        )(q, k, v, gs, carry, a_fast, m_mat)

    need_rest = jnp.max(carry[2 * tq:, :]) > SB_SKIP_LOG2
    return lax.cond(need_rest, rest, lambda a, c: a, a_fast, carry)


FOX_T = 512
FOX_WIN_TILES = (2, 4, 6)
FOX_ZB_MAX = 40.0
FOX_SKIP_LOG = -104.0


def _head_column(blk, hh):
    lane8 = lax.broadcasted_iota(jnp.int32, blk.shape, 1)
    return jnp.sum(jnp.where(lane8 == hh, blk, 0.0), axis=1, keepdims=True)


BIAS_ONE_LANE = 3 * N_HEADS


def _split_bias_rows(f, lane):
    hi = f.astype(BF16).astype(F32)
    r1 = f - hi
    mid = r1.astype(BF16).astype(F32)
    low = (r1 - mid).astype(BF16).astype(F32)
    out = jnp.where(lane < N_HEADS, hi,
                    jnp.where(lane < 2 * N_HEADS, pltpu.roll(mid, N_HEADS, axis=1),
                              jnp.where(lane < BIAS_ONE_LANE, pltpu.roll(low, 2 * N_HEADS, axis=1),
                                        jnp.where(lane == BIAS_ONE_LANE, 1.0, 0.0))))
    return out.astype(BF16)


def _bias_selectors():
    src = lax.broadcasted_iota(jnp.int32, (N_HEADS, LANES, LANES), 1)
    dst = lax.broadcasted_iota(jnp.int32, (N_HEADS, LANES, LANES), 2)
    head = lax.broadcasted_iota(jnp.int32, (N_HEADS, LANES, LANES), 0)

    def sel(bias_base, one_base):
        term = dst - bias_base
        takes_bias = (term >= 0) & (term < 3) & (src == term * N_HEADS + head)
        takes_one = (dst >= one_base) & (dst < one_base + 3) & (src == BIAS_ONE_LANE)
        return (takes_bias | takes_one).astype(BF16)

    return jnp.stack([sel(0, 3), sel(3, 0)])


def _fox_fast_body(live_ref, q_ref, k_ref, v_ref, qb_ref, kb_ref, sel_ref, gs_ref, a_ref, kaug_s):
    hh = pl.program_id(0)
    qb = pl.program_id(1)
    nq = pl.num_programs(1)
    tb = q_ref.shape[0]
    t = k_ref.shape[0]

    @pl.when(qb == 0)
    def _():
        def build(c, carry):
            start = pl.multiple_of(c * tb, tb)
            kaug_s[pl.ds(start, tb), :] = _dot(kb_ref[pl.ds(start, tb), :], sel_ref[1]).astype(BF16)
            return carry
        lax.fori_loop(0, nq, build, 0)

    q_start = pl.multiple_of(qb * tb, tb)
    q_end = q_start + tb
    q2 = jnp.concatenate([q_ref[...], _dot(qb_ref[...], sel_ref[0]).astype(BF16)], axis=1)
    row = lax.broadcasted_iota(jnp.int32, (tb, 1), 0)
    live = live_ref[hh * nq + qb]

    def run(tiles):
        span = min(tiles * tb, t)
        lane_w = lax.broadcasted_iota(jnp.int32, (span, LANES), 1)
        ones_blk = jnp.where(lane_w == 0, 1.0, 0.0).astype(BF16)
        col = lax.broadcasted_iota(jnp.int32, (1, span), 1)

        def window(j, acc):
            upper = q_end - j * span
            start = pl.multiple_of(jnp.maximum(upper - span, 0), tb)
            k2 = jnp.concatenate([k_ref[pl.ds(start, span), :], kaug_s[pl.ds(start, span), :]], axis=1)
            p = jnp.exp2(_dot_nt(q2, k2))
            valid = jnp.logical_and(col - (q_start - start) <= row, col < upper - start)
            p = jnp.where(valid, p, 0.0).astype(BF16)
            v2 = jnp.concatenate([v_ref[pl.ds(start, span), :], ones_blk], axis=1)
            return acc + _dot(p, v2)

        n_win = (live + (tiles - 1)) // tiles
        acc = lax.fori_loop(0, n_win, window, jnp.zeros((tb, 2 * HEAD_DIM), F32))
        o = acc[:, :HEAD_DIM] / acc[:, HEAD_DIM:HEAD_DIM + 1]
        a_ref[...] = (o * gs_ref[...].astype(F32)).astype(BF16)

    bounds = (0,) + FOX_WIN_TILES
    for lo_t, hi_t in zip(bounds[:-1], bounds[1:]):
        last = hi_t == FOX_WIN_TILES[-1]
        cond = live > lo_t if last else jnp.logical_and(live > lo_t, live <= hi_t)
        pl.when(cond)(functools.partial(run, hi_t))


def _fox_slow_body(q_ref, k_ref, v_ref, fcol_ref, frow_ref, gs_ref, a_ref):
    hh = pl.program_id(0)
    qb = pl.program_id(1)
    tb = q_ref.shape[0]
    q = q_ref[...]
    fq = _head_column(fcol_ref[...], hh) * LOG2E

    def scores(kb):
        start = pl.multiple_of(kb * tb, tb)
        k = k_ref[pl.ds(start, tb), :]
        fk = frow_ref[0, :, pl.ds(start, tb)] * LOG2E
        return _dot_nt(q, k) + (fq - fk), v_ref[pl.ds(start, tb), :]

    def update(s, v, carry):
        m, l, acc = carry
        m_new = jnp.maximum(m, jnp.max(s, axis=1, keepdims=True))
        alpha = jnp.exp2(m - m_new)
        p = jnp.exp2(s - m_new)
        l = alpha * l + jnp.sum(p, axis=1, keepdims=True)
        acc = alpha * acc + _dot(p.astype(BF16), v)
        return m_new, l, acc

    def body(kb, carry):
        s, v = scores(kb)
        return update(s, v, carry)

    init = (jnp.full((tb, 1), NEG_BIG, F32), jnp.zeros((tb, 1), F32), jnp.zeros((tb, HEAD_DIM), F32))
    carry = lax.fori_loop(0, qb, body, init)
    s, v = scores(qb)
    row = lax.broadcasted_iota(jnp.int32, (tb, tb), 0)
    col = lax.broadcasted_iota(jnp.int32, (tb, tb), 1)
    s = jnp.where(col <= row, s, NEG_BIG)
    _, l, acc = update(s, v, carry)
    a_ref[...] = ((acc / l) * gs_ref[...].astype(F32)).astype(BF16)


def _fox_prompt(q, k, v, f_col, f_row, gs, zb, q_bias, k_bias):
    t, w = q.shape
    tb = min(FOX_T, t)
    nq = t // tb
    out_shape = jax.ShapeDtypeStruct((t, w), BF16)

    def fast(q, k, v, f_col, f_row, gs, q_bias, k_bias):
        f_start = f_row[:, ::tb]
        f_end = f_row[:, tb - 1::tb]
        dead = (f_start[:, :, None] - f_end[:, None, :]) < FOX_SKIP_LOG
        lo = jnp.sum(dead, axis=2).astype(jnp.int32)
        live = (jnp.arange(1, nq + 1, dtype=jnp.int32)[None, :] - lo).reshape(-1)
        head_blk = lambda h, i, live: (i, h)
        head_all = lambda h, i, live: (0, h)
        grid_spec = pltpu.PrefetchScalarGridSpec(
            num_scalar_prefetch=1,
            grid=(N_HEADS, nq),
            in_specs=[pl.BlockSpec((tb, HEAD_DIM), head_blk),
                      pl.BlockSpec((t, HEAD_DIM), head_all),
                      pl.BlockSpec((t, HEAD_DIM), head_all),
                      pl.BlockSpec((tb, LANES), lambda h, i, live: (i, 0)),
                      pl.BlockSpec((t, LANES), lambda h, i, live: (0, 0), pipeline_mode=pl.Buffered(1)),
                      pl.BlockSpec((2, None, LANES, LANES), lambda h, i, live: (0, h, 0, 0)),
                      pl.BlockSpec((tb, HEAD_DIM), head_blk)],
            out_specs=pl.BlockSpec((tb, HEAD_DIM), head_blk),
            scratch_shapes=[pltpu.VMEM((t, LANES), BF16)])
        return pl.pallas_call(
            _fox_fast_body, grid_spec=grid_spec, out_shape=out_shape,
            compiler_params=_cparams(("parallel", "arbitrary"), vmem_mib=48),
            name="fox_prompt_fast",
        )(live, q, k, v, q_bias, k_bias, _bias_selectors(), gs)

    def slow(q, k, v, f_col, f_row, gs, q_bias, k_bias):
        f_row3 = f_row.reshape(N_HEADS, 1, t)
        return pl.pallas_call(
            _fox_slow_body,
            grid=(N_HEADS, nq),
            in_specs=[pl.BlockSpec((tb, HEAD_DIM), lambda h, i: (i, h)),
                      pl.BlockSpec((t, HEAD_DIM), lambda h, i: (0, h)),
                      pl.BlockSpec((t, HEAD_DIM), lambda h, i: (0, h)),
                      pl.BlockSpec((tb, N_HEADS), lambda h, i: (i, 0)),
                      pl.BlockSpec((1, 1, t), lambda h, i: (h, 0, 0)),
                      pl.BlockSpec((tb, HEAD_DIM), lambda h, i: (i, h))],
            out_specs=pl.BlockSpec((tb, HEAD_DIM), lambda h, i: (i, h)),
            out_shape=out_shape,
            compiler_params=_cparams(("parallel", "parallel"), vmem_mib=48),
            name="fox_prompt_slow",
        )(q, k, v, f_col, f_row3, gs)

    return lax.cond(zb <= FOX_ZB_MAX, fast, slow, q, k, v, f_col, f_row, gs, q_bias, k_bias)


DEC_KB = 1024
SB_DEC_WIN = 256


def _own_head_mask():
    row = lax.broadcasted_iota(jnp.int32, (SUBLANES, W_ATT), 0)
    col = lax.broadcasted_iota(jnp.int32, (SUBLANES, W_ATT), 1)
    return (col // HEAD_DIM) == row


def _expand_q(q_ref, qx_s):
    n_tok = q_ref.shape[0]
    qf = q_ref[...].astype(F32)
    own = _own_head_mask()
    for tkn in range(n_tok):
        rep = jnp.broadcast_to(qf[tkn:tkn + 1, :], (SUBLANES, W_ATT))
        qx_s[tkn * N_HEADS:(tkn + 1) * N_HEADS, :] = jnp.where(own, rep, 0.0)


def _pad_new(kn_ref, vn_ref, knp_s, vnp_s):
    n_tok = kn_ref.shape[0]
    knp_s[...] = jnp.zeros_like(knp_s)
    vnp_s[...] = jnp.zeros_like(vnp_s)
    knp_s[0:n_tok, :] = kn_ref[...]
    vnp_s[0:n_tok, :] = vn_ref[...]


def _gather_heads(c_ref, dst_s):
    n_keys = dst_s.shape[0]
    for hh in range(N_HEADS):
        dst_s[:, hh * HEAD_DIM:(hh + 1) * HEAD_DIM] = (
            c_ref[0, pl.ds(hh, n_keys, stride=N_HEADS), :].astype(BF16))


def _collapse_heads(o_full, out_s):
    n_tok = o_full.shape[0] // N_HEADS
    own = _own_head_mask()
    for tkn in range(n_tok):
        blk = jnp.where(own, o_full[tkn * N_HEADS:(tkn + 1) * N_HEADS, :], 0.0)
        out_s[tkn:tkn + 1, :] = jnp.sum(blk, axis=0, keepdims=True)


def _sb_dec_fast_body(q_ref, kn_ref, vn_ref, kc_ref, vc_ref, gs_ref, mn_ref, mc_ref, a_ref, c_ref,
                      qx_s, knp_s, vnp_s, kx_s, vx_s, out_s):
    n_rows = qx_s.shape[0]
    _expand_q(q_ref, qx_s)
    _pad_new(kn_ref, vn_ref, knp_s, vnp_s)
    qx = qx_s[...].astype(BF16)
    row = lax.broadcasted_iota(jnp.int32, (n_rows, LANES), 0)
    col = lax.broadcasted_iota(jnp.int32, (n_rows, LANES), 1)
    mask = col < row // N_HEADS
    o_new, carry = _sb_block(qx, knp_s[...], vnp_s[...], mn_ref[...], None, mask)
    _gather_heads(kc_ref, kx_s)
    _gather_heads(vc_ref, vx_s)
    o_win, carry = _sb_block(qx, kx_s[...], vx_s[...], mc_ref[...], carry, None)
    _collapse_heads(o_new + o_win, out_s)
    a_ref[...] = (out_s[...] * gs_ref[...].astype(F32)).astype(BF16)
    c_ref[...] = jnp.broadcast_to(carry, c_ref.shape)


def _sb_dec_rest_body(q_ref, kc_ref, vc_ref, gs_ref, cin_ref, afast_ref, mc_ref, a_ref,
                      qx_s, kx_s, vx_s, acc_s, c_s, out_s):
    s = pl.program_id(1)

    @pl.when(s == 0)
    def _():
        _expand_q(q_ref, qx_s)
        acc_s[...] = jnp.zeros_like(acc_s)
        c_s[...] = cin_ref[:, 0:1]

    @pl.when(jnp.max(c_s[...]) > SB_SKIP_LOG2)
    def _():
        _gather_heads(kc_ref, kx_s)
        _gather_heads(vc_ref, vx_s)
        o, carry = _sb_block(qx_s[...].astype(BF16), kx_s[...], vx_s[...], mc_ref[...], c_s[...], None)
        acc_s[...] += o
        c_s[...] = carry

    @pl.when(s == pl.num_programs(1) - 1)
    def _():
        _collapse_heads(acc_s[...], out_s)
        a_ref[...] = (afast_ref[...].astype(F32) + out_s[...] * gs_ref[...].astype(F32)).astype(BF16)


def _sb_decode(q, k_new, v_new, k_cache, v_cache, gs, dec_seq):
    n_rows_all, w = q.shape
    n_b = n_rows_all // dec_seq
    p_len = k_cache.shape[1] // N_HEADS
    win = min(SB_DEC_WIN, p_len)
    n_win = p_len // win
    n_rows = dec_seq * N_HEADS
    m_new = _tri(LANES, "row_gt_col")
    m_win = _tri(win, "row_gt_col")
    tok = lambda b: (b, 0)
    a_fast, carry = pl.pallas_call(
        _sb_dec_fast_body,
        grid=(n_b,),
        in_specs=[pl.BlockSpec((dec_seq, w), tok), pl.BlockSpec((dec_seq, w), tok),
                  pl.BlockSpec((dec_seq, w), tok),
                  pl.BlockSpec((1, win * N_HEADS, HEAD_DIM), lambda b: (b, n_win - 1, 0)),
                  pl.BlockSpec((1, win * N_HEADS, HEAD_DIM), lambda b: (b, n_win - 1, 0)),
                  pl.BlockSpec((dec_seq, w), tok),
                  pl.BlockSpec((LANES, LANES), lambda b: (0, 0)),
                  pl.BlockSpec((win, win), lambda b: (0, 0))],
        out_specs=(pl.BlockSpec((dec_seq, w), tok), pl.BlockSpec((n_rows, LANES), tok)),
        out_shape=(jax.ShapeDtypeStruct((n_rows_all, w), BF16),
                   jax.ShapeDtypeStruct((n_b * n_rows, LANES), F32)),
        scratch_shapes=[pltpu.VMEM((n_rows, w), F32), pltpu.VMEM((LANES, w), BF16),
                        pltpu.VMEM((LANES, w), BF16), pltpu.VMEM((win, w), BF16),
                        pltpu.VMEM((win, w), BF16), pltpu.VMEM((dec_seq, w), F32)],
        compiler_params=_cparams(("parallel",), vmem_mib=48),
        name="sb_decode_fast",
    )(q, k_new, v_new, k_cache, v_cache, gs, m_new, m_win)
    if n_win <= 1:
        return a_fast

    def rest(a_fast, carry):
        tok2 = lambda b, s: (b, 0)
        older = lambda b, s: (b, n_win - 2 - s, 0)
        return pl.pallas_call(
            _sb_dec_rest_body,
            grid=(n_b, n_win - 1),
            in_specs=[pl.BlockSpec((dec_seq, w), tok2),
                      pl.BlockSpec((1, win * N_HEADS, HEAD_DIM), older),
                      pl.BlockSpec((1, win * N_HEADS, HEAD_DIM), older),
                      pl.BlockSpec((dec_seq, w), tok2),
                      pl.BlockSpec((n_rows, LANES), tok2),
                      pl.BlockSpec((dec_seq, w), tok2),
                      pl.BlockSpec((win, win), lambda b, s: (0, 0))],
            out_specs=pl.BlockSpec((dec_seq, w), tok2),
            out_shape=jax.ShapeDtypeStruct((n_rows_all, w), BF16),
            scratch_shapes=[pltpu.VMEM((n_rows, w), F32), pltpu.VMEM((win, w), BF16),
                            pltpu.VMEM((win, w), BF16), pltpu.VMEM((n_rows, w), F32),
                            pltpu.VMEM((n_rows, 1), F32), pltpu.VMEM((dec_seq, w), F32)],
            compiler_params=_cparams(("parallel", "arbitrary"), vmem_mib=48),
            name="sb_decode_rest",
        )(q, k_cache, v_cache, gs, carry, a_fast, m_win)

    need_rest = jnp.max(carry) > SB_SKIP_LOG2
    return lax.cond(need_rest, rest, lambda a, c: a, a_fast, carry)


def _fox_dec_body(q_ref, kn_ref, vn_ref, kc_ref, vc_ref, gs_ref, cn_ref, r_ref, a_ref,
                  qx_s, knp_s, vnp_s, kx_s, vx_s, acc_s, m_s, l_s, fq_s, out_s):
    s = pl.program_id(1)
    n_rows = qx_s.shape[0]
    n_tok = n_rows // N_HEADS

    def update(sc, v):
        m = m_s[...]
        m_new = jnp.maximum(m, jnp.max(sc, axis=1, keepdims=True))
        alpha = jnp.exp2(m - m_new)
        p = jnp.exp2(sc - m_new)
        l_s[...] = alpha * l_s[...] + jnp.sum(p, axis=1, keepdims=True)
        acc_s[...] = alpha * acc_s[...] + _dot(p.astype(BF16), v)
        m_s[...] = m_new

    @pl.when(s == 0)
    def _():
        _expand_q(q_ref, qx_s)
        _pad_new(kn_ref, vn_ref, knp_s, vnp_s)
        m_s[...] = jnp.full_like(m_s, NEG_BIG)
        l_s[...] = jnp.zeros_like(l_s)
        acc_s[...] = jnp.zeros_like(acc_s)
        row = lax.broadcasted_iota(jnp.int32, (n_rows, LANES), 0)
        col = lax.broadcasted_iota(jnp.int32, (n_rows, LANES), 1)
        cn = jnp.concatenate([cn_ref[0]] * n_tok, axis=0) * LOG2E
        fq = jnp.sum(jnp.where(col == row // N_HEADS, cn, 0.0), axis=1, keepdims=True)
        fq_s[...] = fq
        sc = _dot_nt(qx_s[...].astype(BF16), knp_s[...]) + (fq - cn)
        sc = jnp.where(col <= row // N_HEADS, sc, NEG_BIG)
        update(sc, vnp_s[...])

    _gather_heads(kc_ref, kx_s)
    _gather_heads(vc_ref, vx_s)
    r = jnp.concatenate([r_ref[...]] * n_tok, axis=0) * LOG2E
    update(_dot_nt(qx_s[...].astype(BF16), kx_s[...]) + (fq_s[...] + r), vx_s[...])

    @pl.when(s == pl.num_programs(1) - 1)
    def _():
        _collapse_heads(acc_s[...] / l_s[...], out_s)
        a_ref[...] = (out_s[...] * gs_ref[...].astype(F32)).astype(BF16)


def _fox_decode(q, k_new, v_new, k_cache, v_cache, gs, c_new, r_past, dec_seq):
    n_rows_all, w = q.shape
    n_b = n_rows_all // dec_seq
    p_len = k_cache.shape[1] // N_HEADS
    kb = min(DEC_KB, p_len)
    nkb = p_len // kb
    n_rows = dec_seq * N_HEADS
    tok = lambda b, s: (b, 0)
    cache = lambda b, s: (b, s, 0)
    return pl.pallas_call(
        _fox_dec_body,
        grid=(n_b, nkb),
        in_specs=[pl.BlockSpec((dec_seq, w), tok), pl.BlockSpec((dec_seq, w), tok),
                  pl.BlockSpec((dec_seq, w), tok),
                  pl.BlockSpec((1, kb * N_HEADS, HEAD_DIM), cache),
                  pl.BlockSpec((1, kb * N_HEADS, HEAD_DIM), cache),
                  pl.BlockSpec((dec_seq, w), tok),
                  pl.BlockSpec((1, N_HEADS, LANES), lambda b, s: (b, 0, 0)),
                  pl.BlockSpec((N_HEADS, kb), lambda b, s: (b, s))],
        out_specs=pl.BlockSpec((dec_seq, w), tok),
        out_shape=jax.ShapeDtypeStruct((n_rows_all, w), BF16),
        scratch_shapes=[pltpu.VMEM((n_rows, w), F32), pltpu.VMEM((LANES, w), BF16),
                        pltpu.VMEM((LANES, w), BF16), pltpu.VMEM((kb, w), BF16),
                        pltpu.VMEM((kb, w), BF16), pltpu.VMEM((n_rows, w), F32),
                        pltpu.VMEM((n_rows, 1), F32), pltpu.VMEM((n_rows, 1), F32),
                        pltpu.VMEM((n_rows, 1), F32), pltpu.VMEM((dec_seq, w), F32)],
        compiler_params=_cparams(("parallel", "arbitrary"), vmem_mib=48),
        name="fox_decode",
    )(q, k_new, v_new, k_cache, v_cache, gs, c_new, r_past)


def _post_body(x_ref, asb_ref, afx_ref, msb_ref, mfx_ref, wsb_ref, wfx_ref, wo_ref, y_ref):
    u_sb = _dot(asb_ref[...], wsb_ref[...])
    u_fx = _dot(afx_ref[...], wfx_ref[...])
    merged = msb_ref[...].astype(F32) * u_sb + mfx_ref[...].astype(F32) * u_fx
    y_ref[...] = x_ref[...] + _dot(merged.astype(BF16), wo_ref[...])


def _post(x2d, a_sb, a_fx, m_sig, w_sb, w_fx, w_o):
    m, d = x2d.shape
    w = a_sb.shape[1]
    tm = min(m, 256)
    row = lambda i: (i, 0)
    const = lambda i: (0, 0)
    resident = functools.partial(pl.BlockSpec, index_map=const, pipeline_mode=pl.Buffered(1))
    return pl.pallas_call(
        _post_body,
        grid=(m // tm,),
        in_specs=[pl.BlockSpec((tm, d), row), pl.BlockSpec((tm, w), row), pl.BlockSpec((tm, w), row),
                  pl.BlockSpec((tm, d), lambda i: (i, 0)), pl.BlockSpec((tm, d), lambda i: (i, 1)),
                  resident((w, d)), resident((w, d)), resident((d, d))],
        out_specs=pl.BlockSpec((tm, d), row),
        out_shape=jax.ShapeDtypeStruct((m, d), F32),
        compiler_params=_cparams(("parallel",), vmem_mib=56),
        name="post",
    )(x2d, a_sb, a_fx, m_sig, m_sig, w_sb, w_fx, w_o)


def _project_all(hp, hs, w, qnw, knw):
    seg = lambda c: c * W_ATT
    q_sb = _proj(hp, hs, w, seg(0), "scale")
    k_sb = _proj(hp, hs, w, seg(1), "kv")
    v_sb = _proj(hp, hs, w, seg(2), "kv")
    g_sb = _proj(hp, hs, w, seg(3), "silu")
    q_fx = _proj(hp, hs, w, seg(4), "qnorm", nw=qnw)
    k_fx = _proj(hp, hs, w, seg(5), "knorm", nw=knw)
    v_fx = _proj(hp, hs, w, seg(6), "kv")
    g_fx = _proj(hp, hs, w, seg(7), "silu")
    m_sig = _proj(hp, hs, w, seg(8) + N_HEADS, "sigmoid", n_tiles=2 * D_MODEL // W_ATT)
    calls = (q_sb, k_sb, v_sb, g_sb, q_fx, k_fx, v_fx, g_fx, m_sig)
    prompt = tuple(o for r in calls for o in r[:len(r) // 2])
    decode = tuple(o for r in calls for o in r[len(r) // 2:])
    return prompt, decode


def kernel(x_prompt, x_sample, cache_sb_k, cache_sb_v, cache_fox_k, cache_fox_v, cache_fox_logf,
           norm_w, w_in, b_forget, q_norm_w, k_norm_w, w_branch_sb, w_branch_fox, w_out):
    depth = norm_w.shape[0]
    assert depth == 1, "single-layer step"
    bsz, seq, d = x_prompt.shape
    assert bsz == 1
    n_dec, dec_seq, _ = x_sample.shape
    p_len = cache_sb_k.shape[2]
    n_main = 8 * W_ATT

    w_t = jnp.swapaxes(w_in, 1, 2)
    b_row = jnp.pad(b_forget[0].astype(F32)[None, :], ((0, 0), (0, LANES - N_HEADS)))
    qnw = jnp.tile(q_norm_w[0].astype(F32), N_HEADS)[None, :]
    knw = jnp.tile(k_norm_w[0].astype(F32), N_HEADS)[None, :]
    nw_row = norm_w[0].astype(F32)[None, :]
    w_sb = w_branch_sb[0].astype(BF16)
    w_fx = w_branch_fox[0].astype(BF16)
    w_o = w_out[0].astype(BF16)

    xp = x_prompt.reshape(seq, d)
    xs = x_sample.reshape(n_dec * dec_seq, d)
    hp = _rmsnorm(xp, nw_row)
    hs = _rmsnorm(xs, nw_row)
    prompt_proj, decode_proj = _project_all(hp, hs, w_t, qnw, knw)

    (q_sb, k_sb, k_sb_b, v_sb, v_sb_b, g_sb, q_fx, k_fx, k_fx_b, v_fx, v_fx_b, g_fx,
     m_sig) = prompt_proj
    zb = (1.02 * HEAD_DIM * QK_SCALE) * jnp.max(jnp.abs(q_norm_w[0])) * jnp.max(jnp.abs(k_norm_w[0]))
    zb = zb.astype(F32)
    logf, f_col, f_row, q_bias, k_bias = _logf_prompt(hp, w_t, n_main, b_row, (zb * LOG2E).reshape(1, 1))
    a_sb = _sb_prompt(q_sb, k_sb_b, v_sb_b, g_sb)
    a_fx = _fox_prompt(q_fx, k_fx_b, v_fx_b, f_col, f_row, g_fx, zb, q_bias, k_bias)
    y_prompt = _post(xp, a_sb, a_fx, m_sig, w_sb, w_fx, w_o).reshape(bsz, seq, d)

    (sq_sb, sk_sb, sk_sb_b, sv_sb, sv_sb_b, sg_sb, sq_fx, sk_fx, sk_fx_b, sv_fx, sv_fx_b, sg_fx,
     sm_sig) = decode_proj
    past_logf_t = jnp.transpose(cache_fox_logf[0].astype(F32), (0, 2, 1)).reshape(n_dec * N_HEADS, p_len)
    s_logf, c_all, r_past = _logf_sample(hs, w_t, n_main, b_row, past_logf_t, dec_seq)
    c_new = jnp.transpose(c_all.reshape(N_HEADS, n_dec, dec_seq), (1, 0, 2))
    c_new = jnp.pad(c_new, ((0, 0), (0, 0), (0, LANES - dec_seq)))
    kc_sb = cache_sb_k[0].reshape(n_dec, p_len * N_HEADS, HEAD_DIM)
    vc_sb = cache_sb_v[0].reshape(n_dec, p_len * N_HEADS, HEAD_DIM)
    kc_fx = cache_fox_k[0].reshape(n_dec, p_len * N_HEADS, HEAD_DIM)
    vc_fx = cache_fox_v[0].reshape(n_dec, p_len * N_HEADS, HEAD_DIM)
    sa_sb = _sb_decode(sq_sb, sk_sb_b, sv_sb_b, kc_sb, vc_sb, sg_sb, dec_seq)
    sa_fx = _fox_decode(sq_fx, sk_fx_b, sv_fx_b, kc_fx, vc_fx, sg_fx, c_new, r_past, dec_seq)
    y_sample = _post(xs, sa_sb, sa_fx, sm_sig, w_sb, w_fx, w_o).reshape(n_dec, dec_seq, d)

    hd = (N_HEADS, HEAD_DIM)
    return (y_prompt, y_sample,
            k_sb.reshape(1, bsz, seq, *hd), v_sb.reshape(1, bsz, seq, *hd),
            k_fx.reshape(1, bsz, seq, *hd), v_fx.reshape(1, bsz, seq, *hd),
            logf.reshape(1, bsz, seq, N_HEADS),
            sk_sb.reshape(1, n_dec, dec_seq, *hd), sv_sb.reshape(1, n_dec, dec_seq, *hd),
            sk_fx.reshape(1, n_dec, dec_seq, *hd), sv_fx.reshape(1, n_dec, dec_seq, *hd),
            s_logf.reshape(1, n_dec, dec_seq, N_HEADS))
```

```python
import functools

import jax
import jax.numpy as jnp
from jax import lax
from jax.experimental import pallas as pl
from jax.experimental.pallas import tpu as pltpu

F32 = jnp.float32
BF16 = jnp.bfloat16

D_MODEL = 2048
N_HEADS = 8
HEAD_DIM = 128
W_ATT = N_HEADS * HEAD_DIM
RMS_EPS = 1e-6
QK_SCALE = HEAD_DIM ** -0.5
LOG2E = 1.4426950408889634
LANES = 128
SUBLANES = 8
SB_SKIP_LOG2 = -150.0
NEG_BIG = -1e30
MIB = 1024 * 1024


def _cparams(sem, vmem_mib=None):
    kw = dict(dimension_semantics=sem)
    if vmem_mib is not None:
        kw["vmem_limit_bytes"] = vmem_mib * MIB
    return pltpu.CompilerParams(**kw)


def _softplus_neg_abs(z):
    return jnp.log1p(jnp.exp(-jnp.abs(z)))


def _log_sigmoid(z):
    return jnp.minimum(z, 0.0) - _softplus_neg_abs(z)


def _split_bf16(x, n):
    parts = []
    r = x
    for _ in range(n - 1):
        p = r.astype(BF16)
        parts.append(p)
        r = r - p.astype(F32)
    parts.append(r.astype(BF16))
    return parts


def _dot(a, b):
    return jnp.dot(a, b, preferred_element_type=F32)


def _dot_nt(a, b):
    return lax.dot_general(a, b, (((1,), (1,)), ((), ())), preferred_element_type=F32)


def _dot_split_lhs(x, m, n):
    acc = None
    for p in _split_bf16(x, n):
        t = _dot(p, m)
        acc = t if acc is None else acc + t
    return acc


def _dot_split_rhs(m, x, n):
    acc = None
    for p in _split_bf16(x, n):
        t = _dot(m, p)
        acc = t if acc is None else acc + t
    return acc


def _tri(n, kind):
    r = lax.broadcasted_iota(jnp.int32, (n, n), 0)
    c = lax.broadcasted_iota(jnp.int32, (n, n), 1)
    if kind == "row_gt_col":
        m = r > c
    elif kind == "row_le_col":
        m = r <= c
    elif kind == "row_ge_col":
        m = r >= c
    else:
        raise ValueError(kind)
    return m.astype(BF16)


def _rmsnorm_body(x_ref, w_ref, o_ref):
    x = x_ref[...]
    ms = jnp.mean(x * x, axis=-1, keepdims=True)
    o_ref[...] = (x * lax.rsqrt(ms + RMS_EPS) * w_ref[...]).astype(o_ref.dtype)


def _rmsnorm(x2d, w_row):
    m, d = x2d.shape
    tm = min(m, 512)
    return pl.pallas_call(
        _rmsnorm_body,
        grid=(m // tm,),
        in_specs=[pl.BlockSpec((tm, d), lambda i: (i, 0)),
                  pl.BlockSpec((1, d), lambda i: (0, 0))],
        out_specs=pl.BlockSpec((tm, d), lambda i: (i, 0)),
        out_shape=jax.ShapeDtypeStruct((m, d), BF16),
        compiler_params=_cparams(("parallel",)),
        name="rmsnorm",
    )(x2d, w_row)


def _head_rmsnorm(acc, nw):
    parts = []
    for hh in range(N_HEADS):
        a = acc[:, hh * HEAD_DIM:(hh + 1) * HEAD_DIM]
        ms = jnp.mean(a * a, axis=-1, keepdims=True)
        parts.append(a * lax.rsqrt(ms + RMS_EPS))
    return jnp.concatenate(parts, axis=1) * nw


def _proj_epilogue(acc, kind, nw_ref, outs):
    if kind == "scale":
        outs[0][...] = (acc * (QK_SCALE * LOG2E)).astype(BF16)
    elif kind == "kv":
        outs[0][...] = acc
        outs[1][...] = acc.astype(BF16)
    elif kind == "silu":
        outs[0][...] = (acc * jax.nn.sigmoid(acc)).astype(BF16)
    elif kind == "sigmoid":
        outs[0][...] = jax.nn.sigmoid(acc).astype(BF16)
    elif kind == "qnorm":
        outs[0][...] = (_head_rmsnorm(acc, nw_ref[...]) * (QK_SCALE * LOG2E)).astype(BF16)
    elif kind == "knorm":
        y = _head_rmsnorm(acc, nw_ref[...])
        outs[0][...] = y
        outs[1][...] = y.astype(BF16)
    else:
        raise ValueError(kind)


def _proj_body(*refs, kind, n_out, shift):
    hp_ref, hs_ref, w_ref = refs[:3]
    pos = 3
    wn_ref = nw_ref = None
    if shift:
        wn_ref = refs[pos]
        pos += 1
    if kind in ("qnorm", "knorm"):
        nw_ref = refs[pos]
        pos += 1
    outs_p = refs[pos:pos + n_out]
    outs_s = refs[pos + n_out:pos + 2 * n_out]
    wb_s = refs[-1]
    i = pl.program_id(1)
    n_prompt = pl.num_programs(1) - 1

    @pl.when(i == 0)
    def _():
        if shift:
            wt = jnp.concatenate([w_ref[shift:, :], wn_ref[...]], axis=0)
        else:
            wt = w_ref[...]
        wb_s[...] = wt.T.astype(BF16)

    @pl.when(i < n_prompt)
    def _():
        _proj_epilogue(_dot(hp_ref[...], wb_s[...]), kind, nw_ref, outs_p)

    @pl.when(i == n_prompt)
    def _():
        _proj_epilogue(_dot(hs_ref[...], wb_s[...]), kind, nw_ref, outs_s)


def _proj(hp, hs, wt, col0, kind, nw=None, n_tiles=1):
    mp, d = hp.shape
    ms = hs.shape[0]
    tn = W_ATT
    dtypes = (F32, BF16) if kind in ("kv", "knorm") else (BF16,)
    tm = min(mp, 2048 if (F32 not in dtypes and n_tiles == 1) else 1024)
    n_prompt = mp // tm
    blk0 = col0 // tn
    shift = col0 - blk0 * tn
    assert shift in (0, SUBLANES)
    grid = (n_tiles, n_prompt + 1)
    prow = lambda j, i: (jnp.minimum(i, n_prompt - 1), 0)
    w_mode = {} if n_tiles > 1 else dict(pipeline_mode=pl.Buffered(1))
    in_specs = [pl.BlockSpec((tm, d), prow),
                pl.BlockSpec((ms, d), lambda j, i: (0, 0)),
                pl.BlockSpec((None, tn, d), lambda j, i: (0, blk0 + j, 0), **w_mode)]
    args = [hp, hs, wt]
    if shift:
        per = tn // shift
        in_specs.append(pl.BlockSpec((None, shift, d), lambda j, i: (0, (blk0 + j + 1) * per, 0)))
        args.append(wt)
    if kind in ("qnorm", "knorm"):
        in_specs.append(pl.BlockSpec((1, tn), lambda j, i: (0, 0)))
        args.append(nw)
    n_cols = tn * n_tiles
    p_block = pl.BlockSpec((tm, tn), lambda j, i: (jnp.minimum(i, n_prompt - 1), j))
    s_block = pl.BlockSpec((ms, tn), lambda j, i: (0, j))
    out_shape = tuple(jax.ShapeDtypeStruct((mp, n_cols), t) for t in dtypes) + \
        tuple(jax.ShapeDtypeStruct((ms, n_cols), t) for t in dtypes)
    out_specs = (p_block,) * len(dtypes) + (s_block,) * len(dtypes)
    return pl.pallas_call(
        functools.partial(_proj_body, kind=kind, n_out=len(dtypes), shift=shift),
        grid=grid,
        in_specs=in_specs,
        out_specs=out_specs,
        out_shape=out_shape,
        scratch_shapes=[pltpu.VMEM((d, tn), BF16)],
        compiler_params=_cparams(("parallel", "arbitrary"), vmem_mib=56),
        name="proj_" + kind,
    )(*args)


def _forget_weight(wft_ref):
    wft = wft_ref[...]
    pad = jnp.zeros((LANES - wft.shape[0], wft.shape[1]), F32)
    return jnp.concatenate([wft, pad], axis=0).T.astype(BF16)


def _logf_body(h_ref, wft_ref, brow_ref, l_ref, c2_ref, lf_ref, fcol_ref, frow_ref, qbias_ref, kbias_ref,
               ccol_s, wf_s):
    i = pl.program_id(0)
    tm = h_ref.shape[0]

    @pl.when(i == 0)
    def _():
        ccol_s[...] = jnp.zeros_like(ccol_s)
        wf_s[...] = _forget_weight(wft_ref)

    lf = _log_sigmoid(_dot(h_ref[...], wf_s[...]) + brow_ref[...])
    lf_ref[...] = lf[:, :N_HEADS]
    f_col = _dot_split_rhs(l_ref[...], lf, 3) + ccol_s[...]
    fcol_ref[...] = f_col[:, :N_HEADS]
    ccol_s[...] = f_col[tm - 1:tm, :]
    frow_ref[...] = f_col.T[:N_HEADS, :]
    lane = lax.broadcasted_iota(jnp.int32, f_col.shape, 1)
    f2 = f_col * LOG2E
    qbias_ref[...] = _split_bias_rows(f2 - c2_ref[...], lane)
    kbias_ref[...] = _split_bias_rows(-f2, lane)


def _logf_prompt(h, wt, f_row0, b_row, c2):
    m, d = h.shape
    tm = min(m, 1024)
    l_mat = _tri(tm, "row_ge_col")
    const = lambda i: (0, 0)
    return pl.pallas_call(
        _logf_body,
        grid=(m // tm,),
        in_specs=[pl.BlockSpec((tm, d), lambda i: (i, 0)),
                  pl.BlockSpec((None, N_HEADS, d), lambda i: (0, f_row0 // N_HEADS, 0)),
                  pl.BlockSpec((1, LANES), const),
                  pl.BlockSpec((tm, tm), const),
                  pl.BlockSpec((1, 1), const)],
        out_specs=(pl.BlockSpec((tm, N_HEADS), lambda i: (i, 0)),
                   pl.BlockSpec((tm, N_HEADS), lambda i: (i, 0)),
                   pl.BlockSpec((N_HEADS, tm), lambda i: (0, i)),
                   pl.BlockSpec((tm, LANES), lambda i: (i, 0)),
                   pl.BlockSpec((tm, LANES), lambda i: (i, 0))),
        out_shape=(jax.ShapeDtypeStruct((m, N_HEADS), F32),
                   jax.ShapeDtypeStruct((m, N_HEADS), F32),
                   jax.ShapeDtypeStruct((N_HEADS, m), F32),
                   jax.ShapeDtypeStruct((m, LANES), BF16),
                   jax.ShapeDtypeStruct((m, LANES), BF16)),
        scratch_shapes=[pltpu.VMEM((1, LANES), F32), pltpu.VMEM((d, LANES), BF16)],
        compiler_params=_cparams(("arbitrary",)),
        name="logf_prompt",
    )(h, wt, b_row, l_mat, c2)


def _logf_sample_body(h_ref, wft_ref, brow_ref, bu_ref, x_ref, ms_ref, lf_ref, c_ref, r_ref, *, kb):
    lf = _log_sigmoid(_dot(h_ref[...], _forget_weight(wft_ref)) + brow_ref[...])
    lf_ref[...] = lf[:, :N_HEADS]
    lft = lf.T[:2 * SUBLANES, :]
    c_ref[...] = _dot_split_lhs(lft, bu_ref[...], 3)[:N_HEADS, :]
    n_blocks = x_ref.shape[1] // kb
    carry = jnp.zeros((x_ref.shape[0], 1), F32)
    for blk in range(n_blocks - 1, -1, -1):
        x = x_ref[:, blk * kb:(blk + 1) * kb]
        cum = _dot_split_lhs(x, ms_ref[...], 3)
        r_ref[:, blk * kb:(blk + 1) * kb] = cum + carry
        carry = carry + cum[:, 0:1] + x[:, 0:1]


def _logf_sample(h_s, wt, f_row0, b_row, past_logf_t, dec_seq):
    n_rows, d = h_s.shape
    n_bh, p_len = past_logf_t.shape
    kb = min(p_len, 512)
    r = lax.broadcasted_iota(jnp.int32, (n_rows, n_rows), 0)
    c = lax.broadcasted_iota(jnp.int32, (n_rows, n_rows), 1)
    bu = ((r // dec_seq == c // dec_seq) & (r <= c)).astype(BF16)
    ms = _tri(kb, "row_gt_col")
    whole = lambda shape: pl.BlockSpec(shape, lambda i: (0,) * len(shape))
    return pl.pallas_call(
        functools.partial(_logf_sample_body, kb=kb),
        grid=(1,),
        in_specs=[whole((n_rows, d)),
                  pl.BlockSpec((None, N_HEADS, d), lambda i: (0, f_row0 // N_HEADS, 0)),
                  whole((1, LANES)), whole((n_rows, n_rows)), whole((n_bh, p_len)), whole((kb, kb))],
        out_specs=(whole((n_rows, N_HEADS)), whole((N_HEADS, n_rows)), whole((n_bh, p_len))),
        out_shape=(jax.ShapeDtypeStruct((n_rows, N_HEADS), F32),
                   jax.ShapeDtypeStruct((N_HEADS, n_rows), F32),
                   jax.ShapeDtypeStruct((n_bh, p_len), F32)),
        compiler_params=_cparams(("arbitrary",), vmem_mib=32),
        name="logf_sample",
    )(h_s, wt, b_row, bu, past_logf_t, ms)


SB_TQ = 256


def _sb_block(q, k, v, m_mat, carry, mask):
    z = _dot_nt(q, k)
    lsn = jnp.minimum(-z, 0.0) - jnp.log(1.0 + jnp.exp2(-jnp.abs(z))) * LOG2E
    lsp = lsn + z
    if mask is not None:
        lsn = jnp.where(mask, lsn, 0.0)
    cum = _dot_split_lhs(lsn, m_mat, 2)
    if carry is not None:
        cum = cum + carry
    w = jnp.exp2(lsp + cum)
    if mask is not None:
        w = jnp.where(mask, w, 0.0)
    o = _dot(w.astype(BF16), v)
    new_carry = cum[:, 0:1] + lsn[:, 0:1]
    return o, new_carry


def _sb_fast_body(q_ref, kd_ref, vd_ref, kp_ref, vp_ref, gs_ref, m_ref, a_ref, c_ref):
    i = pl.program_id(0)
    tq = q_ref.shape[0]
    has_prev = i > 0
    row = lax.broadcasted_iota(jnp.int32, (tq, tq), 0)
    col = lax.broadcasted_iota(jnp.int32, (tq, tq), 1)
    dmask = col < row
    pmask = jnp.logical_and(has_prev, col >= 0)
    lane = lax.broadcasted_iota(jnp.int32, (tq, LANES), 1)
    m_mat = m_ref[...]
    cacc = jnp.full((tq, LANES), NEG_BIG, F32)
    for hh in range(N_HEADS):
        sl = slice(hh * HEAD_DIM, (hh + 1) * HEAD_DIM)
        q = q_ref[:, sl]
        od, cd = _sb_block(q, kd_ref[:, sl], vd_ref[:, sl], m_mat, None, dmask)
        op, cp = _sb_block(q, kp_ref[:, sl], vp_ref[:, sl], m_mat, cd, pmask)
        a_ref[:, sl] = ((od + op) * gs_ref[:, sl].astype(F32)).astype(BF16)
        cacc = jnp.where(lane == hh, cp, cacc)
    c_ref[...] = cacc


def _sb_rest_body(q_ref, k_ref, v_ref, gs_ref, cin_ref, afast_ref, m_ref, a_ref, o_s, c_s):
    i = pl.program_id(0)
    s = pl.program_id(1)
    j = i - 2 - s
    tq = q_ref.shape[0]

    @pl.when(s == 0)
    def _():
        o_s[...] = jnp.zeros_like(o_s)
        c_s[...] = cin_ref[...]

    active = jnp.logical_and(j >= 0, jnp.max(c_s[...]) > SB_SKIP_LOG2)

    @pl.when(active)
    def _():
        lane = lax.broadcasted_iota(jnp.int32, (tq, LANES), 1)
        m_mat = m_ref[...]
        c_all = c_s[...]
        cacc = c_all
        for hh in range(N_HEADS):
            sl = slice(hh * HEAD_DIM, (hh + 1) * HEAD_DIM)
            carry = jnp.sum(jnp.where(lane == hh, c_all, 0.0), axis=1, keepdims=True)
            o, cn = _sb_block(q_ref[:, sl], k_ref[:, sl], v_ref[:, sl], m_mat, carry, None)
            o_s[:, sl] += o
            cacc = jnp.where(lane == hh, cn, cacc)
        c_s[...] = cacc

    @pl.when(s == pl.num_programs(1) - 1)
    def _():
        a_ref[...] = (afast_ref[...].astype(F32) + o_s[...] * gs_ref[...].astype(F32)).astype(BF16)


def _sb_prompt(q, k, v, gs):
    t, w = q.shape
    tq = min(SB_TQ, t)
    nq = t // tq
    m_mat = _tri(tq, "row_gt_col")
    blk = lambda f: pl.BlockSpec((tq, w), f)
    a_fast, carry = pl.pallas_call(
        _sb_fast_body,
        grid=(nq,),
        in_specs=[blk(lambda i: (i, 0)), blk(lambda i: (i, 0)), blk(lambda i: (i, 0)),
                  blk(lambda i: (jnp.maximum(i - 1, 0), 0)), blk(lambda i: (jnp.maximum(i - 1, 0), 0)),
                  blk(lambda i: (i, 0)),
                  pl.BlockSpec((tq, tq), lambda i: (0, 0))],
        out_specs=(blk(lambda i: (i, 0)), pl.BlockSpec((tq, LANES), lambda i: (i, 0))),
        out_shape=(jax.ShapeDtypeStruct((t, w), BF16), jax.ShapeDtypeStruct((t, LANES), F32)),
        compiler_params=_cparams(("parallel",), vmem_mib=48),
        name="sb_prompt_fast",
    )(q, k, v, k, v, gs, m_mat)
    if nq <= 2:
        return a_fast

    def rest(a_fast, carry):
        kidx = lambda i, s: (jnp.maximum(i - 2 - s, 0), 0)
        return pl.pallas_call(
            _sb_rest_body,
            grid=(nq, nq - 2),
            in_specs=[blk(lambda i, s: (i, 0)), blk(kidx), blk(kidx), blk(lambda i, s: (i, 0)),
                      pl.BlockSpec((tq, LANES), lambda i, s: (i, 0)),
                      blk(lambda i, s: (i, 0)),
                      pl.BlockSpec((tq, tq), lambda i, s: (0, 0))],
            out_specs=blk(lambda i, s: (i, 0)),
            out_shape=jax.ShapeDtypeStruct((t, w), BF16),
            scratch_shapes=[pltpu.VMEM((tq, w), F32), pltpu.VMEM((tq, LANES), F32)],
            compiler_params=_cparams(("parallel", "arbitrary"), vmem_mib=48),
            name="sb_prompt_rest",
        )(q, k, v, gs, carry, a_fast, m_mat)

    need_rest = jnp.max(carry[2 * tq:, :]) > SB_SKIP_LOG2
    return lax.cond(need_rest, rest, lambda a, c: a, a_fast, carry)


FOX_T = 512
FOX_WIN_TILES = (1, 2, 3, 4, 5, 6)
FOX_ZB_MAX = 40.0
FOX_SKIP_LOG = -104.0


def _head_column(blk, hh):
    lane8 = lax.broadcasted_iota(jnp.int32, blk.shape, 1)
    return jnp.sum(jnp.where(lane8 == hh, blk, 0.0), axis=1, keepdims=True)


BIAS_ONE_LANE = 3 * N_HEADS


def _split_bias_rows(f, lane):
    hi = f.astype(BF16).astype(F32)
    r1 = f - hi
    mid = r1.astype(BF16).astype(F32)
    low = (r1 - mid).astype(BF16).astype(F32)
    out = jnp.where(lane < N_HEADS, hi,
                    jnp.where(lane < 2 * N_HEADS, pltpu.roll(mid, N_HEADS, axis=1),
                              jnp.where(lane < BIAS_ONE_LANE, pltpu.roll(low, 2 * N_HEADS, axis=1),
                                        jnp.where(lane == BIAS_ONE_LANE, 1.0, 0.0))))
    return out.astype(BF16)


def _bias_selectors():
    src = lax.broadcasted_iota(jnp.int32, (N_HEADS, LANES, LANES), 1)
    dst = lax.broadcasted_iota(jnp.int32, (N_HEADS, LANES, LANES), 2)
    head = lax.broadcasted_iota(jnp.int32, (N_HEADS, LANES, LANES), 0)

    def sel(bias_base, one_base):
        term = dst - bias_base
        takes_bias = (term >= 0) & (term < 3) & (src == term * N_HEADS + head)
        takes_one = (dst >= one_base) & (dst < one_base + 3) & (src == BIAS_ONE_LANE)
        return (takes_bias | takes_one).astype(BF16)

    return jnp.stack([sel(0, 3), sel(3, 0)])


def _fox_fast_body(live_ref, q_ref, k_ref, v_ref, qb_ref, kb_ref, sel_ref, gs_ref, a_ref, kaug_s):
    hh = pl.program_id(0)
    qb = pl.program_id(1)
    nq = pl.num_programs(1)
    tb = q_ref.shape[0]
    t = k_ref.shape[0]

    @pl.when(qb == 0)
    def _():
        def build(c, carry):
            start = pl.multiple_of(c * tb, tb)
            kaug_s[pl.ds(start, tb), :] = _dot(kb_ref[pl.ds(start, tb), :], sel_ref[1]).astype(BF16)
            return carry
        lax.fori_loop(0, nq, build, 0)

    q_start = pl.multiple_of(qb * tb, tb)
    q_end = q_start + tb
    q2 = jnp.concatenate([q_ref[...], _dot(qb_ref[...], sel_ref[0]).astype(BF16)], axis=1)
    row = lax.broadcasted_iota(jnp.int32, (tb, 1), 0)
    live = live_ref[hh * nq + qb]

    def run(tiles):
        span = min(tiles * tb, t)
        lane_w = lax.broadcasted_iota(jnp.int32, (span, LANES), 1)
        ones_blk = jnp.where(lane_w == 0, 1.0, 0.0).astype(BF16)
        col = lax.broadcasted_iota(jnp.int32, (1, span), 1)

        def window(j, acc):
            upper = q_end - j * span
            start = pl.multiple_of(jnp.maximum(upper - span, 0), tb)
            k2 = jnp.concatenate([k_ref[pl.ds(start, span), :], kaug_s[pl.ds(start, span), :]], axis=1)
            p = jnp.exp2(_dot_nt(q2, k2))
            valid = jnp.logical_and(col - (q_start - start) <= row, col < upper - start)
            p = jnp.where(valid, p, 0.0).astype(BF16)
            v2 = jnp.concatenate([v_ref[pl.ds(start, span), :], ones_blk], axis=1)
            return acc + _dot(p, v2)

        n_win = (live + (tiles - 1)) // tiles
        acc = lax.fori_loop(0, n_win, window, jnp.zeros((tb, 2 * HEAD_DIM), F32))
        o = acc[:, :HEAD_DIM] / acc[:, HEAD_DIM:HEAD_DIM + 1]
        a_ref[...] = (o * gs_ref[...].astype(F32)).astype(BF16)

    bounds = (0,) + FOX_WIN_TILES
    for lo_t, hi_t in zip(bounds[:-1], bounds[1:]):
        last = hi_t == FOX_WIN_TILES[-1]
        cond = live > lo_t if last else jnp.logical_and(live > lo_t, live <= hi_t)
        pl.when(cond)(functools.partial(run, hi_t))


def _fox_slow_body(q_ref, k_ref, v_ref, fcol_ref, frow_ref, gs_ref, a_ref):
    hh = pl.program_id(0)
    qb = pl.program_id(1)
    tb = q_ref.shape[0]
    q = q_ref[...]
    fq = _head_column(fcol_ref[...], hh) * LOG2E

    def scores(kb):
        start = pl.multiple_of(kb * tb, tb)
        k = k_ref[pl.ds(start, tb), :]
        fk = frow_ref[0, :, pl.ds(start, tb)] * LOG2E
        return _dot_nt(q, k) + (fq - fk), v_ref[pl.ds(start, tb), :]

    def update(s, v, carry):
        m, l, acc = carry
        m_new = jnp.maximum(m, jnp.max(s, axis=1, keepdims=True))
        alpha = jnp.exp2(m - m_new)
        p = jnp.exp2(s - m_new)
        l = alpha * l + jnp.sum(p, axis=1, keepdims=True)
        acc = alpha * acc + _dot(p.astype(BF16), v)
        return m_new, l, acc

    def body(kb, carry):
        s, v = scores(kb)
        return update(s, v, carry)

    init = (jnp.full((tb, 1), NEG_BIG, F32), jnp.zeros((tb, 1), F32), jnp.zeros((tb, HEAD_DIM), F32))
    carry = lax.fori_loop(0, qb, body, init)
    s, v = scores(qb)
    row = lax.broadcasted_iota(jnp.int32, (tb, tb), 0)
    col = lax.broadcasted_iota(jnp.int32, (tb, tb), 1)
    s = jnp.where(col <= row, s, NEG_BIG)
    _, l, acc = update(s, v, carry)
    a_ref[...] = ((acc / l) * gs_ref[...].astype(F32)).astype(BF16)


def _fox_prompt(q, k, v, f_col, f_row, gs, zb, q_bias, k_bias):
    t, w = q.shape
    tb = min(FOX_T, t)
    nq = t // tb
    out_shape = jax.ShapeDtypeStruct((t, w), BF16)

    def fast(q, k, v, f_col, f_row, gs, q_bias, k_bias):
        f_start = f_row[:, ::tb]
        f_end = f_row[:, tb - 1::tb]
        dead = (f_start[:, :, None] - f_end[:, None, :]) < FOX_SKIP_LOG
        lo = jnp.sum(dead, axis=2).astype(jnp.int32)
        live = (jnp.arange(1, nq + 1, dtype=jnp.int32)[None, :] - lo).reshape(-1)
        head_blk = lambda h, i, live: (i, h)
        head_all = lambda h, i, live: (0, h)
        grid_spec = pltpu.PrefetchScalarGridSpec(
            num_scalar_prefetch=1,
            grid=(N_HEADS, nq),
            in_specs=[pl.BlockSpec((tb, HEAD_DIM), head_blk),
                      pl.BlockSpec((t, HEAD_DIM), head_all),
                      pl.BlockSpec((t, HEAD_DIM), head_all),
                      pl.BlockSpec((tb, LANES), lambda h, i, live: (i, 0)),
                      pl.BlockSpec((t, LANES), lambda h, i, live: (0, 0), pipeline_mode=pl.Buffered(1)),
                      pl.BlockSpec((2, None, LANES, LANES), lambda h, i, live: (0, h, 0, 0)),
                      pl.BlockSpec((tb, HEAD_DIM), head_blk)],
            out_specs=pl.BlockSpec((tb, HEAD_DIM), head_blk),
            scratch_shapes=[pltpu.VMEM((t, LANES), BF16)])
        return pl.pallas_call(
            _fox_fast_body, grid_spec=grid_spec, out_shape=out_shape,
            compiler_params=_cparams(("parallel", "arbitrary"), vmem_mib=48),
            name="fox_prompt_fast",
        )(live, q, k, v, q_bias, k_bias, _bias_selectors(), gs)

    def slow(q, k, v, f_col, f_row, gs, q_bias, k_bias):
        f_row3 = f_row.reshape(N_HEADS, 1, t)
        return pl.pallas_call(
            _fox_slow_body,
            grid=(N_HEADS, nq),
            in_specs=[pl.BlockSpec((tb, HEAD_DIM), lambda h, i: (i, h)),
                      pl.BlockSpec((t, HEAD_DIM), lambda h, i: (0, h)),
                      pl.BlockSpec((t, HEAD_DIM), lambda h, i: (0, h)),
                      pl.BlockSpec((tb, N_HEADS), lambda h, i: (i, 0)),
                      pl.BlockSpec((1, 1, t), lambda h, i: (h, 0, 0)),
                      pl.BlockSpec((tb, HEAD_DIM), lambda h, i: (i, h))],
            out_specs=pl.BlockSpec((tb, HEAD_DIM), lambda h, i: (i, h)),
            out_shape=out_shape,
            compiler_params=_cparams(("parallel", "parallel"), vmem_mib=48),
            name="fox_prompt_slow",
        )(q, k, v, f_col, f_row3, gs)

    return lax.cond(zb <= FOX_ZB_MAX, fast, slow, q, k, v, f_col, f_row, gs, q_bias, k_bias)


DEC_KB = 2048
SB_DEC_WIN = 256


def _own_head_mask():
    row = lax.broadcasted_iota(jnp.int32, (SUBLANES, W_ATT), 0)
    col = lax.broadcasted_iota(jnp.int32, (SUBLANES, W_ATT), 1)
    return (col // HEAD_DIM) == row


def _expand_q(q_ref, qx_s):
    n_tok = q_ref.shape[0]
    qf = q_ref[...].astype(F32)
    own = _own_head_mask()
    for tkn in range(n_tok):
        rep = jnp.broadcast_to(qf[tkn:tkn + 1, :], (SUBLANES, W_ATT))
        qx_s[tkn * N_HEADS:(tkn + 1) * N_HEADS, :] = jnp.where(own, rep, 0.0)


def _pad_new(kn_ref, vn_ref, knp_s, vnp_s):
    n_tok = kn_ref.shape[0]
    knp_s[...] = jnp.zeros_like(knp_s)
    vnp_s[...] = jnp.zeros_like(vnp_s)
    knp_s[0:n_tok, :] = kn_ref[...]
    vnp_s[0:n_tok, :] = vn_ref[...]


def _gather_heads(c_ref, dst_s):
    n_keys = dst_s.shape[0]
    for hh in range(N_HEADS):
        dst_s[:, hh * HEAD_DIM:(hh + 1) * HEAD_DIM] = (
            c_ref[0, pl.ds(hh, n_keys, stride=N_HEADS), :].astype(BF16))


def _collapse_heads(o_full, out_s):
    n_tok = o_full.shape[0] // N_HEADS
    own = _own_head_mask()
    for tkn in range(n_tok):
        blk = jnp.where(own, o_full[tkn * N_HEADS:(tkn + 1) * N_HEADS, :], 0.0)
        out_s[tkn:tkn + 1, :] = jnp.sum(blk, axis=0, keepdims=True)


def _sb_dec_fast_body(q_ref, kn_ref, vn_ref, kc_ref, vc_ref, gs_ref, mn_ref, mc_ref, a_ref, c_ref,
                      qx_s, knp_s, vnp_s, kx_s, vx_s, out_s):
    n_rows = qx_s.shape[0]
    _expand_q(q_ref, qx_s)
    _pad_new(kn_ref, vn_ref, knp_s, vnp_s)
    qx = qx_s[...].astype(BF16)
    row = lax.broadcasted_iota(jnp.int32, (n_rows, LANES), 0)
    col = lax.broadcasted_iota(jnp.int32, (n_rows, LANES), 1)
    mask = col < row // N_HEADS
    o_new, carry = _sb_block(qx, knp_s[...], vnp_s[...], mn_ref[...], None, mask)
    _gather_heads(kc_ref, kx_s)
    _gather_heads(vc_ref, vx_s)
    o_win, carry = _sb_block(qx, kx_s[...], vx_s[...], mc_ref[...], carry, None)
    _collapse_heads(o_new + o_win, out_s)
    a_ref[...] = (out_s[...] * gs_ref[...].astype(F32)).astype(BF16)
    c_ref[...] = jnp.broadcast_to(carry, c_ref.shape)


def _sb_dec_rest_body(q_ref, kc_ref, vc_ref, gs_ref, cin_ref, afast_ref, mc_ref, a_ref,
                      qx_s, kx_s, vx_s, acc_s, c_s, out_s):
    s = pl.program_id(1)

    @pl.when(s == 0)
    def _():
        _expand_q(q_ref, qx_s)
        acc_s[...] = jnp.zeros_like(acc_s)
        c_s[...] = cin_ref[:, 0:1]

    @pl.when(jnp.max(c_s[...]) > SB_SKIP_LOG2)
    def _():
        _gather_heads(kc_ref, kx_s)
        _gather_heads(vc_ref, vx_s)
        o, carry = _sb_block(qx_s[...].astype(BF16), kx_s[...], vx_s[...], mc_ref[...], c_s[...], None)
        acc_s[...] += o
        c_s[...] = carry

    @pl.when(s == pl.num_programs(1) - 1)
    def _():
        _collapse_heads(acc_s[...], out_s)
        a_ref[...] = (afast_ref[...].astype(F32) + out_s[...] * gs_ref[...].astype(F32)).astype(BF16)


def _sb_decode(q, k_new, v_new, k_cache, v_cache, gs, dec_seq):
    n_rows_all, w = q.shape
    n_b = n_rows_all // dec_seq
    p_len = k_cache.shape[1] // N_HEADS
    win = min(SB_DEC_WIN, p_len)
    n_win = p_len // win
    n_rows = dec_seq * N_HEADS
    m_new = _tri(LANES, "row_gt_col")
    m_win = _tri(win, "row_gt_col")
    tok = lambda b: (b, 0)
    a_fast, carry = pl.pallas_call(
        _sb_dec_fast_body,
        grid=(n_b,),
        in_specs=[pl.BlockSpec((dec_seq, w), tok), pl.BlockSpec((dec_seq, w), tok),
                  pl.BlockSpec((dec_seq, w), tok),
                  pl.BlockSpec((1, win * N_HEADS, HEAD_DIM), lambda b: (b, n_win - 1, 0)),
                  pl.BlockSpec((1, win * N_HEADS, HEAD_DIM), lambda b: (b, n_win - 1, 0)),
                  pl.BlockSpec((dec_seq, w), tok),
                  pl.BlockSpec((LANES, LANES), lambda b: (0, 0)),
                  pl.BlockSpec((win, win), lambda b: (0, 0))],
        out_specs=(pl.BlockSpec((dec_seq, w), tok), pl.BlockSpec((n_rows, LANES), tok)),
        out_shape=(jax.ShapeDtypeStruct((n_rows_all, w), BF16),
                   jax.ShapeDtypeStruct((n_b * n_rows, LANES), F32)),
        scratch_shapes=[pltpu.VMEM((n_rows, w), F32), pltpu.VMEM((LANES, w), BF16),
                        pltpu.VMEM((LANES, w), BF16), pltpu.VMEM((win, w), BF16),
                        pltpu.VMEM((win, w), BF16), pltpu.VMEM((dec_seq, w), F32)],
        compiler_params=_cparams(("parallel",), vmem_mib=48),
        name="sb_decode_fast",
    )(q, k_new, v_new, k_cache, v_cache, gs, m_new, m_win)
    if n_win <= 1:
        return a_fast

    def rest(a_fast, carry):
        tok2 = lambda b, s: (b, 0)
        older = lambda b, s: (b, n_win - 2 - s, 0)
        return pl.pallas_call(
            _sb_dec_rest_body,
            grid=(n_b, n_win - 1),
            in_specs=[pl.BlockSpec((dec_seq, w), tok2),
                      pl.BlockSpec((1, win * N_HEADS, HEAD_DIM), older),
                      pl.BlockSpec((1, win * N_HEADS, HEAD_DIM), older),
                      pl.BlockSpec((dec_seq, w), tok2),
                      pl.BlockSpec((n_rows, LANES), tok2),
                      pl.BlockSpec((dec_seq, w), tok2),
                      pl.BlockSpec((win, win), lambda b, s: (0, 0))],
            out_specs=pl.BlockSpec((dec_seq, w), tok2),
            out_shape=jax.ShapeDtypeStruct((n_rows_all, w), BF16),
            scratch_shapes=[pltpu.VMEM((n_rows, w), F32), pltpu.VMEM((win, w), BF16),
                            pltpu.VMEM((win, w), BF16), pltpu.VMEM((n_rows, w), F32),
                            pltpu.VMEM((n_rows, 1), F32), pltpu.VMEM((dec_seq, w), F32)],
            compiler_params=_cparams(("parallel", "arbitrary"), vmem_mib=48),
            name="sb_decode_rest",
        )(q, k_cache, v_cache, gs, carry, a_fast, m_win)

    need_rest = jnp.max(carry) > SB_SKIP_LOG2
    return lax.cond(need_rest, rest, lambda a, c: a, a_fast, carry)


def _fox_dec_body(q_ref, kn_ref, vn_ref, kc_ref, vc_ref, gs_ref, cn_ref, r_ref, a_ref,
                  qx_s, knp_s, vnp_s, kx_s, vx_s, acc_s, m_s, l_s, fq_s, out_s):
    s = pl.program_id(1)
    n_rows = qx_s.shape[0]
    n_tok = n_rows // N_HEADS

    def update(sc, v):
        m = m_s[...]
        m_new = jnp.maximum(m, jnp.max(sc, axis=1, keepdims=True))
        alpha = jnp.exp2(m - m_new)
        p = jnp.exp2(sc - m_new)
        l_s[...] = alpha * l_s[...] + jnp.sum(p, axis=1, keepdims=True)
        acc_s[...] = alpha * acc_s[...] + _dot(p.astype(BF16), v)
        m_s[...] = m_new

    @pl.when(s == 0)
    def _():
        _expand_q(q_ref, qx_s)
        _pad_new(kn_ref, vn_ref, knp_s, vnp_s)
        m_s[...] = jnp.full_like(m_s, NEG_BIG)
        l_s[...] = jnp.zeros_like(l_s)
        acc_s[...] = jnp.zeros_like(acc_s)
        row = lax.broadcasted_iota(jnp.int32, (n_rows, LANES), 0)
        col = lax.broadcasted_iota(jnp.int32, (n_rows, LANES), 1)
        cn = jnp.concatenate([cn_ref[0]] * n_tok, axis=0) * LOG2E
        fq = jnp.sum(jnp.where(col == row // N_HEADS, cn, 0.0), axis=1, keepdims=True)
        fq_s[...] = fq
        sc = _dot_nt(qx_s[...].astype(BF16), knp_s[...]) + (fq - cn)
        sc = jnp.where(col <= row // N_HEADS, sc, NEG_BIG)
        update(sc, vnp_s[...])

    _gather_heads(kc_ref, kx_s)
    _gather_heads(vc_ref, vx_s)
    r = jnp.concatenate([r_ref[...]] * n_tok, axis=0) * LOG2E
    update(_dot_nt(qx_s[...].astype(BF16), kx_s[...]) + (fq_s[...] + r), vx_s[...])

    @pl.when(s == pl.num_programs(1) - 1)
    def _():
        _collapse_heads(acc_s[...] / l_s[...], out_s)
        a_ref[...] = (out_s[...] * gs_ref[...].astype(F32)).astype(BF16)


def _fox_decode(q, k_new, v_new, k_cache, v_cache, gs, c_new, r_past, dec_seq):
    n_rows_all, w = q.shape
    n_b = n_rows_all // dec_seq
    p_len = k_cache.shape[1] // N_HEADS
    kb = min(DEC_KB, p_len)
    nkb = p_len // kb
    n_rows = dec_seq * N_HEADS
    tok = lambda b, s: (b, 0)
    cache = lambda b, s: (b, s, 0)
    return pl.pallas_call(
        _fox_dec_body,
        grid=(n_b, nkb),
        in_specs=[pl.BlockSpec((dec_seq, w), tok), pl.BlockSpec((dec_seq, w), tok),
                  pl.BlockSpec((dec_seq, w), tok),
                  pl.BlockSpec((1, kb * N_HEADS, HEAD_DIM), cache),
                  pl.BlockSpec((1, kb * N_HEADS, HEAD_DIM), cache),
                  pl.BlockSpec((dec_seq, w), tok),
                  pl.BlockSpec((1, N_HEADS, LANES), lambda b, s: (b, 0, 0)),
                  pl.BlockSpec((N_HEADS, kb), lambda b, s: (b, s))],
        out_specs=pl.BlockSpec((dec_seq, w), tok),
        out_shape=jax.ShapeDtypeStruct((n_rows_all, w), BF16),
        scratch_shapes=[pltpu.VMEM((n_rows, w), F32), pltpu.VMEM((LANES, w), BF16),
                        pltpu.VMEM((LANES, w), BF16), pltpu.VMEM((kb, w), BF16),
                        pltpu.VMEM((kb, w), BF16), pltpu.VMEM((n_rows, w), F32),
                        pltpu.VMEM((n_rows, 1), F32), pltpu.VMEM((n_rows, 1), F32),
                        pltpu.VMEM((n_rows, 1), F32), pltpu.VMEM((dec_seq, w), F32)],
        compiler_params=_cparams(("parallel", "arbitrary"), vmem_mib=56),
        name="fox_decode",
    )(q, k_new, v_new, k_cache, v_cache, gs, c_new, r_past)


def _post_body(x_ref, asb_ref, afx_ref, msb_ref, mfx_ref, wsb_ref, wfx_ref, wo_ref, y_ref):
    u_sb = _dot(asb_ref[...], wsb_ref[...])
    u_fx = _dot(afx_ref[...], wfx_ref[...])
    merged = msb_ref[...].astype(F32) * u_sb + mfx_ref[...].astype(F32) * u_fx
    y_ref[...] = x_ref[...] + _dot(merged.astype(BF16), wo_ref[...])


def _post(x2d, a_sb, a_fx, m_sig, w_sb, w_fx, w_o):
    m, d = x2d.shape
    w = a_sb.shape[1]
    tm = min(m, 512)
    row = lambda i: (i, 0)
    const = lambda i: (0, 0)
    resident = functools.partial(pl.BlockSpec, index_map=const, pipeline_mode=pl.Buffered(1))
    return pl.pallas_call(
        _post_body,
        grid=(m // tm,),
        in_specs=[pl.BlockSpec((tm, d), row), pl.BlockSpec((tm, w), row), pl.BlockSpec((tm, w), row),
                  pl.BlockSpec((tm, d), lambda i: (i, 0)), pl.BlockSpec((tm, d), lambda i: (i, 1)),
                  resident((w, d)), resident((w, d)), resident((d, d))],
        out_specs=pl.BlockSpec((tm, d), row),
        out_shape=jax.ShapeDtypeStruct((m, d), F32),
        compiler_params=_cparams(("parallel",), vmem_mib=56),
        name="post",
    )(x2d, a_sb, a_fx, m_sig, m_sig, w_sb, w_fx, w_o)


def _project_all(hp, hs, w, qnw, knw):
    seg = lambda c: c * W_ATT
    q_sb = _proj(hp, hs, w, seg(0), "scale")
    k_sb = _proj(hp, hs, w, seg(1), "kv")
    v_sb = _proj(hp, hs, w, seg(2), "kv")
    g_sb = _proj(hp, hs, w, seg(3), "silu")
    q_fx = _proj(hp, hs, w, seg(4), "qnorm", nw=qnw)
    k_fx = _proj(hp, hs, w, seg(5), "knorm", nw=knw)
    v_fx = _proj(hp, hs, w, seg(6), "kv")
    g_fx = _proj(hp, hs, w, seg(7), "silu")
    m_sig = _proj(hp, hs, w, seg(8) + N_HEADS, "sigmoid", n_tiles=2 * D_MODEL // W_ATT)
    calls = (q_sb, k_sb, v_sb, g_sb, q_fx, k_fx, v_fx, g_fx, m_sig)
    prompt = tuple(o for r in calls for o in r[:len(r) // 2])
    decode = tuple(o for r in calls for o in r[len(r) // 2:])
    return prompt, decode


def kernel(x_prompt, x_sample, cache_sb_k, cache_sb_v, cache_fox_k, cache_fox_v, cache_fox_logf,
           norm_w, w_in, b_forget, q_norm_w, k_norm_w, w_branch_sb, w_branch_fox, w_out):
    depth = norm_w.shape[0]
    assert depth == 1, "single-layer step"
    bsz, seq, d = x_prompt.shape
    assert bsz == 1
    n_dec, dec_seq, _ = x_sample.shape
    p_len = cache_sb_k.shape[2]
    n_main = 8 * W_ATT

    w_t = jnp.swapaxes(w_in, 1, 2)
    b_row = jnp.pad(b_forget[0].astype(F32)[None, :], ((0, 0), (0, LANES - N_HEADS)))
    qnw = jnp.tile(q_norm_w[0].astype(F32), N_HEADS)[None, :]
    knw = jnp.tile(k_norm_w[0].astype(F32), N_HEADS)[None, :]
    nw_row = norm_w[0].astype(F32)[None, :]
    w_sb = w_branch_sb[0].astype(BF16)
    w_fx = w_branch_fox[0].astype(BF16)
    w_o = w_out[0].astype(BF16)

    xp = x_prompt.reshape(seq, d)
    xs = x_sample.reshape(n_dec * dec_seq, d)
    hp = _rmsnorm(xp, nw_row)
    hs = _rmsnorm(xs, nw_row)
    prompt_proj, decode_proj = _project_all(hp, hs, w_t, qnw, knw)

    (q_sb, k_sb, k_sb_b, v_sb, v_sb_b, g_sb, q_fx, k_fx, k_fx_b, v_fx, v_fx_b, g_fx,
     m_sig) = prompt_proj
    zb = (1.02 * HEAD_DIM * QK_SCALE) * jnp.max(jnp.abs(q_norm_w[0])) * jnp.max(jnp.abs(k_norm_w[0]))
    zb = zb.astype(F32)
    logf, f_col, f_row, q_bias, k_bias = _logf_prompt(hp, w_t, n_main, b_row, (zb * LOG2E).reshape(1, 1))
    a_sb = _sb_prompt(q_sb, k_sb_b, v_sb_b, g_sb)
    a_fx = _fox_prompt(q_fx, k_fx_b, v_fx_b, f_col, f_row, g_fx, zb, q_bias, k_bias)
    y_prompt = _post(xp, a_sb, a_fx, m_sig, w_sb, w_fx, w_o).reshape(bsz, seq, d)

    (sq_sb, sk_sb, sk_sb_b, sv_sb, sv_sb_b, sg_sb, sq_fx, sk_fx, sk_fx_b, sv_fx, sv_fx_b, sg_fx,
     sm_sig) = decode_proj
    past_logf_t = jnp.transpose(cache_fox_logf[0].astype(F32), (0, 2, 1)).reshape(n_dec * N_HEADS, p_len)
    s_logf, c_all, r_past = _logf_sample(hs, w_t, n_main, b_row, past_logf_t, dec_seq)
    c_new = jnp.transpose(c_all.reshape(N_HEADS, n_dec, dec_seq), (1, 0, 2))
    c_new = jnp.pad(c_new, ((0, 0), (0, 0), (0, LANES - dec_seq)))
    kc_sb = cache_sb_k[0].reshape(n_dec, p_len * N_HEADS, HEAD_DIM)
    vc_sb = cache_sb_v[0].reshape(n_dec, p_len * N_HEADS, HEAD_DIM)
    kc_fx = cache_fox_k[0].reshape(n_dec, p_len * N_HEADS, HEAD_DIM)
    vc_fx = cache_fox_v[0].reshape(n_dec, p_len * N_HEADS, HEAD_DIM)
    sa_sb = _sb_decode(sq_sb, sk_sb_b, sv_sb_b, kc_sb, vc_sb, sg_sb, dec_seq)
    sa_fx = _fox_decode(sq_fx, sk_fx_b, sv_fx_b, kc_fx, vc_fx, sg_fx, c_new, r_past, dec_seq)
    y_sample = _post(xs, sa_sb, sa_fx, sm_sig, w_sb, w_fx, w_o).reshape(n_dec, dec_seq, d)

    hd = (N_HEADS, HEAD_DIM)
    return (y_prompt, y_sample,
            k_sb.reshape(1, bsz, seq, *hd), v_sb.reshape(1, bsz, seq, *hd),
            k_fx.reshape(1, bsz, seq, *hd), v_fx.reshape(1, bsz, seq, *hd),
            logf.reshape(1, bsz, seq, N_HEADS),
            sk_sb.reshape(1, n_dec, dec_seq, *hd), sv_sb.reshape(1, n_dec, dec_seq, *hd),
            sk_fx.reshape(1, n_dec, dec_seq, *hd), sv_fx.reshape(1, n_dec, dec_seq, *hd),
            s_logf.reshape(1, n_dec, dec_seq, N_HEADS))
```

```python
import functools

import jax
import jax.numpy as jnp
from jax import lax
from jax.experimental import pallas as pl
from jax.experimental.pallas import tpu as pltpu

F32 = jnp.float32
BF16 = jnp.bfloat16

D_MODEL = 2048
N_HEADS = 8
HEAD_DIM = 128
W_ATT = N_HEADS * HEAD_DIM
RMS_EPS = 1e-6
QK_SCALE = HEAD_DIM ** -0.5
LOG2E = 1.4426950408889634
LANES = 128
SUBLANES = 8
SB_SKIP_LOG2 = -150.0
NEG_BIG = -1e30
MIB = 1024 * 1024


def _cparams(sem, vmem_mib=None):
    kw = dict(dimension_semantics=sem)
    if vmem_mib is not None:
        kw["vmem_limit_bytes"] = vmem_mib * MIB
    return pltpu.CompilerParams(**kw)


def _softplus_neg_abs(z):
    return jnp.log1p(jnp.exp(-jnp.abs(z)))


def _log_sigmoid(z):
    return jnp.minimum(z, 0.0) - _softplus_neg_abs(z)


def _split_bf16(x, n):
    parts = []
    r = x
    for _ in range(n - 1):
        p = r.astype(BF16)
        parts.append(p)
        r = r - p.astype(F32)
    parts.append(r.astype(BF16))
    return parts


def _dot(a, b):
    return jnp.dot(a, b, preferred_element_type=F32)


def _dot_nt(a, b):
    return lax.dot_general(a, b, (((1,), (1,)), ((), ())), preferred_element_type=F32)


def _dot_split_lhs(x, m, n):
    acc = None
    for p in _split_bf16(x, n):
        t = _dot(p, m)
        acc = t if acc is None else acc + t
    return acc


def _dot_split_rhs(m, x, n):
    acc = None
    for p in _split_bf16(x, n):
        t = _dot(m, p)
        acc = t if acc is None else acc + t
    return acc


def _tri(n, kind):
    r = lax.broadcasted_iota(jnp.int32, (n, n), 0)
    c = lax.broadcasted_iota(jnp.int32, (n, n), 1)
    if kind == "row_gt_col":
        m = r > c
    elif kind == "row_le_col":
        m = r <= c
    elif kind == "row_ge_col":
        m = r >= c
    else:
        raise ValueError(kind)
    return m.astype(BF16)


def _rmsnorm_body(x_ref, w_ref, o_ref):
    x = x_ref[...]
    ms = jnp.mean(x * x, axis=-1, keepdims=True)
    o_ref[...] = (x * lax.rsqrt(ms + RMS_EPS) * w_ref[...]).astype(o_ref.dtype)


def _rmsnorm(x2d, w_row):
    m, d = x2d.shape
    tm = min(m, 512)
    return pl.pallas_call(
        _rmsnorm_body,
        grid=(m // tm,),
        in_specs=[pl.BlockSpec((tm, d), lambda i: (i, 0)),
                  pl.BlockSpec((1, d), lambda i: (0, 0))],
        out_specs=pl.BlockSpec((tm, d), lambda i: (i, 0)),
        out_shape=jax.ShapeDtypeStruct((m, d), BF16),
        compiler_params=_cparams(("parallel",)),
        name="rmsnorm",
    )(x2d, w_row)


def _head_rmsnorm(acc, nw):
    parts = []
    for hh in range(N_HEADS):
        a = acc[:, hh * HEAD_DIM:(hh + 1) * HEAD_DIM]
        ms = jnp.mean(a * a, axis=-1, keepdims=True)
        parts.append(a * lax.rsqrt(ms + RMS_EPS))
    return jnp.concatenate(parts, axis=1) * nw


def _proj_epilogue(acc, kind, nw_ref, outs):
    if kind == "scale":
        outs[0][...] = (acc * (QK_SCALE * LOG2E)).astype(BF16)
    elif kind == "kv":
        outs[0][...] = acc
        outs[1][...] = acc.astype(BF16)
    elif kind == "silu":
        outs[0][...] = (acc * jax.nn.sigmoid(acc)).astype(BF16)
    elif kind == "sigmoid":
        outs[0][...] = jax.nn.sigmoid(acc).astype(BF16)
    elif kind == "qnorm":
        outs[0][...] = (_head_rmsnorm(acc, nw_ref[...]) * (QK_SCALE * LOG2E)).astype(BF16)
    elif kind == "knorm":
        y = _head_rmsnorm(acc, nw_ref[...])
        outs[0][...] = y
        outs[1][...] = y.astype(BF16)
    else:
        raise ValueError(kind)


def _proj_body(*refs, kind, n_out, shift):
    hp_ref, hs_ref, w_ref = refs[:3]
    pos = 3
    wn_ref = nw_ref = None
    if shift:
        wn_ref = refs[pos]
        pos += 1
    if kind in ("qnorm", "knorm"):
        nw_ref = refs[pos]
        pos += 1
    outs_p = refs[pos:pos + n_out]
    outs_s = refs[pos + n_out:pos + 2 * n_out]
    wb_s = refs[-1]
    i = pl.program_id(1)
    n_prompt = pl.num_programs(1) - 1

    @pl.when(i == 0)
    def _():
        if shift:
            wt = jnp.concatenate([w_ref[shift:, :], wn_ref[...]], axis=0)
        else:
            wt = w_ref[...]
        wb_s[...] = wt.T.astype(BF16)

    @pl.when(i < n_prompt)
    def _():
        _proj_epilogue(_dot(hp_ref[...], wb_s[...]), kind, nw_ref, outs_p)

    @pl.when(i == n_prompt)
    def _():
        _proj_epilogue(_dot(hs_ref[...], wb_s[...]), kind, nw_ref, outs_s)


def _proj(hp, hs, wt, col0, kind, nw=None, n_tiles=1):
    mp, d = hp.shape
    ms = hs.shape[0]
    tn = W_ATT
    dtypes = (F32, BF16) if kind in ("kv", "knorm") else (BF16,)
    tm = min(mp, 1024)
    n_prompt = mp // tm
    blk0 = col0 // tn
    shift = col0 - blk0 * tn
    assert shift in (0, SUBLANES)
    grid = (n_tiles, n_prompt + 1)
    prow = lambda j, i: (jnp.minimum(i, n_prompt - 1), 0)
    w_mode = {} if n_tiles > 1 else dict(pipeline_mode=pl.Buffered(1))
    in_specs = [pl.BlockSpec((tm, d), prow),
                pl.BlockSpec((ms, d), lambda j, i: (0, 0)),
                pl.BlockSpec((None, tn, d), lambda j, i: (0, blk0 + j, 0), **w_mode)]
    args = [hp, hs, wt]
    if shift:
        per = tn // shift
        in_specs.append(pl.BlockSpec((None, shift, d), lambda j, i: (0, (blk0 + j + 1) * per, 0)))
        args.append(wt)
    if kind in ("qnorm", "knorm"):
        in_specs.append(pl.BlockSpec((1, tn), lambda j, i: (0, 0)))
        args.append(nw)
    n_cols = tn * n_tiles
    p_block = pl.BlockSpec((tm, tn), lambda j, i: (jnp.minimum(i, n_prompt - 1), j))
    s_block = pl.BlockSpec((ms, tn), lambda j, i: (0, j))
    out_shape = tuple(jax.ShapeDtypeStruct((mp, n_cols), t) for t in dtypes) + \
        tuple(jax.ShapeDtypeStruct((ms, n_cols), t) for t in dtypes)
    out_specs = (p_block,) * len(dtypes) + (s_block,) * len(dtypes)
    return pl.pallas_call(
        functools.partial(_proj_body, kind=kind, n_out=len(dtypes), shift=shift),
        grid=grid,
        in_specs=in_specs,
        out_specs=out_specs,
        out_shape=out_shape,
        scratch_shapes=[pltpu.VMEM((d, tn), BF16)],
        compiler_params=_cparams(("parallel", "arbitrary"), vmem_mib=56),
        name="proj_" + kind,
    )(*args)


def _forget_weight(wft_ref):
    wft = wft_ref[...]
    pad = jnp.zeros((LANES - wft.shape[0], wft.shape[1]), F32)
    return jnp.concatenate([wft, pad], axis=0).T.astype(BF16)


def _logf_body(h_ref, wft_ref, brow_ref, l_ref, c2_ref, lf_ref, fcol_ref, frow_ref, qbias_ref, kbias_ref,
               ccol_s, wf_s):
    i = pl.program_id(0)
    tm = h_ref.shape[0]

    @pl.when(i == 0)
    def _():
        ccol_s[...] = jnp.zeros_like(ccol_s)
        wf_s[...] = _forget_weight(wft_ref)

    lf = _log_sigmoid(_dot(h_ref[...], wf_s[...]) + brow_ref[...])
    lf_ref[...] = lf[:, :N_HEADS]
    f_col = _dot_split_rhs(l_ref[...], lf, 3) + ccol_s[...]
    fcol_ref[...] = f_col[:, :N_HEADS]
    ccol_s[...] = f_col[tm - 1:tm, :]
    frow_ref[...] = f_col.T[:N_HEADS, :]
    lane = lax.broadcasted_iota(jnp.int32, f_col.shape, 1)
    f2 = f_col * LOG2E
    qbias_ref[...] = _split_bias_rows(f2 - c2_ref[...], lane)
    kbias_ref[...] = _split_bias_rows(-f2, lane)


def _logf_prompt(h, wt, f_row0, b_row, c2):
    m, d = h.shape
    tm = min(m, 512)
    l_mat = _tri(tm, "row_ge_col")
    const = lambda i: (0, 0)
    return pl.pallas_call(
        _logf_body,
        grid=(m // tm,),
        in_specs=[pl.BlockSpec((tm, d), lambda i: (i, 0)),
                  pl.BlockSpec((None, N_HEADS, d), lambda i: (0, f_row0 // N_HEADS, 0)),
                  pl.BlockSpec((1, LANES), const),
                  pl.BlockSpec((tm, tm), const),
                  pl.BlockSpec((1, 1), const)],
        out_specs=(pl.BlockSpec((tm, N_HEADS), lambda i: (i, 0)),
                   pl.BlockSpec((tm, N_HEADS), lambda i: (i, 0)),
                   pl.BlockSpec((N_HEADS, tm), lambda i: (0, i)),
                   pl.BlockSpec((tm, LANES), lambda i: (i, 0)),
                   pl.BlockSpec((tm, LANES), lambda i: (i, 0))),
        out_shape=(jax.ShapeDtypeStruct((m, N_HEADS), F32),
                   jax.ShapeDtypeStruct((m, N_HEADS), F32),
                   jax.ShapeDtypeStruct((N_HEADS, m), F32),
                   jax.ShapeDtypeStruct((m, LANES), BF16),
                   jax.ShapeDtypeStruct((m, LANES), BF16)),
        scratch_shapes=[pltpu.VMEM((1, LANES), F32), pltpu.VMEM((d, LANES), BF16)],
        compiler_params=_cparams(("arbitrary",)),
        name="logf_prompt",
    )(h, wt, b_row, l_mat, c2)


def _logf_sample_body(h_ref, wft_ref, brow_ref, bu_ref, x_ref, ms_ref, lf_ref, c_ref, r_ref, *, kb):
    lf = _log_sigmoid(_dot(h_ref[...], _forget_weight(wft_ref)) + brow_ref[...])
    lf_ref[...] = lf[:, :N_HEADS]
    lft = lf.T[:2 * SUBLANES, :]
    c_ref[...] = _dot_split_lhs(lft, bu_ref[...], 3)[:N_HEADS, :]
    n_blocks = x_ref.shape[1] // kb
    carry = jnp.zeros((x_ref.shape[0], 1), F32)
    for blk in range(n_blocks - 1, -1, -1):
        x = x_ref[:, blk * kb:(blk + 1) * kb]
        cum = _dot_split_lhs(x, ms_ref[...], 3)
        r_ref[:, blk * kb:(blk + 1) * kb] = cum + carry
        carry = carry + cum[:, 0:1] + x[:, 0:1]


def _logf_sample(h_s, wt, f_row0, b_row, past_logf_t, dec_seq):
    n_rows, d = h_s.shape
    n_bh, p_len = past_logf_t.shape
    kb = min(p_len, 512)
    r = lax.broadcasted_iota(jnp.int32, (n_rows, n_rows), 0)
    c = lax.broadcasted_iota(jnp.int32, (n_rows, n_rows), 1)
    bu = ((r // dec_seq == c // dec_seq) & (r <= c)).astype(BF16)
    ms = _tri(kb, "row_gt_col")
    whole = lambda shape: pl.BlockSpec(shape, lambda i: (0,) * len(shape))
    return pl.pallas_call(
        functools.partial(_logf_sample_body, kb=kb),
        grid=(1,),
        in_specs=[whole((n_rows, d)),
                  pl.BlockSpec((None, N_HEADS, d), lambda i: (0, f_row0 // N_HEADS, 0)),
                  whole((1, LANES)), whole((n_rows, n_rows)), whole((n_bh, p_len)), whole((kb, kb))],
        out_specs=(whole((n_rows, N_HEADS)), whole((N_HEADS, n_rows)), whole((n_bh, p_len))),
        out_shape=(jax.ShapeDtypeStruct((n_rows, N_HEADS), F32),
                   jax.ShapeDtypeStruct((N_HEADS, n_rows), F32),
                   jax.ShapeDtypeStruct((n_bh, p_len), F32)),
        compiler_params=_cparams(("arbitrary",), vmem_mib=32),
        name="logf_sample",
    )(h_s, wt, b_row, bu, past_logf_t, ms)


SB_TQ = 256


def _sb_block(q, k, v, m_mat, carry, mask):
    z = _dot_nt(q, k)
    lsn = jnp.minimum(-z, 0.0) - jnp.log(1.0 + jnp.exp2(-jnp.abs(z))) * LOG2E
    lsp = lsn + z
    if mask is not None:
        lsn = jnp.where(mask, lsn, 0.0)
    cum = _dot_split_lhs(lsn, m_mat, 2)
    if carry is not None:
        cum = cum + carry
    w = jnp.exp2(lsp + cum)
    if mask is not None:
        w = jnp.where(mask, w, 0.0)
    o = _dot(w.astype(BF16), v)
    new_carry = cum[:, 0:1] + lsn[:, 0:1]
    return o, new_carry


def _sb_fast_body(*refs):
    _sb_fast_step(pl.program_id(0), *refs)


def _sb_fast_step(i, q_ref, kd_ref, vd_ref, kp_ref, vp_ref, gs_ref, m_ref, a_ref, c_ref):
    tq = q_ref.shape[0]
    has_prev = i > 0
    row = lax.broadcasted_iota(jnp.int32, (tq, tq), 0)
    col = lax.broadcasted_iota(jnp.int32, (tq, tq), 1)
    dmask = col < row
    pmask = jnp.logical_and(has_prev, col >= 0)
    lane = lax.broadcasted_iota(jnp.int32, (tq, LANES), 1)
    m_mat = m_ref[...]
    cacc = jnp.full((tq, LANES), NEG_BIG, F32)
    for hh in range(N_HEADS):
        sl = slice(hh * HEAD_DIM, (hh + 1) * HEAD_DIM)
        q = q_ref[:, sl]
        od, cd = _sb_block(q, kd_ref[:, sl], vd_ref[:, sl], m_mat, None, dmask)
        op, cp = _sb_block(q, kp_ref[:, sl], vp_ref[:, sl], m_mat, cd, pmask)
        a_ref[:, sl] = ((od + op) * gs_ref[:, sl].astype(F32)).astype(BF16)
        cacc = jnp.where(lane == hh, cp, cacc)
    c_ref[...] = cacc


def _sb_rest_body(q_ref, k_ref, v_ref, gs_ref, cin_ref, afast_ref, m_ref, a_ref, o_s, c_s):
    i = pl.program_id(0)
    s = pl.program_id(1)
    j = i - 2 - s
    tq = q_ref.shape[0]

    @pl.when(s == 0)
    def _():
        o_s[...] = jnp.zeros_like(o_s)
        c_s[...] = cin_ref[...]

    active = jnp.logical_and(j >= 0, jnp.max(c_s[...]) > SB_SKIP_LOG2)

    @pl.when(active)
    def _():
        lane = lax.broadcasted_iota(jnp.int32, (tq, LANES), 1)
        m_mat = m_ref[...]
        c_all = c_s[...]
        cacc = c_all
        for hh in range(N_HEADS):
            sl = slice(hh * HEAD_DIM, (hh + 1) * HEAD_DIM)
            carry = jnp.sum(jnp.where(lane == hh, c_all, 0.0), axis=1, keepdims=True)
            o, cn = _sb_block(q_ref[:, sl], k_ref[:, sl], v_ref[:, sl], m_mat, carry, None)
            o_s[:, sl] += o
            cacc = jnp.where(lane == hh, cn, cacc)
        c_s[...] = cacc

    @pl.when(s == pl.num_programs(1) - 1)
    def _():
        a_ref[...] = (afast_ref[...].astype(F32) + o_s[...] * gs_ref[...].astype(F32)).astype(BF16)


def _sb_prompt(q, k, v, gs, fox_dec=None):
    t, w = q.shape
    tq = min(SB_TQ, t)
    nq = t // tq
    m_mat = _tri(tq, "row_gt_col")
    blk = lambda f: pl.BlockSpec((tq, w), f)
    sb_in = [blk(lambda i: (i, 0)), blk(lambda i: (i, 0)), blk(lambda i: (i, 0)),
             blk(lambda i: (jnp.maximum(i - 1, 0), 0)), blk(lambda i: (jnp.maximum(i - 1, 0), 0)),
             blk(lambda i: (i, 0)),
             pl.BlockSpec((tq, tq), lambda i: (0, 0))]
    sb_args = (q, k, v, k, v, gs, m_mat)
    sb_out_specs = (blk(lambda i: (i, 0)), pl.BlockSpec((tq, LANES), lambda i: (i, 0)))
    sb_out_shape = (jax.ShapeDtypeStruct((t, w), BF16), jax.ShapeDtypeStruct((t, LANES), F32))
    a_dec = None
    if fox_dec is not None:
        dec_seq = fox_dec[-1]
        n_b, kb, nkb = _fox_dec_geometry(fox_dec[0], fox_dec[3], dec_seq)
        if n_b * nkb != nq:
            a_dec = _fox_decode(*fox_dec)
            fox_dec = None
    if fox_dec is None:
        a_fast, carry = pl.pallas_call(
            _sb_fast_body, grid=(nq,), in_specs=sb_in, out_specs=sb_out_specs, out_shape=sb_out_shape,
            compiler_params=_cparams(("parallel",), vmem_mib=48), name="sb_prompt_fast",
        )(*sb_args)
    else:
        dec_in, dec_out, dec_scratch = _fox_dec_specs(dec_seq, w, kb, lambda i: i // nkb, lambda i: i % nkb)
        n_sb, n_dec = len(sb_in), len(dec_in)

        def fused_body(*refs):
            i = pl.program_id(0)
            outs = refs[n_sb + n_dec:n_sb + n_dec + 3]
            _sb_fast_step(i, *refs[:n_sb], outs[0], outs[1])
            _fox_dec_step(i % nkb, nkb, *refs[n_sb:n_sb + n_dec], outs[2], *refs[n_sb + n_dec + 3:])

        a_fast, carry, a_dec = pl.pallas_call(
            fused_body, grid=(nq,), in_specs=sb_in + dec_in,
            out_specs=sb_out_specs + (dec_out,),
            out_shape=sb_out_shape + (jax.ShapeDtypeStruct(fox_dec[0].shape, BF16),),
            scratch_shapes=dec_scratch,
            compiler_params=_cparams(("arbitrary",), vmem_mib=56), name="sb_prompt_fox_decode",
        )(*sb_args, *fox_dec[:-1])
    if nq <= 2:
        return a_fast, a_dec

    def rest(a_fast, carry):
        kidx = lambda i, s: (jnp.maximum(i - 2 - s, 0), 0)
        return pl.pallas_call(
            _sb_rest_body,
            grid=(nq, nq - 2),
            in_specs=[blk(lambda i, s: (i, 0)), blk(kidx), blk(kidx), blk(lambda i, s: (i, 0)),
                      pl.BlockSpec((tq, LANES), lambda i, s: (i, 0)),
                      blk(lambda i, s: (i, 0)),
                      pl.BlockSpec((tq, tq), lambda i, s: (0, 0))],
            out_specs=blk(lambda i, s: (i, 0)),
            out_shape=jax.ShapeDtypeStruct((t, w), BF16),
            scratch_shapes=[pltpu.VMEM((tq, w), F32), pltpu.VMEM((tq, LANES), F32)],
            compiler_params=_cparams(("parallel", "arbitrary"), vmem_mib=48),
            name="sb_prompt_rest",
        )(q, k, v, gs, carry, a_fast, m_mat)

    need_rest = jnp.max(carry[2 * tq:, :]) > SB_SKIP_LOG2
    return lax.cond(need_rest, rest, lambda a, c: a, a_fast, carry), a_dec


FOX_T = 512
FOX_WIN_TILES = (1, 2, 3, 4, 5, 6)
FOX_ZB_MAX = 40.0
FOX_SKIP_LOG = -104.0


def _head_column(blk, hh):
    lane8 = lax.broadcasted_iota(jnp.int32, blk.shape, 1)
    return jnp.sum(jnp.where(lane8 == hh, blk, 0.0), axis=1, keepdims=True)


BIAS_ONE_LANE = 3 * N_HEADS


def _split_bias_rows(f, lane):
    hi = f.astype(BF16).astype(F32)
    r1 = f - hi
    mid = r1.astype(BF16).astype(F32)
    low = (r1 - mid).astype(BF16).astype(F32)
    out = jnp.where(lane < N_HEADS, hi,
                    jnp.where(lane < 2 * N_HEADS, pltpu.roll(mid, N_HEADS, axis=1),
                              jnp.where(lane < BIAS_ONE_LANE, pltpu.roll(low, 2 * N_HEADS, axis=1),
                                        jnp.where(lane == BIAS_ONE_LANE, 1.0, 0.0))))
    return out.astype(BF16)


def _bias_selectors():
    src = lax.broadcasted_iota(jnp.int32, (N_HEADS, LANES, LANES), 1)
    dst = lax.broadcasted_iota(jnp.int32, (N_HEADS, LANES, LANES), 2)
    head = lax.broadcasted_iota(jnp.int32, (N_HEADS, LANES, LANES), 0)

    def sel(bias_base, one_base):
        term = dst - bias_base
        takes_bias = (term >= 0) & (term < 3) & (src == term * N_HEADS + head)
        takes_one = (dst >= one_base) & (dst < one_base + 3) & (src == BIAS_ONE_LANE)
        return (takes_bias | takes_one).astype(BF16)

    return jnp.stack([sel(0, 3), sel(3, 0)])


def _fox_fast_body(live_ref, q_ref, k_ref, v_ref, qb_ref, kb_ref, sel_ref, gs_ref, a_ref, kaug_s):
    hh = pl.program_id(0)
    qb = pl.program_id(1)
    nq = pl.num_programs(1)
    tb = q_ref.shape[0]
    t = k_ref.shape[0]

    @pl.when(qb == 0)
    def _():
        def build(c, carry):
            start = pl.multiple_of(c * tb, tb)
            kaug_s[pl.ds(start, tb), :] = _dot(kb_ref[pl.ds(start, tb), :], sel_ref[1]).astype(BF16)
            return carry
        lax.fori_loop(0, nq, build, 0)

    q_start = pl.multiple_of(qb * tb, tb)
    q_end = q_start + tb
    q2 = jnp.concatenate([q_ref[...], _dot(qb_ref[...], sel_ref[0]).astype(BF16)], axis=1)
    row = lax.broadcasted_iota(jnp.int32, (tb, 1), 0)
    live = live_ref[hh * nq + qb]

    def run(tiles):
        span = min(tiles * tb, t)
        lane_w = lax.broadcasted_iota(jnp.int32, (span, LANES), 1)
        ones_blk = jnp.where(lane_w == 0, 1.0, 0.0).astype(BF16)
        col = lax.broadcasted_iota(jnp.int32, (1, span), 1)

        def window(j, acc):
            upper = q_end - j * span
            start = pl.multiple_of(jnp.maximum(upper - span, 0), tb)
            k2 = jnp.concatenate([k_ref[pl.ds(start, span), :], kaug_s[pl.ds(start, span), :]], axis=1)
            p = jnp.exp2(_dot_nt(q2, k2))
            valid = jnp.logical_and(col - (q_start - start) <= row, col < upper - start)
            p = jnp.where(valid, p, 0.0).astype(BF16)
            v2 = jnp.concatenate([v_ref[pl.ds(start, span), :], ones_blk], axis=1)
            return acc + _dot(p, v2)

        n_win = (live + (tiles - 1)) // tiles
        acc = lax.fori_loop(0, n_win, window, jnp.zeros((tb, 2 * HEAD_DIM), F32))
        o = acc[:, :HEAD_DIM] / acc[:, HEAD_DIM:HEAD_DIM + 1]
        a_ref[...] = (o * gs_ref[...].astype(F32)).astype(BF16)

    bounds = (0,) + FOX_WIN_TILES
    for lo_t, hi_t in zip(bounds[:-1], bounds[1:]):
        last = hi_t == FOX_WIN_TILES[-1]
        cond = live > lo_t if last else jnp.logical_and(live > lo_t, live <= hi_t)
        pl.when(cond)(functools.partial(run, hi_t))


def _fox_slow_body(q_ref, k_ref, v_ref, fcol_ref, frow_ref, gs_ref, a_ref):
    hh = pl.program_id(0)
    qb = pl.program_id(1)
    tb = q_ref.shape[0]
    q = q_ref[...]
    fq = _head_column(fcol_ref[...], hh) * LOG2E

    def scores(kb):
        start = pl.multiple_of(kb * tb, tb)
        k = k_ref[pl.ds(start, tb), :]
        fk = frow_ref[0, :, pl.ds(start, tb)] * LOG2E
        return _dot_nt(q, k) + (fq - fk), v_ref[pl.ds(start, tb), :]

    def update(s, v, carry):
        m, l, acc = carry
        m_new = jnp.maximum(m, jnp.max(s, axis=1, keepdims=True))
        alpha = jnp.exp2(m - m_new)
        p = jnp.exp2(s - m_new)
        l = alpha * l + jnp.sum(p, axis=1, keepdims=True)
        acc = alpha * acc + _dot(p.astype(BF16), v)
        return m_new, l, acc

    def body(kb, carry):
        s, v = scores(kb)
        return update(s, v, carry)

    init = (jnp.full((tb, 1), NEG_BIG, F32), jnp.zeros((tb, 1), F32), jnp.zeros((tb, HEAD_DIM), F32))
    carry = lax.fori_loop(0, qb, body, init)
    s, v = scores(qb)
    row = lax.broadcasted_iota(jnp.int32, (tb, tb), 0)
    col = lax.broadcasted_iota(jnp.int32, (tb, tb), 1)
    s = jnp.where(col <= row, s, NEG_BIG)
    _, l, acc = update(s, v, carry)
    a_ref[...] = ((acc / l) * gs_ref[...].astype(F32)).astype(BF16)


def _fox_prompt(q, k, v, f_col, f_row, gs, zb, q_bias, k_bias):
    t, w = q.shape
    tb = min(FOX_T, t)
    nq = t // tb
    out_shape = jax.ShapeDtypeStruct((t, w), BF16)

    def fast(q, k, v, f_col, f_row, gs, q_bias, k_bias):
        f_start = f_row[:, ::tb]
        f_end = f_row[:, tb - 1::tb]
        dead = (f_start[:, :, None] - f_end[:, None, :]) < FOX_SKIP_LOG
        lo = jnp.sum(dead, axis=2).astype(jnp.int32)
        live = (jnp.arange(1, nq + 1, dtype=jnp.int32)[None, :] - lo).reshape(-1)
        head_blk = lambda h, i, live: (i, h)
        head_all = lambda h, i, live: (0, h)
        grid_spec = pltpu.PrefetchScalarGridSpec(
            num_scalar_prefetch=1,
            grid=(N_HEADS, nq),
            in_specs=[pl.BlockSpec((tb, HEAD_DIM), head_blk),
                      pl.BlockSpec((t, HEAD_DIM), head_all),
                      pl.BlockSpec((t, HEAD_DIM), head_all),
                      pl.BlockSpec((tb, LANES), lambda h, i, live: (i, 0)),
                      pl.BlockSpec((t, LANES), lambda h, i, live: (0, 0), pipeline_mode=pl.Buffered(1)),
                      pl.BlockSpec((2, None, LANES, LANES), lambda h, i, live: (0, h, 0, 0)),
                      pl.BlockSpec((tb, HEAD_DIM), head_blk)],
            out_specs=pl.BlockSpec((tb, HEAD_DIM), head_blk),
            scratch_shapes=[pltpu.VMEM((t, LANES), BF16)])
        return pl.pallas_call(
            _fox_fast_body, grid_spec=grid_spec, out_shape=out_shape,
            compiler_params=_cparams(("parallel", "arbitrary"), vmem_mib=48),
            name="fox_prompt_fast",
        )(live, q, k, v, q_bias, k_bias, _bias_selectors(), gs)

    def slow(q, k, v, f_col, f_row, gs, q_bias, k_bias):
        f_row3 = f_row.reshape(N_HEADS, 1, t)
        return pl.pallas_call(
            _fox_slow_body,
            grid=(N_HEADS, nq),
            in_specs=[pl.BlockSpec((tb, HEAD_DIM), lambda h, i: (i, h)),
                      pl.BlockSpec((t, HEAD_DIM), lambda h, i: (0, h)),
                      pl.BlockSpec((t, HEAD_DIM), lambda h, i: (0, h)),
                      pl.BlockSpec((tb, N_HEADS), lambda h, i: (i, 0)),
                      pl.BlockSpec((1, 1, t), lambda h, i: (h, 0, 0)),
                      pl.BlockSpec((tb, HEAD_DIM), lambda h, i: (i, h))],
            out_specs=pl.BlockSpec((tb, HEAD_DIM), lambda h, i: (i, h)),
            out_shape=out_shape,
            compiler_params=_cparams(("parallel", "parallel"), vmem_mib=48),
            name="fox_prompt_slow",
        )(q, k, v, f_col, f_row3, gs)

    return lax.cond(zb <= FOX_ZB_MAX, fast, slow, q, k, v, f_col, f_row, gs, q_bias, k_bias)


DEC_KB = 2048
SB_DEC_WIN = 256


def _own_head_mask():
    row = lax.broadcasted_iota(jnp.int32, (SUBLANES, W_ATT), 0)
    col = lax.broadcasted_iota(jnp.int32, (SUBLANES, W_ATT), 1)
    return (col // HEAD_DIM) == row


def _expand_q(q_ref, qx_s):
    n_tok = q_ref.shape[0]
    qf = q_ref[...].astype(F32)
    own = _own_head_mask()
    for tkn in range(n_tok):
        rep = jnp.broadcast_to(qf[tkn:tkn + 1, :], (SUBLANES, W_ATT))
        qx_s[tkn * N_HEADS:(tkn + 1) * N_HEADS, :] = jnp.where(own, rep, 0.0)


def _pad_new(kn_ref, vn_ref, knp_s, vnp_s):
    n_tok = kn_ref.shape[0]
    knp_s[...] = jnp.zeros_like(knp_s)
    vnp_s[...] = jnp.zeros_like(vnp_s)
    knp_s[0:n_tok, :] = kn_ref[...]
    vnp_s[0:n_tok, :] = vn_ref[...]


def _gather_heads(c_ref, dst_s):
    n_keys = dst_s.shape[0]
    for hh in range(N_HEADS):
        dst_s[:, hh * HEAD_DIM:(hh + 1) * HEAD_DIM] = (
            c_ref[0, pl.ds(hh, n_keys, stride=N_HEADS), :].astype(BF16))


def _collapse_heads(o_full, out_s):
    n_tok = o_full.shape[0] // N_HEADS
    own = _own_head_mask()
    for tkn in range(n_tok):
        blk = jnp.where(own, o_full[tkn * N_HEADS:(tkn + 1) * N_HEADS, :], 0.0)
        out_s[tkn:tkn + 1, :] = jnp.sum(blk, axis=0, keepdims=True)


def _sb_dec_fast_body(q_ref, kn_ref, vn_ref, kc_ref, vc_ref, gs_ref, mn_ref, mc_ref, a_ref, c_ref,
                      qx_s, knp_s, vnp_s, kx_s, vx_s, out_s):
    n_rows = qx_s.shape[0]
    _expand_q(q_ref, qx_s)
    _pad_new(kn_ref, vn_ref, knp_s, vnp_s)
    qx = qx_s[...].astype(BF16)
    row = lax.broadcasted_iota(jnp.int32, (n_rows, LANES), 0)
    col = lax.broadcasted_iota(jnp.int32, (n_rows, LANES), 1)
    mask = col < row // N_HEADS
    o_new, carry = _sb_block(qx, knp_s[...], vnp_s[...], mn_ref[...], None, mask)
    _gather_heads(kc_ref, kx_s)
    _gather_heads(vc_ref, vx_s)
    o_win, carry = _sb_block(qx, kx_s[...], vx_s[...], mc_ref[...], carry, None)
    _collapse_heads(o_new + o_win, out_s)
    a_ref[...] = (out_s[...] * gs_ref[...].astype(F32)).astype(BF16)
    c_ref[...] = jnp.broadcast_to(carry, c_ref.shape)


def _sb_dec_rest_body(q_ref, kc_ref, vc_ref, gs_ref, cin_ref, afast_ref, mc_ref, a_ref,
                      qx_s, kx_s, vx_s, acc_s, c_s, out_s):
    s = pl.program_id(1)

    @pl.when(s == 0)
    def _():
        _expand_q(q_ref, qx_s)
        acc_s[...] = jnp.zeros_like(acc_s)
        c_s[...] = cin_ref[:, 0:1]

    @pl.when(jnp.max(c_s[...]) > SB_SKIP_LOG2)
    def _():
        _gather_heads(kc_ref, kx_s)
        _gather_heads(vc_ref, vx_s)
        o, carry = _sb_block(qx_s[...].astype(BF16), kx_s[...], vx_s[...], mc_ref[...], c_s[...], None)
        acc_s[...] += o
        c_s[...] = carry

    @pl.when(s == pl.num_programs(1) - 1)
    def _():
        _collapse_heads(acc_s[...], out_s)
        a_ref[...] = (afast_ref[...].astype(F32) + out_s[...] * gs_ref[...].astype(F32)).astype(BF16)


def _sb_decode(q, k_new, v_new, k_cache, v_cache, gs, dec_seq):
    n_rows_all, w = q.shape
    n_b = n_rows_all // dec_seq
    p_len = k_cache.shape[1] // N_HEADS
    win = min(SB_DEC_WIN, p_len)
    n_win = p_len // win
    n_rows = dec_seq * N_HEADS
    m_new = _tri(LANES, "row_gt_col")
    m_win = _tri(win, "row_gt_col")
    tok = lambda b: (b, 0)
    a_fast, carry = pl.pallas_call(
        _sb_dec_fast_body,
        grid=(n_b,),
        in_specs=[pl.BlockSpec((dec_seq, w), tok), pl.BlockSpec((dec_seq, w), tok),
                  pl.BlockSpec((dec_seq, w), tok),
                  pl.BlockSpec((1, win * N_HEADS, HEAD_DIM), lambda b: (b, n_win - 1, 0)),
                  pl.BlockSpec((1, win * N_HEADS, HEAD_DIM), lambda b: (b, n_win - 1, 0)),
                  pl.BlockSpec((dec_seq, w), tok),
                  pl.BlockSpec((LANES, LANES), lambda b: (0, 0)),
                  pl.BlockSpec((win, win), lambda b: (0, 0))],
        out_specs=(pl.BlockSpec((dec_seq, w), tok), pl.BlockSpec((n_rows, LANES), tok)),
        out_shape=(jax.ShapeDtypeStruct((n_rows_all, w), BF16),
                   jax.ShapeDtypeStruct((n_b * n_rows, LANES), F32)),
        scratch_shapes=[pltpu.VMEM((n_rows, w), F32), pltpu.VMEM((LANES, w), BF16),
                        pltpu.VMEM((LANES, w), BF16), pltpu.VMEM((win, w), BF16),
                        pltpu.VMEM((win, w), BF16), pltpu.VMEM((dec_seq, w), F32)],
        compiler_params=_cparams(("parallel",), vmem_mib=48),
        name="sb_decode_fast",
    )(q, k_new, v_new, k_cache, v_cache, gs, m_new, m_win)
    if n_win <= 1:
        return a_fast

    def rest(a_fast, carry):
        tok2 = lambda b, s: (b, 0)
        older = lambda b, s: (b, n_win - 2 - s, 0)
        return pl.pallas_call(
            _sb_dec_rest_body,
            grid=(n_b, n_win - 1),
            in_specs=[pl.BlockSpec((dec_seq, w), tok2),
                      pl.BlockSpec((1, win * N_HEADS, HEAD_DIM), older),
                      pl.BlockSpec((1, win * N_HEADS, HEAD_DIM), older),
                      pl.BlockSpec((dec_seq, w), tok2),
                      pl.BlockSpec((n_rows, LANES), tok2),
                      pl.BlockSpec((dec_seq, w), tok2),
                      pl.BlockSpec((win, win), lambda b, s: (0, 0))],
            out_specs=pl.BlockSpec((dec_seq, w), tok2),
            out_shape=jax.ShapeDtypeStruct((n_rows_all, w), BF16),
            scratch_shapes=[pltpu.VMEM((n_rows, w), F32), pltpu.VMEM((win, w), BF16),
                            pltpu.VMEM((win, w), BF16), pltpu.VMEM((n_rows, w), F32),
                            pltpu.VMEM((n_rows, 1), F32), pltpu.VMEM((dec_seq, w), F32)],
            compiler_params=_cparams(("parallel", "arbitrary"), vmem_mib=48),
            name="sb_decode_rest",
        )(q, k_cache, v_cache, gs, carry, a_fast, m_win)

    need_rest = jnp.max(carry) > SB_SKIP_LOG2
    return lax.cond(need_rest, rest, lambda a, c: a, a_fast, carry)


def _fox_dec_body(*refs):
    _fox_dec_step(pl.program_id(1), pl.num_programs(1), *refs)


def _fox_dec_step(s, n_s, q_ref, kn_ref, vn_ref, kc_ref, vc_ref, gs_ref, cn_ref, r_ref, a_ref,
                  qx_s, knp_s, vnp_s, kx_s, vx_s, acc_s, m_s, l_s, fq_s, out_s):
    n_rows = qx_s.shape[0]
    n_tok = n_rows // N_HEADS

    def update(sc, v):
        m = m_s[...]
        m_new = jnp.maximum(m, jnp.max(sc, axis=1, keepdims=True))
        alpha = jnp.exp2(m - m_new)
        p = jnp.exp2(sc - m_new)
        l_s[...] = alpha * l_s[...] + jnp.sum(p, axis=1, keepdims=True)
        acc_s[...] = alpha * acc_s[...] + _dot(p.astype(BF16), v)
        m_s[...] = m_new

    @pl.when(s == 0)
    def _():
        _expand_q(q_ref, qx_s)
        _pad_new(kn_ref, vn_ref, knp_s, vnp_s)
        m_s[...] = jnp.full_like(m_s, NEG_BIG)
        l_s[...] = jnp.zeros_like(l_s)
        acc_s[...] = jnp.zeros_like(acc_s)
        row = lax.broadcasted_iota(jnp.int32, (n_rows, LANES), 0)
        col = lax.broadcasted_iota(jnp.int32, (n_rows, LANES), 1)
        cn = jnp.concatenate([cn_ref[0]] * n_tok, axis=0) * LOG2E
        fq = jnp.sum(jnp.where(col == row // N_HEADS, cn, 0.0), axis=1, keepdims=True)
        fq_s[...] = fq
        sc = _dot_nt(qx_s[...].astype(BF16), knp_s[...]) + (fq - cn)
        sc = jnp.where(col <= row // N_HEADS, sc, NEG_BIG)
        update(sc, vnp_s[...])

    _gather_heads(kc_ref, kx_s)
    _gather_heads(vc_ref, vx_s)
    r = jnp.concatenate([r_ref[...]] * n_tok, axis=0) * LOG2E
    update(_dot_nt(qx_s[...].astype(BF16), kx_s[...]) + (fq_s[...] + r), vx_s[...])

    @pl.when(s == n_s - 1)
    def _():
        _collapse_heads(acc_s[...] / l_s[...], out_s)
        a_ref[...] = (out_s[...] * gs_ref[...].astype(F32)).astype(BF16)


def _fox_dec_geometry(q, k_cache, dec_seq):
    n_b = q.shape[0] // dec_seq
    p_len = k_cache.shape[1] // N_HEADS
    kb = min(DEC_KB, p_len)
    return n_b, kb, p_len // kb


def _fox_dec_specs(dec_seq, w, kb, stream, block):
    n_rows = dec_seq * N_HEADS
    tok = lambda *g: (stream(*g), 0)
    cache = lambda *g: (stream(*g), block(*g), 0)
    in_specs = [pl.BlockSpec((dec_seq, w), tok), pl.BlockSpec((dec_seq, w), tok),
                pl.BlockSpec((dec_seq, w), tok),
                pl.BlockSpec((1, kb * N_HEADS, HEAD_DIM), cache),
                pl.BlockSpec((1, kb * N_HEADS, HEAD_DIM), cache),
                pl.BlockSpec((dec_seq, w), tok),
                pl.BlockSpec((1, N_HEADS, LANES), lambda *g: (stream(*g), 0, 0)),
                pl.BlockSpec((N_HEADS, kb), lambda *g: (stream(*g), block(*g)))]
    scratch = [pltpu.VMEM((n_rows, w), F32), pltpu.VMEM((LANES, w), BF16),
               pltpu.VMEM((LANES, w), BF16), pltpu.VMEM((kb, w), BF16),
               pltpu.VMEM((kb, w), BF16), pltpu.VMEM((n_rows, w), F32),
               pltpu.VMEM((n_rows, 1), F32), pltpu.VMEM((n_rows, 1), F32),
               pltpu.VMEM((n_rows, 1), F32), pltpu.VMEM((dec_seq, w), F32)]
    return in_specs, pl.BlockSpec((dec_seq, w), tok), scratch


def _fox_decode(q, k_new, v_new, k_cache, v_cache, gs, c_new, r_past, dec_seq):
    w = q.shape[1]
    n_b, kb, nkb = _fox_dec_geometry(q, k_cache, dec_seq)
    in_specs, out_spec, scratch = _fox_dec_specs(dec_seq, w, kb, lambda b, s: b, lambda b, s: s)
    return pl.pallas_call(
        _fox_dec_body,
        grid=(n_b, nkb),
        in_specs=in_specs,
        out_specs=out_spec,
        out_shape=jax.ShapeDtypeStruct(q.shape, BF16),
        scratch_shapes=scratch,
        compiler_params=_cparams(("parallel", "arbitrary"), vmem_mib=56),
        name="fox_decode",
    )(q, k_new, v_new, k_cache, v_cache, gs, c_new, r_past)


def _post_body(x_ref, asb_ref, afx_ref, msb_ref, mfx_ref, wsb_ref, wfx_ref, wo_ref, y_ref):
    u_sb = _dot(asb_ref[...], wsb_ref[...])
    u_fx = _dot(afx_ref[...], wfx_ref[...])
    merged = msb_ref[...].astype(F32) * u_sb + mfx_ref[...].astype(F32) * u_fx
    y_ref[...] = x_ref[...] + _dot(merged.astype(BF16), wo_ref[...])


def _post(x2d, a_sb, a_fx, m_sig, w_sb, w_fx, w_o):
    m, d = x2d.shape
    w = a_sb.shape[1]
    tm = min(m, 512)
    row = lambda i: (i, 0)
    const = lambda i: (0, 0)
    resident = functools.partial(pl.BlockSpec, index_map=const, pipeline_mode=pl.Buffered(1))
    return pl.pallas_call(
        _post_body,
        grid=(m // tm,),
        in_specs=[pl.BlockSpec((tm, d), row), pl.BlockSpec((tm, w), row), pl.BlockSpec((tm, w), row),
                  pl.BlockSpec((tm, d), lambda i: (i, 0)), pl.BlockSpec((tm, d), lambda i: (i, 1)),
                  resident((w, d)), resident((w, d)), resident((d, d))],
        out_specs=pl.BlockSpec((tm, d), row),
        out_shape=jax.ShapeDtypeStruct((m, d), F32),
        compiler_params=_cparams(("parallel",), vmem_mib=56),
        name="post",
    )(x2d, a_sb, a_fx, m_sig, m_sig, w_sb, w_fx, w_o)


def _project_all(hp, hs, w, qnw, knw):
    seg = lambda c: c * W_ATT
    q_sb = _proj(hp, hs, w, seg(0), "scale")
    k_sb = _proj(hp, hs, w, seg(1), "kv")
    v_sb = _proj(hp, hs, w, seg(2), "kv")
    g_sb = _proj(hp, hs, w, seg(3), "silu")
    q_fx = _proj(hp, hs, w, seg(4), "qnorm", nw=qnw)
    k_fx = _proj(hp, hs, w, seg(5), "knorm", nw=knw)
    v_fx = _proj(hp, hs, w, seg(6), "kv")
    g_fx = _proj(hp, hs, w, seg(7), "silu")
    m_sig = _proj(hp, hs, w, seg(8) + N_HEADS, "sigmoid", n_tiles=2 * D_MODEL // W_ATT)
    calls = (q_sb, k_sb, v_sb, g_sb, q_fx, k_fx, v_fx, g_fx, m_sig)
    prompt = tuple(o for r in calls for o in r[:len(r) // 2])
    decode = tuple(o for r in calls for o in r[len(r) // 2:])
    return prompt, decode


def kernel(x_prompt, x_sample, cache_sb_k, cache_sb_v, cache_fox_k, cache_fox_v, cache_fox_logf,
           norm_w, w_in, b_forget, q_norm_w, k_norm_w, w_branch_sb, w_branch_fox, w_out):
    depth = norm_w.shape[0]
    assert depth == 1, "single-layer step"
    bsz, seq, d = x_prompt.shape
    assert bsz == 1
    n_dec, dec_seq, _ = x_sample.shape
    p_len = cache_sb_k.shape[2]
    n_main = 8 * W_ATT

    w_t = jnp.swapaxes(w_in, 1, 2)
    b_row = jnp.pad(b_forget[0].astype(F32)[None, :], ((0, 0), (0, LANES - N_HEADS)))
    qnw = jnp.tile(q_norm_w[0].astype(F32), N_HEADS)[None, :]
    knw = jnp.tile(k_norm_w[0].astype(F32), N_HEADS)[None, :]
    nw_row = norm_w[0].astype(F32)[None, :]
    w_sb = w_branch_sb[0].astype(BF16)
    w_fx = w_branch_fox[0].astype(BF16)
    w_o = w_out[0].astype(BF16)

    xp = x_prompt.reshape(seq, d)
    xs = x_sample.reshape(n_dec * dec_seq, d)
    hp = _rmsnorm(xp, nw_row)
    hs = _rmsnorm(xs, nw_row)
    prompt_proj, decode_proj = _project_all(hp, hs, w_t, qnw, knw)

    (q_sb, k_sb, k_sb_b, v_sb, v_sb_b, g_sb, q_fx, k_fx, k_fx_b, v_fx, v_fx_b, g_fx,
     m_sig) = prompt_proj
    (sq_sb, sk_sb, sk_sb_b, sv_sb, sv_sb_b, sg_sb, sq_fx, sk_fx, sk_fx_b, sv_fx, sv_fx_b, sg_fx,
     sm_sig) = decode_proj

    past_logf_t = jnp.transpose(cache_fox_logf[0].astype(F32), (0, 2, 1)).reshape(n_dec * N_HEADS, p_len)
    s_logf, c_all, r_past = _logf_sample(hs, w_t, n_main, b_row, past_logf_t, dec_seq)
    c_new = jnp.transpose(c_all.reshape(N_HEADS, n_dec, dec_seq), (1, 0, 2))
    c_new = jnp.pad(c_new, ((0, 0), (0, 0), (0, LANES - dec_seq)))
    kc_sb = cache_sb_k[0].reshape(n_dec, p_len * N_HEADS, HEAD_DIM)
    vc_sb = cache_sb_v[0].reshape(n_dec, p_len * N_HEADS, HEAD_DIM)
    kc_fx = cache_fox_k[0].reshape(n_dec, p_len * N_HEADS, HEAD_DIM)
    vc_fx = cache_fox_v[0].reshape(n_dec, p_len * N_HEADS, HEAD_DIM)

    zb = (1.02 * HEAD_DIM * QK_SCALE) * jnp.max(jnp.abs(q_norm_w[0])) * jnp.max(jnp.abs(k_norm_w[0]))
    zb = zb.astype(F32)
    logf, f_col, f_row, q_bias, k_bias = _logf_prompt(hp, w_t, n_main, b_row, (zb * LOG2E).reshape(1, 1))
    a_sb, sa_fx = _sb_prompt(q_sb, k_sb_b, v_sb_b, g_sb,
                             fox_dec=(sq_fx, sk_fx_b, sv_fx_b, kc_fx, vc_fx, sg_fx, c_new, r_past, dec_seq))
    a_fx = _fox_prompt(q_fx, k_fx_b, v_fx_b, f_col, f_row, g_fx, zb, q_bias, k_bias)
    sa_sb = _sb_decode(sq_sb, sk_sb_b, sv_sb_b, kc_sb, vc_sb, sg_sb, dec_seq)
    y_prompt = _post(xp, a_sb, a_fx, m_sig, w_sb, w_fx, w_o).reshape(bsz, seq, d)
    y_sample = _post(xs, sa_sb, sa_fx, sm_sig, w_sb, w_fx, w_o).reshape(n_dec, dec_seq, d)

    hd = (N_HEADS, HEAD_DIM)
    return (y_prompt, y_sample,
            k_sb.reshape(1, bsz, seq, *hd), v_sb.reshape(1, bsz, seq, *hd),
            k_fx.reshape(1, bsz, seq, *hd), v_fx.reshape(1, bsz, seq, *hd),
            logf.reshape(1, bsz, seq, N_HEADS),
            sk_sb.reshape(1, n_dec, dec_seq, *hd), sv_sb.reshape(1, n_dec, dec_seq, *hd),
            sk_fx.reshape(1, n_dec, dec_seq, *hd), sv_fx.reshape(1, n_dec, dec_seq, *hd),
            s_logf.reshape(1, n_dec, dec_seq, N_HEADS))
```

```python
import functools

import jax
import jax.numpy as jnp
from jax import lax
from jax.experimental import pallas as pl
from jax.experimental.pallas import tpu as pltpu

F32 = jnp.float32
BF16 = jnp.bfloat16

D_MODEL = 2048
N_HEADS = 8
HEAD_DIM = 128
W_ATT = N_HEADS * HEAD_DIM
RMS_EPS = 1e-6
QK_SCALE = HEAD_DIM ** -0.5
LOG2E = 1.4426950408889634
LANES = 128
SUBLANES = 8
SB_SKIP_LOG2 = -150.0
NEG_BIG = -1e30
MIB = 1024 * 1024


def _cparams(sem, vmem_mib=None):
    kw = dict(dimension_semantics=sem)
    if vmem_mib is not None:
        kw["vmem_limit_bytes"] = vmem_mib * MIB
    return pltpu.CompilerParams(**kw)


def _softplus_neg_abs(z):
    return jnp.log1p(jnp.exp(-jnp.abs(z)))


def _log_sigmoid(z):
    return jnp.minimum(z, 0.0) - _softplus_neg_abs(z)


def _split_bf16(x, n):
    parts = []
    r = x
    for _ in range(n - 1):
        p = r.astype(BF16)
        parts.append(p)
        r = r - p.astype(F32)
    parts.append(r.astype(BF16))
    return parts


def _dot(a, b):
    return jnp.dot(a, b, preferred_element_type=F32)


def _dot_nt(a, b):
    return lax.dot_general(a, b, (((1,), (1,)), ((), ())), preferred_element_type=F32)


def _dot_split_lhs(x, m, n):
    acc = None
    for p in _split_bf16(x, n):
        t = _dot(p, m)
        acc = t if acc is None else acc + t
    return acc


def _dot_split_rhs(m, x, n):
    acc = None
    for p in _split_bf16(x, n):
        t = _dot(m, p)
        acc = t if acc is None else acc + t
    return acc


def _tri(n, kind):
    r = lax.broadcasted_iota(jnp.int32, (n, n), 0)
    c = lax.broadcasted_iota(jnp.int32, (n, n), 1)
    if kind == "row_gt_col":
        m = r > c
    elif kind == "row_le_col":
        m = r <= c
    elif kind == "row_ge_col":
        m = r >= c
    else:
        raise ValueError(kind)
    return m.astype(BF16)


def _rmsnorm_body(x_ref, w_ref, o_ref):
    x = x_ref[...]
    ms = jnp.mean(x * x, axis=-1, keepdims=True)
    o_ref[...] = (x * lax.rsqrt(ms + RMS_EPS) * w_ref[...]).astype(o_ref.dtype)


def _rmsnorm(x2d, w_row):
    m, d = x2d.shape
    tm = min(m, 512)
    return pl.pallas_call(
        _rmsnorm_body,
        grid=(m // tm,),
        in_specs=[pl.BlockSpec((tm, d), lambda i: (i, 0)),
                  pl.BlockSpec((1, d), lambda i: (0, 0))],
        out_specs=pl.BlockSpec((tm, d), lambda i: (i, 0)),
        out_shape=jax.ShapeDtypeStruct((m, d), BF16),
        compiler_params=_cparams(("parallel",)),
        name="rmsnorm",
    )(x2d, w_row)


def _head_rmsnorm(acc, nw):
    parts = []
    for hh in range(N_HEADS):
        a = acc[:, hh * HEAD_DIM:(hh + 1) * HEAD_DIM]
        ms = jnp.mean(a * a, axis=-1, keepdims=True)
        parts.append(a * lax.rsqrt(ms + RMS_EPS))
    return jnp.concatenate(parts, axis=1) * nw


def _proj_epilogue(acc, kind, nw_ref, outs):
    if kind == "scale":
        outs[0][...] = (acc * (QK_SCALE * LOG2E)).astype(BF16)
    elif kind == "kv":
        outs[0][...] = acc
        outs[1][...] = acc.astype(BF16)
    elif kind == "silu":
        outs[0][...] = (acc * jax.nn.sigmoid(acc)).astype(BF16)
    elif kind == "sigmoid":
        outs[0][...] = jax.nn.sigmoid(acc).astype(BF16)
    elif kind == "qnorm":
        outs[0][...] = (_head_rmsnorm(acc, nw_ref[...]) * (QK_SCALE * LOG2E)).astype(BF16)
    elif kind == "knorm":
        y = _head_rmsnorm(acc, nw_ref[...])
        outs[0][...] = y
        outs[1][...] = y.astype(BF16)
    else:
        raise ValueError(kind)


def _proj_body(*refs, kind, n_out, shift):
    hp_ref, hs_ref, w_ref = refs[:3]
    pos = 3
    wn_ref = nw_ref = None
    if shift:
        wn_ref = refs[pos]
        pos += 1
    if kind in ("qnorm", "knorm"):
        nw_ref = refs[pos]
        pos += 1
    outs_p = refs[pos:pos + n_out]
    outs_s = refs[pos + n_out:pos + 2 * n_out]
    wb_s = refs[-1]
    i = pl.program_id(1)
    n_prompt = pl.num_programs(1) - 1

    @pl.when(i == 0)
    def _():
        if shift:
            wt = jnp.concatenate([w_ref[shift:, :], wn_ref[...]], axis=0)
        else:
            wt = w_ref[...]
        wb_s[...] = wt.T.astype(BF16)

    @pl.when(i < n_prompt)
    def _():
        _proj_epilogue(_dot(hp_ref[...], wb_s[...]), kind, nw_ref, outs_p)

    @pl.when(i == n_prompt)
    def _():
        _proj_epilogue(_dot(hs_ref[...], wb_s[...]), kind, nw_ref, outs_s)


def _proj(hp, hs, wt, col0, kind, nw=None, n_tiles=1):
    mp, d = hp.shape
    ms = hs.shape[0]
    tn = W_ATT
    dtypes = (F32, BF16) if kind in ("kv", "knorm") else (BF16,)
    tm = min(mp, 1024)
    n_prompt = mp // tm
    blk0 = col0 // tn
    shift = col0 - blk0 * tn
    assert shift in (0, SUBLANES)
    grid = (n_tiles, n_prompt + 1)
    prow = lambda j, i: (jnp.minimum(i, n_prompt - 1), 0)
    w_mode = {} if n_tiles > 1 else dict(pipeline_mode=pl.Buffered(1))
    in_specs = [pl.BlockSpec((tm, d), prow),
                pl.BlockSpec((ms, d), lambda j, i: (0, 0)),
                pl.BlockSpec((None, tn, d), lambda j, i: (0, blk0 + j, 0), **w_mode)]
    args = [hp, hs, wt]
    if shift:
        per = tn // shift
        in_specs.append(pl.BlockSpec((None, shift, d), lambda j, i: (0, (blk0 + j + 1) * per, 0)))
        args.append(wt)
    if kind in ("qnorm", "knorm"):
        in_specs.append(pl.BlockSpec((1, tn), lambda j, i: (0, 0)))
        args.append(nw)
    n_cols = tn * n_tiles
    p_block = pl.BlockSpec((tm, tn), lambda j, i: (jnp.minimum(i, n_prompt - 1), j))
    s_block = pl.BlockSpec((ms, tn), lambda j, i: (0, j))
    out_shape = tuple(jax.ShapeDtypeStruct((mp, n_cols), t) for t in dtypes) + \
        tuple(jax.ShapeDtypeStruct((ms, n_cols), t) for t in dtypes)
    out_specs = (p_block,) * len(dtypes) + (s_block,) * len(dtypes)
    return pl.pallas_call(
        functools.partial(_proj_body, kind=kind, n_out=len(dtypes), shift=shift),
        grid=grid,
        in_specs=in_specs,
        out_specs=out_specs,
        out_shape=out_shape,
        scratch_shapes=[pltpu.VMEM((d, tn), BF16)],
        compiler_params=_cparams(("parallel", "arbitrary"), vmem_mib=56),
        name="proj_" + kind,
    )(*args)


def _forget_weight(wft_ref):
    wft = wft_ref[...]
    pad = jnp.zeros((LANES - wft.shape[0], wft.shape[1]), F32)
    return jnp.concatenate([wft, pad], axis=0).T.astype(BF16)


def _logf_body(h_ref, wft_ref, brow_ref, l_ref, c2_ref, lf_ref, fcol_ref, frow_ref, qbias_ref, kbias_ref,
               ccol_s, wf_s):
    i = pl.program_id(0)
    tm = h_ref.shape[0]

    @pl.when(i == 0)
    def _():
        ccol_s[...] = jnp.zeros_like(ccol_s)
        wf_s[...] = _forget_weight(wft_ref)

    lf = _log_sigmoid(_dot(h_ref[...], wf_s[...]) + brow_ref[...])
    lf_ref[...] = lf[:, :N_HEADS]
    f_col = _dot_split_rhs(l_ref[...], lf, 3) + ccol_s[...]
    fcol_ref[...] = f_col[:, :N_HEADS]
    ccol_s[...] = f_col[tm - 1:tm, :]
    frow_ref[...] = f_col.T[:N_HEADS, :]
    lane = lax.broadcasted_iota(jnp.int32, f_col.shape, 1)
    f2 = f_col * LOG2E
    qbias_ref[...] = _split_bias_rows(f2 - c2_ref[...], lane)
    kbias_ref[...] = _split_bias_rows(-f2, lane)


def _logf_prompt(h, wt, f_row0, b_row, c2):
    m, d = h.shape
    tm = min(m, 512)
    l_mat = _tri(tm, "row_ge_col")
    const = lambda i: (0, 0)
    return pl.pallas_call(
        _logf_body,
        grid=(m // tm,),
        in_specs=[pl.BlockSpec((tm, d), lambda i: (i, 0)),
                  pl.BlockSpec((None, N_HEADS, d), lambda i: (0, f_row0 // N_HEADS, 0)),
                  pl.BlockSpec((1, LANES), const),
                  pl.BlockSpec((tm, tm), const),
                  pl.BlockSpec((1, 1), const)],
        out_specs=(pl.BlockSpec((tm, N_HEADS), lambda i: (i, 0)),
                   pl.BlockSpec((tm, N_HEADS), lambda i: (i, 0)),
                   pl.BlockSpec((N_HEADS, tm), lambda i: (0, i)),
                   pl.BlockSpec((tm, LANES), lambda i: (i, 0)),
                   pl.BlockSpec((tm, LANES), lambda i: (i, 0))),
        out_shape=(jax.ShapeDtypeStruct((m, N_HEADS), F32),
                   jax.ShapeDtypeStruct((m, N_HEADS), F32),
                   jax.ShapeDtypeStruct((N_HEADS, m), F32),
                   jax.ShapeDtypeStruct((m, LANES), BF16),
                   jax.ShapeDtypeStruct((m, LANES), BF16)),
        scratch_shapes=[pltpu.VMEM((1, LANES), F32), pltpu.VMEM((d, LANES), BF16)],
        compiler_params=_cparams(("arbitrary",)),
        name="logf_prompt",
    )(h, wt, b_row, l_mat, c2)


def _logf_sample_body(h_ref, wft_ref, brow_ref, bu_ref, x_ref, ms_ref, lf_ref, c_ref, r_ref, *, kb):
    lf = _log_sigmoid(_dot(h_ref[...], _forget_weight(wft_ref)) + brow_ref[...])
    lf_ref[...] = lf[:, :N_HEADS]
    lft = lf.T[:2 * SUBLANES, :]
    c_ref[...] = _dot_split_lhs(lft, bu_ref[...], 3)[:N_HEADS, :]
    n_blocks = x_ref.shape[1] // kb
    carry = jnp.zeros((x_ref.shape[0], 1), F32)
    for blk in range(n_blocks - 1, -1, -1):
        x = x_ref[:, blk * kb:(blk + 1) * kb]
        cum = _dot_split_lhs(x, ms_ref[...], 3)
        r_ref[:, blk * kb:(blk + 1) * kb] = cum + carry
        carry = carry + cum[:, 0:1] + x[:, 0:1]


def _logf_sample(h_s, wt, f_row0, b_row, past_logf_t, dec_seq):
    n_rows, d = h_s.shape
    n_bh, p_len = past_logf_t.shape
    kb = min(p_len, 512)
    r = lax.broadcasted_iota(jnp.int32, (n_rows, n_rows), 0)
    c = lax.broadcasted_iota(jnp.int32, (n_rows, n_rows), 1)
    bu = ((r // dec_seq == c // dec_seq) & (r <= c)).astype(BF16)
    ms = _tri(kb, "row_gt_col")
    whole = lambda shape: pl.BlockSpec(shape, lambda i: (0,) * len(shape))
    return pl.pallas_call(
        functools.partial(_logf_sample_body, kb=kb),
        grid=(1,),
        in_specs=[whole((n_rows, d)),
                  pl.BlockSpec((None, N_HEADS, d), lambda i: (0, f_row0 // N_HEADS, 0)),
                  whole((1, LANES)), whole((n_rows, n_rows)), whole((n_bh, p_len)), whole((kb, kb))],
        out_specs=(whole((n_rows, N_HEADS)), whole((N_HEADS, n_rows)), whole((n_bh, p_len))),
        out_shape=(jax.ShapeDtypeStruct((n_rows, N_HEADS), F32),
                   jax.ShapeDtypeStruct((N_HEADS, n_rows), F32),
                   jax.ShapeDtypeStruct((n_bh, p_len), F32)),
        compiler_params=_cparams(("arbitrary",), vmem_mib=32),
        name="logf_sample",
    )(h_s, wt, b_row, bu, past_logf_t, ms)


SB_TQ = 256


def _sb_block(q, k, v, m_mat, carry, mask):
    z = _dot_nt(q, k)
    lsn = jnp.minimum(-z, 0.0) - jnp.log(1.0 + jnp.exp2(-jnp.abs(z))) * LOG2E
    lsp = lsn + z
    if mask is not None:
        lsn = jnp.where(mask, lsn, 0.0)
    cum = _dot_split_lhs(lsn, m_mat, 2)
    if carry is not None:
        cum = cum + carry
    w = jnp.exp2(lsp + cum)
    if mask is not None:
        w = jnp.where(mask, w, 0.0)
    o = _dot(w.astype(BF16), v)
    new_carry = cum[:, 0:1] + lsn[:, 0:1]
    return o, new_carry


def _sb_fast_body(*refs):
    _sb_fast_step(pl.program_id(0), *refs)


def _sb_fast_step(i, q_ref, kd_ref, vd_ref, kp_ref, vp_ref, gs_ref, m_ref, a_ref, c_ref, after_head=None):
    tq = q_ref.shape[0]
    has_prev = i > 0
    row = lax.broadcasted_iota(jnp.int32, (tq, tq), 0)
    col = lax.broadcasted_iota(jnp.int32, (tq, tq), 1)
    dmask = col < row
    pmask = jnp.logical_and(has_prev, col >= 0)
    lane = lax.broadcasted_iota(jnp.int32, (tq, LANES), 1)
    m_mat = m_ref[...]
    cacc = jnp.full((tq, LANES), NEG_BIG, F32)
    for hh in range(N_HEADS):
        sl = slice(hh * HEAD_DIM, (hh + 1) * HEAD_DIM)
        q = q_ref[:, sl]
        od, cd = _sb_block(q, kd_ref[:, sl], vd_ref[:, sl], m_mat, None, dmask)
        op, cp = _sb_block(q, kp_ref[:, sl], vp_ref[:, sl], m_mat, cd, pmask)
        a_ref[:, sl] = ((od + op) * gs_ref[:, sl].astype(F32)).astype(BF16)
        cacc = jnp.where(lane == hh, cp, cacc)
        if after_head and hh in after_head:
            after_head[hh]()
    c_ref[...] = cacc


def _sb_rest_body(q_ref, k_ref, v_ref, gs_ref, cin_ref, afast_ref, m_ref, a_ref, o_s, c_s):
    i = pl.program_id(0)
    s = pl.program_id(1)
    j = i - 2 - s
    tq = q_ref.shape[0]

    @pl.when(s == 0)
    def _():
        o_s[...] = jnp.zeros_like(o_s)
        c_s[...] = cin_ref[...]

    active = jnp.logical_and(j >= 0, jnp.max(c_s[...]) > SB_SKIP_LOG2)

    @pl.when(active)
    def _():
        lane = lax.broadcasted_iota(jnp.int32, (tq, LANES), 1)
        m_mat = m_ref[...]
        c_all = c_s[...]
        cacc = c_all
        for hh in range(N_HEADS):
            sl = slice(hh * HEAD_DIM, (hh + 1) * HEAD_DIM)
            carry = jnp.sum(jnp.where(lane == hh, c_all, 0.0), axis=1, keepdims=True)
            o, cn = _sb_block(q_ref[:, sl], k_ref[:, sl], v_ref[:, sl], m_mat, carry, None)
            o_s[:, sl] += o
            cacc = jnp.where(lane == hh, cn, cacc)
        c_s[...] = cacc

    @pl.when(s == pl.num_programs(1) - 1)
    def _():
        a_ref[...] = (afast_ref[...].astype(F32) + o_s[...] * gs_ref[...].astype(F32)).astype(BF16)


def _sb_prompt(q, k, v, gs, fox_dec=None):
    t, w = q.shape
    tq = min(SB_TQ, t)
    nq = t // tq
    m_mat = _tri(tq, "row_gt_col")
    blk = lambda f: pl.BlockSpec((tq, w), f)
    sb_in = [blk(lambda i: (i, 0)), blk(lambda i: (i, 0)), blk(lambda i: (i, 0)),
             blk(lambda i: (jnp.maximum(i - 1, 0), 0)), blk(lambda i: (jnp.maximum(i - 1, 0), 0)),
             blk(lambda i: (i, 0)),
             pl.BlockSpec((tq, tq), lambda i: (0, 0))]
    sb_args = (q, k, v, k, v, gs, m_mat)
    sb_out_specs = (blk(lambda i: (i, 0)), pl.BlockSpec((tq, LANES), lambda i: (i, 0)))
    sb_out_shape = (jax.ShapeDtypeStruct((t, w), BF16), jax.ShapeDtypeStruct((t, LANES), F32))
    a_dec = None
    if fox_dec is not None:
        n_b, kb, nkb = _fox_dec_geometry(fox_dec[0], fox_dec[3])
        if n_b * nkb != nq:
            a_dec = _fox_decode(*fox_dec)
            fox_dec = None
    if fox_dec is None:
        a_fast, carry = pl.pallas_call(
            _sb_fast_body, grid=(nq,), in_specs=sb_in, out_specs=sb_out_specs, out_shape=sb_out_shape,
            compiler_params=_cparams(("parallel",), vmem_mib=48), name="sb_prompt_fast",
        )(*sb_args)
    else:
        dec_in, dec_out, dec_scratch = _fox_dec_specs(fox_dec[0].shape[1], kb,
                                                      lambda i: i // nkb, lambda i: i % nkb)
        n_sb, n_dec = len(sb_in), len(dec_in)

        def fused_body(*refs):
            i = pl.program_id(0)
            outs = refs[n_sb + n_dec:n_sb + n_dec + 3]
            begin, chunks, end = _fox_dec_parts(i % nkb, nkb, *refs[n_sb:n_sb + n_dec], outs[2],
                                                *refs[n_sb + n_dec + 3:])
            slots = {}
            for ci, fn in enumerate(chunks):
                slots.setdefault(((ci + 1) * N_HEADS - 1) // len(chunks), []).append(fn)
            after = {hh: (lambda fns=fns: [fn() for fn in fns]) for hh, fns in slots.items()}
            begin()
            _sb_fast_step(i, *refs[:n_sb], outs[0], outs[1], after_head=after)
            end()

        a_fast, carry, a_dec = pl.pallas_call(
            fused_body, grid=(nq,), in_specs=sb_in + dec_in,
            out_specs=sb_out_specs + (dec_out,),
            out_shape=sb_out_shape + (jax.ShapeDtypeStruct(fox_dec[0].shape, BF16),),
            scratch_shapes=dec_scratch,
            compiler_params=_cparams(("arbitrary",), vmem_mib=56), name="sb_prompt_fox_decode",
        )(*sb_args, *fox_dec)
    if nq <= 2:
        return a_fast, a_dec

    def rest(a_fast, carry):
        kidx = lambda i, s: (jnp.maximum(i - 2 - s, 0), 0)
        return pl.pallas_call(
            _sb_rest_body,
            grid=(nq, nq - 2),
            in_specs=[blk(lambda i, s: (i, 0)), blk(kidx), blk(kidx), blk(lambda i, s: (i, 0)),
                      pl.BlockSpec((tq, LANES), lambda i, s: (i, 0)),
                      blk(lambda i, s: (i, 0)),
                      pl.BlockSpec((tq, tq), lambda i, s: (0, 0))],
            out_specs=blk(lambda i, s: (i, 0)),
            out_shape=jax.ShapeDtypeStruct((t, w), BF16),
            scratch_shapes=[pltpu.VMEM((tq, w), F32), pltpu.VMEM((tq, LANES), F32)],
            compiler_params=_cparams(("parallel", "arbitrary"), vmem_mib=48),
            name="sb_prompt_rest",
        )(q, k, v, gs, carry, a_fast, m_mat)

    need_rest = jnp.max(carry[2 * tq:, :]) > SB_SKIP_LOG2
    return lax.cond(need_rest, rest, lambda a, c: a, a_fast, carry), a_dec


FOX_T = 512
FOX_WIN_TILES = (1, 2, 3, 4, 5, 6)
FOX_ZB_MAX = 40.0
FOX_SKIP_LOG = -104.0


def _head_column(blk, hh):
    lane8 = lax.broadcasted_iota(jnp.int32, blk.shape, 1)
    return jnp.sum(jnp.where(lane8 == hh, blk, 0.0), axis=1, keepdims=True)


BIAS_ONE_LANE = 3 * N_HEADS


def _split_bias_rows(f, lane):
    hi = f.astype(BF16).astype(F32)
    r1 = f - hi
    mid = r1.astype(BF16).astype(F32)
    low = (r1 - mid).astype(BF16).astype(F32)
    out = jnp.where(lane < N_HEADS, hi,
                    jnp.where(lane < 2 * N_HEADS, pltpu.roll(mid, N_HEADS, axis=1),
                              jnp.where(lane < BIAS_ONE_LANE, pltpu.roll(low, 2 * N_HEADS, axis=1),
                                        jnp.where(lane == BIAS_ONE_LANE, 1.0, 0.0))))
    return out.astype(BF16)


def _bias_selectors():
    src = lax.broadcasted_iota(jnp.int32, (N_HEADS, LANES, LANES), 1)
    dst = lax.broadcasted_iota(jnp.int32, (N_HEADS, LANES, LANES), 2)
    head = lax.broadcasted_iota(jnp.int32, (N_HEADS, LANES, LANES), 0)

    def sel(bias_base, one_base):
        term = dst - bias_base
        takes_bias = (term >= 0) & (term < 3) & (src == term * N_HEADS + head)
        takes_one = (dst >= one_base) & (dst < one_base + 3) & (src == BIAS_ONE_LANE)
        return (takes_bias | takes_one).astype(BF16)

    return jnp.stack([sel(0, 3), sel(3, 0)])


def _fox_fast_body(live_ref, q_ref, k_ref, v_ref, qb_ref, kb_ref, sel_ref, gs_ref, a_ref, kaug_s):
    hh = pl.program_id(0)
    qb = pl.program_id(1)
    nq = pl.num_programs(1)
    tb = q_ref.shape[0]
    t = k_ref.shape[0]

    @pl.when(qb == 0)
    def _():
        def build(c, carry):
            start = pl.multiple_of(c * tb, tb)
            kaug_s[pl.ds(start, tb), :] = _dot(kb_ref[pl.ds(start, tb), :], sel_ref[1]).astype(BF16)
            return carry
        lax.fori_loop(0, nq, build, 0)

    q_start = pl.multiple_of(qb * tb, tb)
    q_end = q_start + tb
    q2 = jnp.concatenate([q_ref[...], _dot(qb_ref[...], sel_ref[0]).astype(BF16)], axis=1)
    row = lax.broadcasted_iota(jnp.int32, (tb, 1), 0)
    live = live_ref[hh * nq + qb]

    def run(tiles):
        span = min(tiles * tb, t)
        lane_w = lax.broadcasted_iota(jnp.int32, (span, LANES), 1)
        ones_blk = jnp.where(lane_w == 0, 1.0, 0.0).astype(BF16)
        col = lax.broadcasted_iota(jnp.int32, (1, span), 1)

        def window(j, acc):
            upper = q_end - j * span
            start = pl.multiple_of(jnp.maximum(upper - span, 0), tb)
            k2 = jnp.concatenate([k_ref[pl.ds(start, span), :], kaug_s[pl.ds(start, span), :]], axis=1)
            p = jnp.exp2(_dot_nt(q2, k2))
            valid = jnp.logical_and(col - (q_start - start) <= row, col < upper - start)
            p = jnp.where(valid, p, 0.0).astype(BF16)
            v2 = jnp.concatenate([v_ref[pl.ds(start, span), :], ones_blk], axis=1)
            return acc + _dot(p, v2)

        n_win = (live + (tiles - 1)) // tiles
        acc = lax.fori_loop(0, n_win, window, jnp.zeros((tb, 2 * HEAD_DIM), F32))
        o = acc[:, :HEAD_DIM] / acc[:, HEAD_DIM:HEAD_DIM + 1]
        a_ref[...] = (o * gs_ref[...].astype(F32)).astype(BF16)

    bounds = (0,) + FOX_WIN_TILES
    for lo_t, hi_t in zip(bounds[:-1], bounds[1:]):
        last = hi_t == FOX_WIN_TILES[-1]
        cond = live > lo_t if last else jnp.logical_and(live > lo_t, live <= hi_t)
        pl.when(cond)(functools.partial(run, hi_t))


def _fox_slow_body(q_ref, k_ref, v_ref, fcol_ref, frow_ref, gs_ref, a_ref):
    hh = pl.program_id(0)
    qb = pl.program_id(1)
    tb = q_ref.shape[0]
    q = q_ref[...]
    fq = _head_column(fcol_ref[...], hh) * LOG2E

    def scores(kb):
        start = pl.multiple_of(kb * tb, tb)
        k = k_ref[pl.ds(start, tb), :]
        fk = frow_ref[0, :, pl.ds(start, tb)] * LOG2E
        return _dot_nt(q, k) + (fq - fk), v_ref[pl.ds(start, tb), :]

    def update(s, v, carry):
        m, l, acc = carry
        m_new = jnp.maximum(m, jnp.max(s, axis=1, keepdims=True))
        alpha = jnp.exp2(m - m_new)
        p = jnp.exp2(s - m_new)
        l = alpha * l + jnp.sum(p, axis=1, keepdims=True)
        acc = alpha * acc + _dot(p.astype(BF16), v)
        return m_new, l, acc

    def body(kb, carry):
        s, v = scores(kb)
        return update(s, v, carry)

    init = (jnp.full((tb, 1), NEG_BIG, F32), jnp.zeros((tb, 1), F32), jnp.zeros((tb, HEAD_DIM), F32))
    carry = lax.fori_loop(0, qb, body, init)
    s, v = scores(qb)
    row = lax.broadcasted_iota(jnp.int32, (tb, tb), 0)
    col = lax.broadcasted_iota(jnp.int32, (tb, tb), 1)
    s = jnp.where(col <= row, s, NEG_BIG)
    _, l, acc = update(s, v, carry)
    a_ref[...] = ((acc / l) * gs_ref[...].astype(F32)).astype(BF16)


def _fox_prompt(q, k, v, f_col, f_row, gs, zb, q_bias, k_bias):
    t, w = q.shape
    tb = min(FOX_T, t)
    nq = t // tb
    out_shape = jax.ShapeDtypeStruct((t, w), BF16)

    def fast(q, k, v, f_col, f_row, gs, q_bias, k_bias):
        f_start = f_row[:, ::tb]
        f_end = f_row[:, tb - 1::tb]
        dead = (f_start[:, :, None] - f_end[:, None, :]) < FOX_SKIP_LOG
        lo = jnp.sum(dead, axis=2).astype(jnp.int32)
        live = (jnp.arange(1, nq + 1, dtype=jnp.int32)[None, :] - lo).reshape(-1)
        head_blk = lambda h, i, live: (i, h)
        head_all = lambda h, i, live: (0, h)
        grid_spec = pltpu.PrefetchScalarGridSpec(
            num_scalar_prefetch=1,
            grid=(N_HEADS, nq),
            in_specs=[pl.BlockSpec((tb, HEAD_DIM), head_blk),
                      pl.BlockSpec((t, HEAD_DIM), head_all),
                      pl.BlockSpec((t, HEAD_DIM), head_all),
                      pl.BlockSpec((tb, LANES), lambda h, i, live: (i, 0)),
                      pl.BlockSpec((t, LANES), lambda h, i, live: (0, 0), pipeline_mode=pl.Buffered(1)),
                      pl.BlockSpec((2, None, LANES, LANES), lambda h, i, live: (0, h, 0, 0)),
                      pl.BlockSpec((tb, HEAD_DIM), head_blk)],
            out_specs=pl.BlockSpec((tb, HEAD_DIM), head_blk),
            scratch_shapes=[pltpu.VMEM((t, LANES), BF16)])
        return pl.pallas_call(
            _fox_fast_body, grid_spec=grid_spec, out_shape=out_shape,
            compiler_params=_cparams(("parallel", "arbitrary"), vmem_mib=48),
            name="fox_prompt_fast",
        )(live, q, k, v, q_bias, k_bias, _bias_selectors(), gs)

    def slow(q, k, v, f_col, f_row, gs, q_bias, k_bias):
        f_row3 = f_row.reshape(N_HEADS, 1, t)
        return pl.pallas_call(
            _fox_slow_body,
            grid=(N_HEADS, nq),
            in_specs=[pl.BlockSpec((tb, HEAD_DIM), lambda h, i: (i, h)),
                      pl.BlockSpec((t, HEAD_DIM), lambda h, i: (0, h)),
                      pl.BlockSpec((t, HEAD_DIM), lambda h, i: (0, h)),
                      pl.BlockSpec((tb, N_HEADS), lambda h, i: (i, 0)),
                      pl.BlockSpec((1, 1, t), lambda h, i: (h, 0, 0)),
                      pl.BlockSpec((tb, HEAD_DIM), lambda h, i: (i, h))],
            out_specs=pl.BlockSpec((tb, HEAD_DIM), lambda h, i: (i, h)),
            out_shape=out_shape,
            compiler_params=_cparams(("parallel", "parallel"), vmem_mib=48),
            name="fox_prompt_slow",
        )(q, k, v, f_col, f_row3, gs)

    return lax.cond(zb <= FOX_ZB_MAX, fast, slow, q, k, v, f_col, f_row, gs, q_bias, k_bias)


DEC_KB = 2048
SB_DEC_WIN = 256


def _own_head_mask():
    row = lax.broadcasted_iota(jnp.int32, (SUBLANES, W_ATT), 0)
    col = lax.broadcasted_iota(jnp.int32, (SUBLANES, W_ATT), 1)
    return (col // HEAD_DIM) == row


def _expand_q(q_ref, qx_s):
    n_tok = q_ref.shape[0]
    qf = q_ref[...].astype(F32)
    own = _own_head_mask()
    for tkn in range(n_tok):
        rep = jnp.broadcast_to(qf[tkn:tkn + 1, :], (SUBLANES, W_ATT))
        qx_s[tkn * N_HEADS:(tkn + 1) * N_HEADS, :] = jnp.where(own, rep, 0.0)


def _pad_new(kn_ref, vn_ref, knp_s, vnp_s):
    n_tok = kn_ref.shape[0]
    knp_s[...] = jnp.zeros_like(knp_s)
    vnp_s[...] = jnp.zeros_like(vnp_s)
    knp_s[0:n_tok, :] = kn_ref[...]
    vnp_s[0:n_tok, :] = vn_ref[...]


def _gather_heads(c_ref, dst_s):
    n_keys = dst_s.shape[0]
    for hh in range(N_HEADS):
        dst_s[:, hh * HEAD_DIM:(hh + 1) * HEAD_DIM] = (
            c_ref[0, pl.ds(hh, n_keys, stride=N_HEADS), :].astype(BF16))


def _collapse_heads(o_full, out_s):
    n_tok = o_full.shape[0] // N_HEADS
    own = _own_head_mask()
    for tkn in range(n_tok):
        blk = jnp.where(own, o_full[tkn * N_HEADS:(tkn + 1) * N_HEADS, :], 0.0)
        out_s[tkn:tkn + 1, :] = jnp.sum(blk, axis=0, keepdims=True)


def _sb_dec_fast_body(q_ref, kn_ref, vn_ref, kc_ref, vc_ref, gs_ref, mn_ref, mc_ref, a_ref, c_ref,
                      qx_s, knp_s, vnp_s, kx_s, vx_s, out_s):
    n_rows = qx_s.shape[0]
    _expand_q(q_ref, qx_s)
    _pad_new(kn_ref, vn_ref, knp_s, vnp_s)
    qx = qx_s[...].astype(BF16)
    row = lax.broadcasted_iota(jnp.int32, (n_rows, LANES), 0)
    col = lax.broadcasted_iota(jnp.int32, (n_rows, LANES), 1)
    mask = col < row // N_HEADS
    o_new, carry = _sb_block(qx, knp_s[...], vnp_s[...], mn_ref[...], None, mask)
    _gather_heads(kc_ref, kx_s)
    _gather_heads(vc_ref, vx_s)
    o_win, carry = _sb_block(qx, kx_s[...], vx_s[...], mc_ref[...], carry, None)
    _collapse_heads(o_new + o_win, out_s)
    a_ref[...] = (out_s[...] * gs_ref[...].astype(F32)).astype(BF16)
    c_ref[...] = jnp.broadcast_to(carry, c_ref.shape)


def _sb_dec_rest_body(q_ref, kc_ref, vc_ref, gs_ref, cin_ref, afast_ref, mc_ref, a_ref,
                      qx_s, kx_s, vx_s, acc_s, c_s, out_s):
    s = pl.program_id(1)

    @pl.when(s == 0)
    def _():
        _expand_q(q_ref, qx_s)
        acc_s[...] = jnp.zeros_like(acc_s)
        c_s[...] = cin_ref[:, 0:1]

    @pl.when(jnp.max(c_s[...]) > SB_SKIP_LOG2)
    def _():
        _gather_heads(kc_ref, kx_s)
        _gather_heads(vc_ref, vx_s)
        o, carry = _sb_block(qx_s[...].astype(BF16), kx_s[...], vx_s[...], mc_ref[...], c_s[...], None)
        acc_s[...] += o
        c_s[...] = carry

    @pl.when(s == pl.num_programs(1) - 1)
    def _():
        _collapse_heads(acc_s[...], out_s)
        a_ref[...] = (afast_ref[...].astype(F32) + out_s[...] * gs_ref[...].astype(F32)).astype(BF16)


def _sb_decode(q, k_new, v_new, k_cache, v_cache, gs, dec_seq):
    n_rows_all, w = q.shape
    n_b = n_rows_all // dec_seq
    p_len = k_cache.shape[1] // N_HEADS
    win = min(SB_DEC_WIN, p_len)
    n_win = p_len // win
    n_rows = dec_seq * N_HEADS
    m_new = _tri(LANES, "row_gt_col")
    m_win = _tri(win, "row_gt_col")
    tok = lambda b: (b, 0)
    newest = pl.BlockSpec((1, win * N_HEADS, HEAD_DIM), lambda b: (b, n_win - 1, 0))
    a_fast, carry = pl.pallas_call(
        _sb_dec_fast_body,
        grid=(n_b,),
        in_specs=[pl.BlockSpec((dec_seq, w), tok), pl.BlockSpec((dec_seq, w), tok),
                  pl.BlockSpec((dec_seq, w), tok),
                  newest, newest,
                  pl.BlockSpec((dec_seq, w), tok),
                  pl.BlockSpec((LANES, LANES), lambda b: (0, 0)),
                  pl.BlockSpec((win, win), lambda b: (0, 0))],
        out_specs=(pl.BlockSpec((dec_seq, w), tok), pl.BlockSpec((n_rows, LANES), tok)),
        out_shape=(jax.ShapeDtypeStruct((n_rows_all, w), BF16),
                   jax.ShapeDtypeStruct((n_b * n_rows, LANES), F32)),
        scratch_shapes=[pltpu.VMEM((n_rows, w), F32), pltpu.VMEM((LANES, w), BF16),
                        pltpu.VMEM((LANES, w), BF16), pltpu.VMEM((win, w), BF16),
                        pltpu.VMEM((win, w), BF16), pltpu.VMEM((dec_seq, w), F32)],
        compiler_params=_cparams(("parallel",), vmem_mib=48),
        name="sb_decode_fast",
    )(q, k_new, v_new, k_cache, v_cache, gs, m_new, m_win)
    if n_win <= 1:
        return a_fast

    def rest(a_fast, carry):
        tok2 = lambda b, s: (b, 0)
        older = pl.BlockSpec((1, win * N_HEADS, HEAD_DIM), lambda b, s: (b, n_win - 2 - s, 0))
        return pl.pallas_call(
            _sb_dec_rest_body,
            grid=(n_b, n_win - 1),
            in_specs=[pl.BlockSpec((dec_seq, w), tok2),
                      older, older,
                      pl.BlockSpec((dec_seq, w), tok2),
                      pl.BlockSpec((n_rows, LANES), tok2),
                      pl.BlockSpec((dec_seq, w), tok2),
                      pl.BlockSpec((win, win), lambda b, s: (0, 0))],
            out_specs=pl.BlockSpec((dec_seq, w), tok2),
            out_shape=jax.ShapeDtypeStruct((n_rows_all, w), BF16),
            scratch_shapes=[pltpu.VMEM((n_rows, w), F32), pltpu.VMEM((win, w), BF16),
                            pltpu.VMEM((win, w), BF16), pltpu.VMEM((n_rows, w), F32),
                            pltpu.VMEM((n_rows, 1), F32), pltpu.VMEM((dec_seq, w), F32)],
            compiler_params=_cparams(("parallel", "arbitrary"), vmem_mib=48),
            name="sb_decode_rest",
        )(q, k_cache, v_cache, gs, carry, a_fast, m_win)

    need_rest = jnp.max(carry) > SB_SKIP_LOG2
    return lax.cond(need_rest, rest, lambda a, c: a, a_fast, carry)


def _fox_dec_body(*refs):
    _fox_dec_step(pl.program_id(1), pl.num_programs(1), *refs)


FOX_DEC_CHUNK = 1024


def _fox_dec_step(s, n_s, *refs):
    begin, chunks, end = _fox_dec_parts(s, n_s, *refs)
    begin()
    for chunk_fn in chunks:
        chunk_fn()
    end()


def _fox_dec_parts(s, n_s, q_ref, kn_ref, vn_ref, kc_ref, vc_ref, gs_ref, c_ref, r_ref, a_ref,
                   acc_s, m_s, l_s):
    n_rows = q_ref.shape[1]
    q = q_ref[0]
    own = lax.broadcasted_iota(jnp.int32, (SUBLANES, LANES), 0) == \
        lax.broadcasted_iota(jnp.int32, (SUBLANES, LANES), 1) % N_HEADS
    head_bias = jnp.where(own, 0.0, NEG_BIG)

    def update(sc, v):
        m = m_s[...]
        m_new = jnp.maximum(m, jnp.max(sc, axis=1, keepdims=True))
        alpha = jnp.exp2(m - m_new)
        p = jnp.exp2(sc - m_new)
        l_s[...] = alpha * l_s[...] + jnp.sum(p, axis=1, keepdims=True)
        acc_s[...] = alpha * acc_s[...] + _dot(p.astype(BF16), v)
        m_s[...] = m_new

    def add_key_bias(z, decay_row):
        n = decay_row.shape[1]
        tile = jnp.concatenate([head_bias] * (n // LANES), axis=1) + decay_row * LOG2E
        return (z.reshape(n_rows // SUBLANES, SUBLANES, n) + tile[None]).reshape(n_rows, n)

    def begin():
        @pl.when(s == 0)
        def _():
            m_s[...] = jnp.full_like(m_s, NEG_BIG)
            l_s[...] = jnp.zeros_like(l_s)
            acc_s[...] = jnp.zeros_like(acc_s)
            row = lax.broadcasted_iota(jnp.int32, (n_rows, LANES), 0)
            col = lax.broadcasted_iota(jnp.int32, (n_rows, LANES), 1)
            sc = add_key_bias(_dot_nt(q, kn_ref[0]), -c_ref[0])
            sc = jnp.where(col // N_HEADS <= row // N_HEADS, sc, NEG_BIG)
            update(sc, vn_ref[0])

    n_keys_rows = kc_ref.shape[1]
    chunk = min(FOX_DEC_CHUNK * N_HEADS, n_keys_rows)

    def one_chunk(c0):
        k2 = kc_ref[0, c0:c0 + chunk, :].astype(BF16)
        v2 = vc_ref[0, c0:c0 + chunk, :].astype(BF16)
        update(add_key_bias(_dot_nt(q, k2), r_ref[0, :, c0:c0 + chunk]), v2)

    def end():
        @pl.when(s == n_s - 1)
        def _():
            a_ref[0] = ((acc_s[...] / l_s[...]) * gs_ref[0].astype(F32)).astype(BF16)

    return begin, [functools.partial(one_chunk, c0) for c0 in range(0, n_keys_rows, chunk)], end


def _fox_dec_geometry(q3, k_cache):
    n_b = q3.shape[0]
    p_len = k_cache.shape[1] // N_HEADS
    kb = min(DEC_KB, p_len)
    return n_b, kb, p_len // kb


def _fox_dec_specs(n_rows, kb, stream, block):
    tok = pl.BlockSpec((1, n_rows, HEAD_DIM), lambda *g: (stream(*g), 0, 0))
    cache = pl.BlockSpec((1, kb * N_HEADS, HEAD_DIM), lambda *g: (stream(*g), block(*g), 0))
    in_specs = [tok, tok, tok, cache, cache, tok,
                pl.BlockSpec((1, 1, LANES), lambda *g: (stream(*g), 0, 0)),
                pl.BlockSpec((1, 1, kb * N_HEADS), lambda *g: (stream(*g), 0, block(*g)))]
    scratch = [pltpu.VMEM((n_rows, HEAD_DIM), F32), pltpu.VMEM((n_rows, 1), F32),
               pltpu.VMEM((n_rows, 1), F32)]
    return in_specs, tok, scratch


def _fox_decode(q3, kn3, vn3, k_cache, v_cache, gs3, c_row, r_row):
    n_b, kb, nkb = _fox_dec_geometry(q3, k_cache)
    in_specs, out_spec, scratch = _fox_dec_specs(q3.shape[1], kb, lambda b, s: b, lambda b, s: s)
    return pl.pallas_call(
        _fox_dec_body,
        grid=(n_b, nkb),
        in_specs=in_specs,
        out_specs=out_spec,
        out_shape=jax.ShapeDtypeStruct(q3.shape, BF16),
        scratch_shapes=scratch,
        compiler_params=_cparams(("parallel", "arbitrary"), vmem_mib=56),
        name="fox_decode",
    )(q3, kn3, vn3, k_cache, v_cache, gs3, c_row, r_row)


def _post_body(x_ref, asb_ref, afx_ref, msb_ref, mfx_ref, wsb_ref, wfx_ref, wo_ref, y_ref):
    u_sb = _dot(asb_ref[...], wsb_ref[...])
    u_fx = _dot(afx_ref[...], wfx_ref[...])
    merged = msb_ref[...].astype(F32) * u_sb + mfx_ref[...].astype(F32) * u_fx
    y_ref[...] = x_ref[...] + _dot(merged.astype(BF16), wo_ref[...])


def _post(x2d, a_sb, a_fx, m_sig, w_sb, w_fx, w_o):
    m, d = x2d.shape
    w = a_sb.shape[1]
    tm = min(m, 512)
    row = lambda i: (i, 0)
    const = lambda i: (0, 0)
    resident = functools.partial(pl.BlockSpec, index_map=const, pipeline_mode=pl.Buffered(1))
    return pl.pallas_call(
        _post_body,
        grid=(m // tm,),
        in_specs=[pl.BlockSpec((tm, d), row), pl.BlockSpec((tm, w), row), pl.BlockSpec((tm, w), row),
                  pl.BlockSpec((tm, d), lambda i: (i, 0)), pl.BlockSpec((tm, d), lambda i: (i, 1)),
                  resident((w, d)), resident((w, d)), resident((d, d))],
        out_specs=pl.BlockSpec((tm, d), row),
        out_shape=jax.ShapeDtypeStruct((m, d), F32),
        compiler_params=_cparams(("parallel",), vmem_mib=56),
        name="post",
    )(x2d, a_sb, a_fx, m_sig, m_sig, w_sb, w_fx, w_o)


def _project_all(hp, hs, w, qnw, knw):
    seg = lambda c: c * W_ATT
    q_sb = _proj(hp, hs, w, seg(0), "scale")
    k_sb = _proj(hp, hs, w, seg(1), "kv")
    v_sb = _proj(hp, hs, w, seg(2), "kv")
    g_sb = _proj(hp, hs, w, seg(3), "silu")
    q_fx = _proj(hp, hs, w, seg(4), "qnorm", nw=qnw)
    k_fx = _proj(hp, hs, w, seg(5), "knorm", nw=knw)
    v_fx = _proj(hp, hs, w, seg(6), "kv")
    g_fx = _proj(hp, hs, w, seg(7), "silu")
    m_sig = _proj(hp, hs, w, seg(8) + N_HEADS, "sigmoid", n_tiles=2 * D_MODEL // W_ATT)
    calls = (q_sb, k_sb, v_sb, g_sb, q_fx, k_fx, v_fx, g_fx, m_sig)
    prompt = tuple(o for r in calls for o in r[:len(r) // 2])
    decode = tuple(o for r in calls for o in r[len(r) // 2:])
    return prompt, decode


def kernel(x_prompt, x_sample, cache_sb_k, cache_sb_v, cache_fox_k, cache_fox_v, cache_fox_logf,
           norm_w, w_in, b_forget, q_norm_w, k_norm_w, w_branch_sb, w_branch_fox, w_out):
    depth = norm_w.shape[0]
    assert depth == 1, "single-layer step"
    bsz, seq, d = x_prompt.shape
    assert bsz == 1
    n_dec, dec_seq, _ = x_sample.shape
    p_len = cache_sb_k.shape[2]
    n_main = 8 * W_ATT

    w_t = jnp.swapaxes(w_in, 1, 2)
    b_row = jnp.pad(b_forget[0].astype(F32)[None, :], ((0, 0), (0, LANES - N_HEADS)))
    qnw = jnp.tile(q_norm_w[0].astype(F32), N_HEADS)[None, :]
    knw = jnp.tile(k_norm_w[0].astype(F32), N_HEADS)[None, :]
    nw_row = norm_w[0].astype(F32)[None, :]
    w_sb = w_branch_sb[0].astype(BF16)
    w_fx = w_branch_fox[0].astype(BF16)
    w_o = w_out[0].astype(BF16)

    xp = x_prompt.reshape(seq, d)
    xs = x_sample.reshape(n_dec * dec_seq, d)
    hp = _rmsnorm(xp, nw_row)
    hs = _rmsnorm(xs, nw_row)
    prompt_proj, decode_proj = _project_all(hp, hs, w_t, qnw, knw)

    (q_sb, k_sb, k_sb_b, v_sb, v_sb_b, g_sb, q_fx, k_fx, k_fx_b, v_fx, v_fx_b, g_fx,
     m_sig) = prompt_proj
    (sq_sb, sk_sb, sk_sb_b, sv_sb, sv_sb_b, sg_sb, sq_fx, sk_fx, sk_fx_b, sv_fx, sv_fx_b, sg_fx,
     sm_sig) = decode_proj

    past_logf_t = jnp.transpose(cache_fox_logf[0].astype(F32), (0, 2, 1)).reshape(n_dec * N_HEADS, p_len)
    s_logf, c_all, r_past = _logf_sample(hs, w_t, n_main, b_row, past_logf_t, dec_seq)
    n_th = dec_seq * N_HEADS
    c_row = jnp.transpose(c_all.reshape(N_HEADS, n_dec, dec_seq), (1, 2, 0)).reshape(n_dec, 1, n_th)
    r_row = jnp.transpose(r_past.reshape(n_dec, N_HEADS, p_len), (0, 2, 1)).reshape(n_dec, 1, p_len * N_HEADS)
    by_head = lambda a: a.reshape(n_dec, n_th, HEAD_DIM)
    kc_sb = cache_sb_k[0].reshape(n_dec, p_len * N_HEADS, HEAD_DIM)
    vc_sb = cache_sb_v[0].reshape(n_dec, p_len * N_HEADS, HEAD_DIM)
    kc_fx = cache_fox_k[0].reshape(n_dec, p_len * N_HEADS, HEAD_DIM)
    vc_fx = cache_fox_v[0].reshape(n_dec, p_len * N_HEADS, HEAD_DIM)

    zb = (1.02 * HEAD_DIM * QK_SCALE) * jnp.max(jnp.abs(q_norm_w[0])) * jnp.max(jnp.abs(k_norm_w[0]))
    zb = zb.astype(F32)
    logf, f_col, f_row, q_bias, k_bias = _logf_prompt(hp, w_t, n_main, b_row, (zb * LOG2E).reshape(1, 1))
    a_sb, sa_fx = _sb_prompt(q_sb, k_sb_b, v_sb_b, g_sb,
                             fox_dec=(by_head(sq_fx), by_head(sk_fx_b), by_head(sv_fx_b), kc_fx, vc_fx,
                                      by_head(sg_fx), c_row, r_row))
    sa_fx = sa_fx.reshape(n_dec * dec_seq, W_ATT)
    a_fx = _fox_prompt(q_fx, k_fx_b, v_fx_b, f_col, f_row, g_fx, zb, q_bias, k_bias)
    sa_sb = _sb_decode(sq_sb, sk_sb_b, sv_sb_b, kc_sb, vc_sb, sg_sb, dec_seq)
    y_prompt = _post(xp, a_sb, a_fx, m_sig, w_sb, w_fx, w_o).reshape(bsz, seq, d)
    y_sample = _post(xs, sa_sb, sa_fx, sm_sig, w_sb, w_fx, w_o).reshape(n_dec, dec_seq, d)

    hd = (N_HEADS, HEAD_DIM)
    return (y_prompt, y_sample,
            k_sb.reshape(1, bsz, seq, *hd), v_sb.reshape(1, bsz, seq, *hd),
            k_fx.reshape(1, bsz, seq, *hd), v_fx.reshape(1, bsz, seq, *hd),
            logf.reshape(1, bsz, seq, N_HEADS),
            sk_sb.reshape(1, n_dec, dec_seq, *hd), sv_sb.reshape(1, n_dec, dec_seq, *hd),
            sk_fx.reshape(1, n_dec, dec_seq, *hd), sv_fx.reshape(1, n_dec, dec_seq, *hd),
            s_logf.reshape(1, n_dec, dec_seq, N_HEADS))
```

```python
import functools

import jax
import jax.numpy as jnp
from jax import lax
from jax.experimental import pallas as pl
from jax.experimental.pallas import tpu as pltpu

F32 = jnp.float32
BF16 = jnp.bfloat16

D_MODEL = 2048
N_HEADS = 8
HEAD_DIM = 128
W_ATT = N_HEADS * HEAD_DIM
RMS_EPS = 1e-6
QK_SCALE = HEAD_DIM ** -0.5
LOG2E = 1.4426950408889634
LANES = 128
SUBLANES = 8
SB_SKIP_LOG2 = -150.0
NEG_BIG = -1e30
MIB = 1024 * 1024


def _cparams(sem, vmem_mib=None):
    kw = dict(dimension_semantics=sem)
    if vmem_mib is not None:
        kw["vmem_limit_bytes"] = vmem_mib * MIB
    return pltpu.CompilerParams(**kw)


def _softplus_neg_abs(z):
    return jnp.log1p(jnp.exp(-jnp.abs(z)))


def _log_sigmoid(z):
    return jnp.minimum(z, 0.0) - _softplus_neg_abs(z)


def _split_bf16(x, n):
    parts = []
    r = x
    for _ in range(n - 1):
        p = r.astype(BF16)
        parts.append(p)
        r = r - p.astype(F32)
    parts.append(r.astype(BF16))
    return parts


def _dot(a, b):
    return jnp.dot(a, b, preferred_element_type=F32)


def _dot_nt(a, b):
    return lax.dot_general(a, b, (((1,), (1,)), ((), ())), preferred_element_type=F32)


def _dot_split_lhs(x, m, n):
    acc = None
    for p in _split_bf16(x, n):
        t = _dot(p, m)
        acc = t if acc is None else acc + t
    return acc


def _dot_split_rhs(m, x, n):
    acc = None
    for p in _split_bf16(x, n):
        t = _dot(m, p)
        acc = t if acc is None else acc + t
    return acc


def _tri(n, kind):
    r = lax.broadcasted_iota(jnp.int32, (n, n), 0)
    c = lax.broadcasted_iota(jnp.int32, (n, n), 1)
    if kind == "row_gt_col":
        m = r > c
    elif kind == "row_le_col":
        m = r <= c
    elif kind == "row_ge_col":
        m = r >= c
    else:
        raise ValueError(kind)
    return m.astype(BF16)


def _rmsnorm_body(x_ref, w_ref, o_ref):
    x = x_ref[...]
    ms = jnp.mean(x * x, axis=-1, keepdims=True)
    o_ref[...] = (x * lax.rsqrt(ms + RMS_EPS) * w_ref[...]).astype(o_ref.dtype)


def _rmsnorm(x2d, w_row):
    m, d = x2d.shape
    tm = min(m, 512)
    return pl.pallas_call(
        _rmsnorm_body,
        grid=(m // tm,),
        in_specs=[pl.BlockSpec((tm, d), lambda i: (i, 0)),
                  pl.BlockSpec((1, d), lambda i: (0, 0))],
        out_specs=pl.BlockSpec((tm, d), lambda i: (i, 0)),
        out_shape=jax.ShapeDtypeStruct((m, d), BF16),
        compiler_params=_cparams(("parallel",)),
        name="rmsnorm",
    )(x2d, w_row)


def _head_rmsnorm(acc, nw):
    parts = []
    for hh in range(N_HEADS):
        a = acc[:, hh * HEAD_DIM:(hh + 1) * HEAD_DIM]
        ms = jnp.mean(a * a, axis=-1, keepdims=True)
        parts.append(a * lax.rsqrt(ms + RMS_EPS))
    return jnp.concatenate(parts, axis=1) * nw


def _proj_epilogue(acc, kind, nw_ref, outs):
    if kind == "scale":
        outs[0][...] = (acc * (QK_SCALE * LOG2E)).astype(BF16)
    elif kind == "kv":
        outs[0][...] = acc
        outs[1][...] = acc.astype(BF16)
    elif kind == "silu":
        outs[0][...] = (acc * jax.nn.sigmoid(acc)).astype(BF16)
    elif kind == "sigmoid":
        outs[0][...] = jax.nn.sigmoid(acc).astype(BF16)
    elif kind == "qnorm":
        outs[0][...] = (_head_rmsnorm(acc, nw_ref[...]) * (QK_SCALE * LOG2E)).astype(BF16)
    elif kind == "knorm":
        y = _head_rmsnorm(acc, nw_ref[...])
        outs[0][...] = y
        outs[1][...] = y.astype(BF16)
    else:
        raise ValueError(kind)


def _proj_body(*refs, kind, n_out, shift):
    hp_ref, hs_ref, w_ref = refs[:3]
    pos = 3
    wn_ref = nw_ref = None
    if shift:
        wn_ref = refs[pos]
        pos += 1
    if kind in ("qnorm", "knorm"):
        nw_ref = refs[pos]
        pos += 1
    outs_p = refs[pos:pos + n_out]
    outs_s = refs[pos + n_out:pos + 2 * n_out]
    wb_s = refs[-1]
    i = pl.program_id(1)
    n_prompt = pl.num_programs(1) - 1

    @pl.when(i == 0)
    def _():
        if shift:
            wt = jnp.concatenate([w_ref[shift:, :], wn_ref[...]], axis=0)
        else:
            wt = w_ref[...]
        wb_s[...] = wt.T.astype(BF16)

    @pl.when(i < n_prompt)
    def _():
        _proj_epilogue(_dot(hp_ref[...], wb_s[...]), kind, nw_ref, outs_p)

    @pl.when(i == n_prompt)
    def _():
        _proj_epilogue(_dot(hs_ref[...], wb_s[...]), kind, nw_ref, outs_s)


def _proj(hp, hs, wt, col0, kind, nw=None, n_tiles=1):
    mp, d = hp.shape
    ms = hs.shape[0]
    tn = W_ATT
    dtypes = (F32, BF16) if kind in ("kv", "knorm") else (BF16,)
    tm = min(mp, 1024)
    n_prompt = mp // tm
    blk0 = col0 // tn
    shift = col0 - blk0 * tn
    assert shift in (0, SUBLANES)
    grid = (n_tiles, n_prompt + 1)
    prow = lambda j, i: (jnp.minimum(i, n_prompt - 1), 0)
    w_mode = {} if n_tiles > 1 else dict(pipeline_mode=pl.Buffered(1))
    in_specs = [pl.BlockSpec((tm, d), prow),
                pl.BlockSpec((ms, d), lambda j, i: (0, 0)),
                pl.BlockSpec((None, tn, d), lambda j, i: (0, blk0 + j, 0), **w_mode)]
    args = [hp, hs, wt]
    if shift:
        per = tn // shift
        in_specs.append(pl.BlockSpec((None, shift, d), lambda j, i: (0, (blk0 + j + 1) * per, 0)))
        args.append(wt)
    if kind in ("qnorm", "knorm"):
        in_specs.append(pl.BlockSpec((1, tn), lambda j, i: (0, 0)))
        args.append(nw)
    n_cols = tn * n_tiles
    p_block = pl.BlockSpec((tm, tn), lambda j, i: (jnp.minimum(i, n_prompt - 1), j))
    s_block = pl.BlockSpec((ms, tn), lambda j, i: (0, j))
    out_shape = tuple(jax.ShapeDtypeStruct((mp, n_cols), t) for t in dtypes) + \
        tuple(jax.ShapeDtypeStruct((ms, n_cols), t) for t in dtypes)
    out_specs = (p_block,) * len(dtypes) + (s_block,) * len(dtypes)
    return pl.pallas_call(
        functools.partial(_proj_body, kind=kind, n_out=len(dtypes), shift=shift),
        grid=grid,
        in_specs=in_specs,
        out_specs=out_specs,
        out_shape=out_shape,
        scratch_shapes=[pltpu.VMEM((d, tn), BF16)],
        compiler_params=_cparams(("parallel", "arbitrary"), vmem_mib=56),
        name="proj_" + kind,
    )(*args)


def _forget_weight(wft_ref):
    wft = wft_ref[...]
    pad = jnp.zeros((LANES - wft.shape[0], wft.shape[1]), F32)
    return jnp.concatenate([wft, pad], axis=0).T.astype(BF16)


def _norm_logf_body(x_ref, nw_ref, wft_ref, brow_ref, l_ref, c2_ref,
                    h_ref, lf_ref, fcol_ref, frow_ref, qbias_ref, kbias_ref, ccol_s, wf_s):
    i = pl.program_id(0)
    tm = x_ref.shape[0]

    @pl.when(i == 0)
    def _():
        ccol_s[...] = jnp.zeros_like(ccol_s)
        wf_s[...] = _forget_weight(wft_ref)

    x = x_ref[...]
    ms = jnp.mean(x * x, axis=-1, keepdims=True)
    h = (x * lax.rsqrt(ms + RMS_EPS) * nw_ref[...]).astype(BF16)
    h_ref[...] = h
    lf = _log_sigmoid(_dot(h, wf_s[...]) + brow_ref[...])
    lf_ref[...] = lf[:, :N_HEADS]
    f_col = _dot_split_rhs(l_ref[...], lf, 3) + ccol_s[...]
    fcol_ref[...] = f_col[:, :N_HEADS]
    ccol_s[...] = f_col[tm - 1:tm, :]
    frow_ref[...] = f_col.T[:N_HEADS, :]
    lane = lax.broadcasted_iota(jnp.int32, f_col.shape, 1)
    f2 = f_col * LOG2E
    qbias_ref[...] = _split_bias_rows(f2 - c2_ref[...], lane)
    kbias_ref[...] = _split_bias_rows(-f2, lane)


def _norm_logf_prompt(x2d, nw_row, wt, f_row0, b_row, c2):
    m, d = x2d.shape
    tm = min(m, 512)
    l_mat = _tri(tm, "row_ge_col")
    const = lambda i: (0, 0)
    return pl.pallas_call(
        _norm_logf_body,
        grid=(m // tm,),
        in_specs=[pl.BlockSpec((tm, d), lambda i: (i, 0)),
                  pl.BlockSpec((1, d), const),
                  pl.BlockSpec((None, N_HEADS, d), lambda i: (0, f_row0 // N_HEADS, 0)),
                  pl.BlockSpec((1, LANES), const),
                  pl.BlockSpec((tm, tm), const),
                  pl.BlockSpec((1, 1), const)],
        out_specs=(pl.BlockSpec((tm, d), lambda i: (i, 0)),
                   pl.BlockSpec((tm, N_HEADS), lambda i: (i, 0)),
                   pl.BlockSpec((tm, N_HEADS), lambda i: (i, 0)),
                   pl.BlockSpec((N_HEADS, tm), lambda i: (0, i)),
                   pl.BlockSpec((tm, LANES), lambda i: (i, 0)),
                   pl.BlockSpec((tm, LANES), lambda i: (i, 0))),
        out_shape=(jax.ShapeDtypeStruct((m, d), BF16),
                   jax.ShapeDtypeStruct((m, N_HEADS), F32),
                   jax.ShapeDtypeStruct((m, N_HEADS), F32),
                   jax.ShapeDtypeStruct((N_HEADS, m), F32),
                   jax.ShapeDtypeStruct((m, LANES), BF16),
                   jax.ShapeDtypeStruct((m, LANES), BF16)),
        scratch_shapes=[pltpu.VMEM((1, LANES), F32), pltpu.VMEM((d, LANES), BF16)],
        compiler_params=_cparams(("arbitrary",)),
        name="norm_logf_prompt",
    )(x2d, nw_row, wt, b_row, l_mat, c2)


def _logf_sample_body(h_ref, wft_ref, brow_ref, bu_ref, x_ref, ms_ref, lf_ref, c_ref, r_ref, *, kb):
    lf = _log_sigmoid(_dot(h_ref[...], _forget_weight(wft_ref)) + brow_ref[...])
    lf_ref[...] = lf[:, :N_HEADS]
    lft = lf.T[:2 * SUBLANES, :]
    c_ref[...] = _dot_split_lhs(lft, bu_ref[...], 3)[:N_HEADS, :]
    n_blocks = x_ref.shape[1] // kb
    carry = jnp.zeros((x_ref.shape[0], 1), F32)
    for blk in range(n_blocks - 1, -1, -1):
        x = x_ref[:, blk * kb:(blk + 1) * kb]
        cum = _dot_split_lhs(x, ms_ref[...], 3)
        r_ref[:, blk * kb:(blk + 1) * kb] = cum + carry
        carry = carry + cum[:, 0:1] + x[:, 0:1]


def _logf_sample(h_s, wt, f_row0, b_row, past_logf_t, dec_seq):
    n_rows, d = h_s.shape
    n_bh, p_len = past_logf_t.shape
    kb = min(p_len, 512)
    r = lax.broadcasted_iota(jnp.int32, (n_rows, n_rows), 0)
    c = lax.broadcasted_iota(jnp.int32, (n_rows, n_rows), 1)
    bu = ((r // dec_seq == c // dec_seq) & (r <= c)).astype(BF16)
    ms = _tri(kb, "row_gt_col")
    whole = lambda shape: pl.BlockSpec(shape, lambda i: (0,) * len(shape))
    return pl.pallas_call(
        functools.partial(_logf_sample_body, kb=kb),
        grid=(1,),
        in_specs=[whole((n_rows, d)),
                  pl.BlockSpec((None, N_HEADS, d), lambda i: (0, f_row0 // N_HEADS, 0)),
                  whole((1, LANES)), whole((n_rows, n_rows)), whole((n_bh, p_len)), whole((kb, kb))],
        out_specs=(whole((n_rows, N_HEADS)), whole((N_HEADS, n_rows)), whole((n_bh, p_len))),
        out_shape=(jax.ShapeDtypeStruct((n_rows, N_HEADS), F32),
                   jax.ShapeDtypeStruct((N_HEADS, n_rows), F32),
                   jax.ShapeDtypeStruct((n_bh, p_len), F32)),
        compiler_params=_cparams(("arbitrary",), vmem_mib=32),
        name="logf_sample",
    )(h_s, wt, b_row, bu, past_logf_t, ms)


SB_TQ = 256


def _sb_block(q, k, v, m_mat, carry, mask):
    z = _dot_nt(q, k)
    lsn = jnp.minimum(-z, 0.0) - jnp.log(1.0 + jnp.exp2(-jnp.abs(z))) * LOG2E
    lsp = lsn + z
    if mask is not None:
        lsn = jnp.where(mask, lsn, 0.0)
    cum = _dot_split_lhs(lsn, m_mat, 2)
    if carry is not None:
        cum = cum + carry
    w = jnp.exp2(lsp + cum)
    if mask is not None:
        w = jnp.where(mask, w, 0.0)
    o = _dot(w.astype(BF16), v)
    new_carry = cum[:, 0:1] + lsn[:, 0:1]
    return o, new_carry


def _sb_fast_body(*refs):
    _sb_fast_step(pl.program_id(0), *refs)


def _sb_fast_step(i, q_ref, kd_ref, vd_ref, kp_ref, vp_ref, gs_ref, m_ref, a_ref, c_ref, after_head=None):
    tq = q_ref.shape[0]
    has_prev = i > 0
    row = lax.broadcasted_iota(jnp.int32, (tq, tq), 0)
    col = lax.broadcasted_iota(jnp.int32, (tq, tq), 1)
    dmask = col < row
    pmask = jnp.logical_and(has_prev, col >= 0)
    lane = lax.broadcasted_iota(jnp.int32, (tq, LANES), 1)
    m_mat = m_ref[...]
    cacc = jnp.full((tq, LANES), NEG_BIG, F32)
    for hh in range(N_HEADS):
        sl = slice(hh * HEAD_DIM, (hh + 1) * HEAD_DIM)
        q = q_ref[:, sl]
        od, cd = _sb_block(q, kd_ref[:, sl], vd_ref[:, sl], m_mat, None, dmask)
        op, cp = _sb_block(q, kp_ref[:, sl], vp_ref[:, sl], m_mat, cd, pmask)
        a_ref[:, sl] = ((od + op) * gs_ref[:, sl].astype(F32)).astype(BF16)
        cacc = jnp.where(lane == hh, cp, cacc)
        if after_head and hh in after_head:
            after_head[hh]()
    c_ref[...] = cacc


def _sb_rest_body(q_ref, k_ref, v_ref, gs_ref, cin_ref, afast_ref, m_ref, a_ref, o_s, c_s):
    i = pl.program_id(0)
    s = pl.program_id(1)
    j = i - 2 - s
    tq = q_ref.shape[0]

    @pl.when(s == 0)
    def _():
        o_s[...] = jnp.zeros_like(o_s)
        c_s[...] = cin_ref[...]

    active = jnp.logical_and(j >= 0, jnp.max(c_s[...]) > SB_SKIP_LOG2)

    @pl.when(active)
    def _():
        lane = lax.broadcasted_iota(jnp.int32, (tq, LANES), 1)
        m_mat = m_ref[...]
        c_all = c_s[...]
        cacc = c_all
        for hh in range(N_HEADS):
            sl = slice(hh * HEAD_DIM, (hh + 1) * HEAD_DIM)
            carry = jnp.sum(jnp.where(lane == hh, c_all, 0.0), axis=1, keepdims=True)
            o, cn = _sb_block(q_ref[:, sl], k_ref[:, sl], v_ref[:, sl], m_mat, carry, None)
            o_s[:, sl] += o
            cacc = jnp.where(lane == hh, cn, cacc)
        c_s[...] = cacc

    @pl.when(s == pl.num_programs(1) - 1)
    def _():
        a_ref[...] = (afast_ref[...].astype(F32) + o_s[...] * gs_ref[...].astype(F32)).astype(BF16)


def _sb_prompt(q, k, v, gs, fox_dec=None):
    t, w = q.shape
    tq = min(SB_TQ, t)
    nq = t // tq
    m_mat = _tri(tq, "row_gt_col")
    blk = lambda f: pl.BlockSpec((tq, w), f)
    sb_in = [blk(lambda i: (i, 0)), blk(lambda i: (i, 0)), blk(lambda i: (i, 0)),
             blk(lambda i: (jnp.maximum(i - 1, 0), 0)), blk(lambda i: (jnp.maximum(i - 1, 0), 0)),
             blk(lambda i: (i, 0)),
             pl.BlockSpec((tq, tq), lambda i: (0, 0))]
    sb_args = (q, k, v, k, v, gs, m_mat)
    sb_out_specs = (blk(lambda i: (i, 0)), pl.BlockSpec((tq, LANES), lambda i: (i, 0)))
    sb_out_shape = (jax.ShapeDtypeStruct((t, w), BF16), jax.ShapeDtypeStruct((t, LANES), F32))
    a_dec = None
    if fox_dec is not None:
        n_b, kb, nkb = _fox_dec_geometry(fox_dec[0], fox_dec[3])
        if n_b * nkb != nq:
            a_dec = _fox_decode(*fox_dec)
            fox_dec = None
    if fox_dec is None:
        a_fast, carry = pl.pallas_call(
            _sb_fast_body, grid=(nq,), in_specs=sb_in, out_specs=sb_out_specs, out_shape=sb_out_shape,
            compiler_params=_cparams(("parallel",), vmem_mib=48), name="sb_prompt_fast",
        )(*sb_args)
    else:
        dec_in, dec_out, dec_scratch = _fox_dec_specs(fox_dec[0].shape[1], kb,
                                                      lambda i: i // nkb, lambda i: i % nkb)
        n_sb, n_dec = len(sb_in), len(dec_in)

        def fused_body(*refs):
            i = pl.program_id(0)
            outs = refs[n_sb + n_dec:n_sb + n_dec + 3]
            begin, chunks, end = _fox_dec_parts(i % nkb, nkb, *refs[n_sb:n_sb + n_dec], outs[2],
                                                *refs[n_sb + n_dec + 3:])
            slots = {}
            for ci, fn in enumerate(chunks):
                slots.setdefault(((ci + 1) * N_HEADS - 1) // len(chunks), []).append(fn)
            after = {hh: (lambda fns=fns: [fn() for fn in fns]) for hh, fns in slots.items()}
            begin()
            _sb_fast_step(i, *refs[:n_sb], outs[0], outs[1], after_head=after)
            end()

        a_fast, carry, a_dec = pl.pallas_call(
            fused_body, grid=(nq,), in_specs=sb_in + dec_in,
            out_specs=sb_out_specs + (dec_out,),
            out_shape=sb_out_shape + (jax.ShapeDtypeStruct(fox_dec[0].shape, BF16),),
            scratch_shapes=dec_scratch,
            compiler_params=_cparams(("arbitrary",), vmem_mib=56), name="sb_prompt_fox_decode",
        )(*sb_args, *fox_dec)
    if nq <= 2:
        return a_fast, a_dec

    def rest(a_fast, carry):
        kidx = lambda i, s: (jnp.maximum(i - 2 - s, 0), 0)
        return pl.pallas_call(
            _sb_rest_body,
            grid=(nq, nq - 2),
            in_specs=[blk(lambda i, s: (i, 0)), blk(kidx), blk(kidx), blk(lambda i, s: (i, 0)),
                      pl.BlockSpec((tq, LANES), lambda i, s: (i, 0)),
                      blk(lambda i, s: (i, 0)),
                      pl.BlockSpec((tq, tq), lambda i, s: (0, 0))],
            out_specs=blk(lambda i, s: (i, 0)),
            out_shape=jax.ShapeDtypeStruct((t, w), BF16),
            scratch_shapes=[pltpu.VMEM((tq, w), F32), pltpu.VMEM((tq, LANES), F32)],
            compiler_params=_cparams(("parallel", "arbitrary"), vmem_mib=48),
            name="sb_prompt_rest",
        )(q, k, v, gs, carry, a_fast, m_mat)

    need_rest = jnp.max(carry[2 * tq:, :]) > SB_SKIP_LOG2
    return lax.cond(need_rest, rest, lambda a, c: a, a_fast, carry), a_dec


FOX_T = 512
FOX_WIN_TILES = (1, 2, 3, 4, 5, 6)
FOX_ZB_MAX = 40.0
FOX_SKIP_LOG = -104.0


def _head_column(blk, hh):
    lane8 = lax.broadcasted_iota(jnp.int32, blk.shape, 1)
    return jnp.sum(jnp.where(lane8 == hh, blk, 0.0), axis=1, keepdims=True)


BIAS_ONE_LANE = 3 * N_HEADS


def _split_bias_rows(f, lane):
    hi = f.astype(BF16).astype(F32)
    r1 = f - hi
    mid = r1.astype(BF16).astype(F32)
    low = (r1 - mid).astype(BF16).astype(F32)
    out = jnp.where(lane < N_HEADS, hi,
                    jnp.where(lane < 2 * N_HEADS, pltpu.roll(mid, N_HEADS, axis=1),
                              jnp.where(lane < BIAS_ONE_LANE, pltpu.roll(low, 2 * N_HEADS, axis=1),
                                        jnp.where(lane == BIAS_ONE_LANE, 1.0, 0.0))))
    return out.astype(BF16)


def _bias_selectors():
    src = lax.broadcasted_iota(jnp.int32, (N_HEADS, LANES, LANES), 1)
    dst = lax.broadcasted_iota(jnp.int32, (N_HEADS, LANES, LANES), 2)
    head = lax.broadcasted_iota(jnp.int32, (N_HEADS, LANES, LANES), 0)

    def sel(bias_base, one_base):
        term = dst - bias_base
        takes_bias = (term >= 0) & (term < 3) & (src == term * N_HEADS + head)
        takes_one = (dst >= one_base) & (dst < one_base + 3) & (src == BIAS_ONE_LANE)
        return (takes_bias | takes_one).astype(BF16)

    return jnp.stack([sel(0, 3), sel(3, 0)])


def _fox_fast_body(live_ref, q_ref, k_ref, v_ref, qb_ref, kb_ref, sel_ref, gs_ref, a_ref, kaug_s):
    hh = pl.program_id(0)
    qb = pl.program_id(1)
    nq = pl.num_programs(1)
    tb = q_ref.shape[0]
    t = k_ref.shape[0]

    @pl.when(qb == 0)
    def _():
        def build(c, carry):
            start = pl.multiple_of(c * tb, tb)
            kaug_s[pl.ds(start, tb), :] = _dot(kb_ref[pl.ds(start, tb), :], sel_ref[1]).astype(BF16)
            return carry
        lax.fori_loop(0, nq, build, 0)

    q_start = pl.multiple_of(qb * tb, tb)
    q_end = q_start + tb
    q2 = jnp.concatenate([q_ref[...], _dot(qb_ref[...], sel_ref[0]).astype(BF16)], axis=1)
    row = lax.broadcasted_iota(jnp.int32, (tb, 1), 0)
    live = live_ref[hh * nq + qb]

    def run(tiles):
        span = min(tiles * tb, t)
        lane_w = lax.broadcasted_iota(jnp.int32, (span, LANES), 1)
        ones_blk = jnp.where(lane_w == 0, 1.0, 0.0).astype(BF16)
        col = lax.broadcasted_iota(jnp.int32, (1, span), 1)

        def window(j, acc):
            upper = q_end - j * span
            start = pl.multiple_of(jnp.maximum(upper - span, 0), tb)
            k2 = jnp.concatenate([k_ref[pl.ds(start, span), :], kaug_s[pl.ds(start, span), :]], axis=1)
            p = jnp.exp2(_dot_nt(q2, k2))
            valid = jnp.logical_and(col - (q_start - start) <= row, col < upper - start)
            p = jnp.where(valid, p, 0.0).astype(BF16)
            v2 = jnp.concatenate([v_ref[pl.ds(start, span), :], ones_blk], axis=1)
            return acc + _dot(p, v2)

        n_win = (live + (tiles - 1)) // tiles
        acc = lax.fori_loop(0, n_win, window, jnp.zeros((tb, 2 * HEAD_DIM), F32))
        o = acc[:, :HEAD_DIM] / acc[:, HEAD_DIM:HEAD_DIM + 1]
        a_ref[...] = (o * gs_ref[...].astype(F32)).astype(BF16)

    bounds = (0,) + FOX_WIN_TILES
    for lo_t, hi_t in zip(bounds[:-1], bounds[1:]):
        last = hi_t == FOX_WIN_TILES[-1]
        cond = live > lo_t if last else jnp.logical_and(live > lo_t, live <= hi_t)
        pl.when(cond)(functools.partial(run, hi_t))


def _fox_slow_body(q_ref, k_ref, v_ref, fcol_ref, frow_ref, gs_ref, a_ref):
    hh = pl.program_id(0)
    qb = pl.program_id(1)
    tb = q_ref.shape[0]
    q = q_ref[...]
    fq = _head_column(fcol_ref[...], hh) * LOG2E

    def scores(kb):
        start = pl.multiple_of(kb * tb, tb)
        k = k_ref[pl.ds(start, tb), :]
        fk = frow_ref[0, :, pl.ds(start, tb)] * LOG2E
        return _dot_nt(q, k) + (fq - fk), v_ref[pl.ds(start, tb), :]

    def update(s, v, carry):
        m, l, acc = carry
        m_new = jnp.maximum(m, jnp.max(s, axis=1, keepdims=True))
        alpha = jnp.exp2(m - m_new)
        p = jnp.exp2(s - m_new)
        l = alpha * l + jnp.sum(p, axis=1, keepdims=True)
        acc = alpha * acc + _dot(p.astype(BF16), v)
        return m_new, l, acc

    def body(kb, carry):
        s, v = scores(kb)
        return update(s, v, carry)

    init = (jnp.full((tb, 1), NEG_BIG, F32), jnp.zeros((tb, 1), F32), jnp.zeros((tb, HEAD_DIM), F32))
    carry = lax.fori_loop(0, qb, body, init)
    s, v = scores(qb)
    row = lax.broadcasted_iota(jnp.int32, (tb, tb), 0)
    col = lax.broadcasted_iota(jnp.int32, (tb, tb), 1)
    s = jnp.where(col <= row, s, NEG_BIG)
    _, l, acc = update(s, v, carry)
    a_ref[...] = ((acc / l) * gs_ref[...].astype(F32)).astype(BF16)


def _fox_prompt(q, k, v, f_col, f_row, gs, zb, q_bias, k_bias):
    t, w = q.shape
    tb = min(FOX_T, t)
    nq = t // tb
    out_shape = jax.ShapeDtypeStruct((t, w), BF16)

    def fast(q, k, v, f_col, f_row, gs, q_bias, k_bias):
        f_start = f_row[:, ::tb]
        f_end = f_row[:, tb - 1::tb]
        dead = (f_start[:, :, None] - f_end[:, None, :]) < FOX_SKIP_LOG
        lo = jnp.sum(dead, axis=2).astype(jnp.int32)
        live = (jnp.arange(1, nq + 1, dtype=jnp.int32)[None, :] - lo).reshape(-1)
        head_blk = lambda h, i, live: (i, h)
        head_all = lambda h, i, live: (0, h)
        grid_spec = pltpu.PrefetchScalarGridSpec(
            num_scalar_prefetch=1,
            grid=(N_HEADS, nq),
            in_specs=[pl.BlockSpec((tb, HEAD_DIM), head_blk),
                      pl.BlockSpec((t, HEAD_DIM), head_all),
                      pl.BlockSpec((t, HEAD_DIM), head_all),
                      pl.BlockSpec((tb, LANES), lambda h, i, live: (i, 0)),
                      pl.BlockSpec((t, LANES), lambda h, i, live: (0, 0), pipeline_mode=pl.Buffered(1)),
                      pl.BlockSpec((2, None, LANES, LANES), lambda h, i, live: (0, h, 0, 0)),
                      pl.BlockSpec((tb, HEAD_DIM), head_blk)],
            out_specs=pl.BlockSpec((tb, HEAD_DIM), head_blk),
            scratch_shapes=[pltpu.VMEM((t, LANES), BF16)])
        return pl.pallas_call(
            _fox_fast_body, grid_spec=grid_spec, out_shape=out_shape,
            compiler_params=_cparams(("parallel", "arbitrary"), vmem_mib=48),
            name="fox_prompt_fast",
        )(live, q, k, v, q_bias, k_bias, _bias_selectors(), gs)

    def slow(q, k, v, f_col, f_row, gs, q_bias, k_bias):
        f_row3 = f_row.reshape(N_HEADS, 1, t)
        return pl.pallas_call(
            _fox_slow_body,
            grid=(N_HEADS, nq),
            in_specs=[pl.BlockSpec((tb, HEAD_DIM), lambda h, i: (i, h)),
                      pl.BlockSpec((t, HEAD_DIM), lambda h, i: (0, h)),
                      pl.BlockSpec((t, HEAD_DIM), lambda h, i: (0, h)),
                      pl.BlockSpec((tb, N_HEADS), lambda h, i: (i, 0)),
                      pl.BlockSpec((1, 1, t), lambda h, i: (h, 0, 0)),
                      pl.BlockSpec((tb, HEAD_DIM), lambda h, i: (i, h))],
            out_specs=pl.BlockSpec((tb, HEAD_DIM), lambda h, i: (i, h)),
            out_shape=out_shape,
            compiler_params=_cparams(("parallel", "parallel"), vmem_mib=48),
            name="fox_prompt_slow",
        )(q, k, v, f_col, f_row3, gs)

    return lax.cond(zb <= FOX_ZB_MAX, fast, slow, q, k, v, f_col, f_row, gs, q_bias, k_bias)


DEC_KB = 2048
SB_DEC_WIN = 256


def _own_head_mask():
    row = lax.broadcasted_iota(jnp.int32, (SUBLANES, W_ATT), 0)
    col = lax.broadcasted_iota(jnp.int32, (SUBLANES, W_ATT), 1)
    return (col // HEAD_DIM) == row


def _expand_q(q_ref, qx_s):
    n_tok = q_ref.shape[0]
    qf = q_ref[...].astype(F32)
    own = _own_head_mask()
    for tkn in range(n_tok):
        rep = jnp.broadcast_to(qf[tkn:tkn + 1, :], (SUBLANES, W_ATT))
        qx_s[tkn * N_HEADS:(tkn + 1) * N_HEADS, :] = jnp.where(own, rep, 0.0)


def _pad_new(kn_ref, vn_ref, knp_s, vnp_s):
    n_tok = kn_ref.shape[0]
    knp_s[...] = jnp.zeros_like(knp_s)
    vnp_s[...] = jnp.zeros_like(vnp_s)
    knp_s[0:n_tok, :] = kn_ref[...]
    vnp_s[0:n_tok, :] = vn_ref[...]


def _gather_heads(c_ref, dst_s):
    n_keys = dst_s.shape[0]
    for hh in range(N_HEADS):
        dst_s[:, hh * HEAD_DIM:(hh + 1) * HEAD_DIM] = (
            c_ref[0, pl.ds(hh, n_keys, stride=N_HEADS), :].astype(BF16))


def _collapse_heads(o_full, out_s):
    n_tok = o_full.shape[0] // N_HEADS
    own = _own_head_mask()
    for tkn in range(n_tok):
        blk = jnp.where(own, o_full[tkn * N_HEADS:(tkn + 1) * N_HEADS, :], 0.0)
        out_s[tkn:tkn + 1, :] = jnp.sum(blk, axis=0, keepdims=True)


def _sb_dec_fast_body(q_ref, kn_ref, vn_ref, kc_ref, vc_ref, gs_ref, mn_ref, mc_ref, a_ref, c_ref,
                      qx_s, knp_s, vnp_s, kx_s, vx_s, out_s):
    n_rows = qx_s.shape[0]
    _expand_q(q_ref, qx_s)
    _pad_new(kn_ref, vn_ref, knp_s, vnp_s)
    qx = qx_s[...].astype(BF16)
    row = lax.broadcasted_iota(jnp.int32, (n_rows, LANES), 0)
    col = lax.broadcasted_iota(jnp.int32, (n_rows, LANES), 1)
    mask = col < row // N_HEADS
    o_new, carry = _sb_block(qx, knp_s[...], vnp_s[...], mn_ref[...], None, mask)
    _gather_heads(kc_ref, kx_s)
    _gather_heads(vc_ref, vx_s)
    o_win, carry = _sb_block(qx, kx_s[...], vx_s[...], mc_ref[...], carry, None)
    _collapse_heads(o_new + o_win, out_s)
    a_ref[...] = (out_s[...] * gs_ref[...].astype(F32)).astype(BF16)
    c_ref[...] = jnp.broadcast_to(carry, c_ref.shape)


def _sb_dec_rest_body(q_ref, kc_ref, vc_ref, gs_ref, cin_ref, afast_ref, mc_ref, a_ref,
                      qx_s, kx_s, vx_s, acc_s, c_s, out_s):
    s = pl.program_id(1)

    @pl.when(s == 0)
    def _():
        _expand_q(q_ref, qx_s)
        acc_s[...] = jnp.zeros_like(acc_s)
        c_s[...] = cin_ref[:, 0:1]

    @pl.when(jnp.max(c_s[...]) > SB_SKIP_LOG2)
    def _():
        _gather_heads(kc_ref, kx_s)
        _gather_heads(vc_ref, vx_s)
        o, carry = _sb_block(qx_s[...].astype(BF16), kx_s[...], vx_s[...], mc_ref[...], c_s[...], None)
        acc_s[...] += o
        c_s[...] = carry

    @pl.when(s == pl.num_programs(1) - 1)
    def _():
        _collapse_heads(acc_s[...], out_s)
        a_ref[...] = (afast_ref[...].astype(F32) + out_s[...] * gs_ref[...].astype(F32)).astype(BF16)


def _sb_decode(q, k_new, v_new, k_cache, v_cache, gs, dec_seq):
    n_rows_all, w = q.shape
    n_b = n_rows_all // dec_seq
    p_len = k_cache.shape[1] // N_HEADS
    win = min(SB_DEC_WIN, p_len)
    n_win = p_len // win
    n_rows = dec_seq * N_HEADS
    m_new = _tri(LANES, "row_gt_col")
    m_win = _tri(win, "row_gt_col")
    tok = lambda b: (b, 0)
    newest = pl.BlockSpec((1, win * N_HEADS, HEAD_DIM), lambda b: (b, n_win - 1, 0))
    a_fast, carry = pl.pallas_call(
        _sb_dec_fast_body,
        grid=(n_b,),
        in_specs=[pl.BlockSpec((dec_seq, w), tok), pl.BlockSpec((dec_seq, w), tok),
                  pl.BlockSpec((dec_seq, w), tok),
                  newest, newest,
                  pl.BlockSpec((dec_seq, w), tok),
                  pl.BlockSpec((LANES, LANES), lambda b: (0, 0)),
                  pl.BlockSpec((win, win), lambda b: (0, 0))],
        out_specs=(pl.BlockSpec((dec_seq, w), tok), pl.BlockSpec((n_rows, LANES), tok)),
        out_shape=(jax.ShapeDtypeStruct((n_rows_all, w), BF16),
                   jax.ShapeDtypeStruct((n_b * n_rows, LANES), F32)),
        scratch_shapes=[pltpu.VMEM((n_rows, w), F32), pltpu.VMEM((LANES, w), BF16),
                        pltpu.VMEM((LANES, w), BF16), pltpu.VMEM((win, w), BF16),
                        pltpu.VMEM((win, w), BF16), pltpu.VMEM((dec_seq, w), F32)],
        compiler_params=_cparams(("parallel",), vmem_mib=48),
        name="sb_decode_fast",
    )(q, k_new, v_new, k_cache, v_cache, gs, m_new, m_win)
    if n_win <= 1:
        return a_fast

    def rest(a_fast, carry):
        tok2 = lambda b, s: (b, 0)
        older = pl.BlockSpec((1, win * N_HEADS, HEAD_DIM), lambda b, s: (b, n_win - 2 - s, 0))
        return pl.pallas_call(
            _sb_dec_rest_body,
            grid=(n_b, n_win - 1),
            in_specs=[pl.BlockSpec((dec_seq, w), tok2),
                      older, older,
                      pl.BlockSpec((dec_seq, w), tok2),
                      pl.BlockSpec((n_rows, LANES), tok2),
                      pl.BlockSpec((dec_seq, w), tok2),
                      pl.BlockSpec((win, win), lambda b, s: (0, 0))],
            out_specs=pl.BlockSpec((dec_seq, w), tok2),
            out_shape=jax.ShapeDtypeStruct((n_rows_all, w), BF16),
            scratch_shapes=[pltpu.VMEM((n_rows, w), F32), pltpu.VMEM((win, w), BF16),
                            pltpu.VMEM((win, w), BF16), pltpu.VMEM((n_rows, w), F32),
                            pltpu.VMEM((n_rows, 1), F32), pltpu.VMEM((dec_seq, w), F32)],
            compiler_params=_cparams(("parallel", "arbitrary"), vmem_mib=48),
            name="sb_decode_rest",
        )(q, k_cache, v_cache, gs, carry, a_fast, m_win)

    need_rest = jnp.max(carry) > SB_SKIP_LOG2
    return lax.cond(need_rest, rest, lambda a, c: a, a_fast, carry)


def _fox_dec_body(*refs):
    _fox_dec_step(pl.program_id(1), pl.num_programs(1), *refs)


FOX_DEC_CHUNK = 1024


def _fox_dec_step(s, n_s, *refs):
    begin, chunks, end = _fox_dec_parts(s, n_s, *refs)
    begin()
    for chunk_fn in chunks:
        chunk_fn()
    end()


def _fox_dec_parts(s, n_s, q_ref, kn_ref, vn_ref, kc_ref, vc_ref, gs_ref, c_ref, r_ref, a_ref,
                   acc_s, m_s, l_s):
    n_rows = q_ref.shape[1]
    q = q_ref[0]
    own = lax.broadcasted_iota(jnp.int32, (SUBLANES, LANES), 0) == \
        lax.broadcasted_iota(jnp.int32, (SUBLANES, LANES), 1) % N_HEADS
    head_bias = jnp.where(own, 0.0, NEG_BIG)

    def update(sc, v):
        m = m_s[...]
        m_new = jnp.maximum(m, jnp.max(sc, axis=1, keepdims=True))
        alpha = jnp.exp2(m - m_new)
        p = jnp.exp2(sc - m_new)
        l_s[...] = alpha * l_s[...] + jnp.sum(p, axis=1, keepdims=True)
        acc_s[...] = alpha * acc_s[...] + _dot(p.astype(BF16), v)
        m_s[...] = m_new

    def add_key_bias(z, decay_row):
        n = decay_row.shape[1]
        tile = jnp.concatenate([head_bias] * (n // LANES), axis=1) + decay_row * LOG2E
        return (z.reshape(n_rows // SUBLANES, SUBLANES, n) + tile[None]).reshape(n_rows, n)

    def begin():
        @pl.when(s == 0)
        def _():
            m_s[...] = jnp.full_like(m_s, NEG_BIG)
            l_s[...] = jnp.zeros_like(l_s)
            acc_s[...] = jnp.zeros_like(acc_s)
            row = lax.broadcasted_iota(jnp.int32, (n_rows, LANES), 0)
            col = lax.broadcasted_iota(jnp.int32, (n_rows, LANES), 1)
            sc = add_key_bias(_dot_nt(q, kn_ref[0]), -c_ref[0])
            sc = jnp.where(col // N_HEADS <= row // N_HEADS, sc, NEG_BIG)
            update(sc, vn_ref[0])

    n_keys_rows = kc_ref.shape[1]
    chunk = min(FOX_DEC_CHUNK * N_HEADS, n_keys_rows)

    def one_chunk(c0):
        k2 = kc_ref[0, c0:c0 + chunk, :].astype(BF16)
        v2 = vc_ref[0, c0:c0 + chunk, :].astype(BF16)
        update(add_key_bias(_dot_nt(q, k2), r_ref[0, :, c0:c0 + chunk]), v2)

    def end():
        @pl.when(s == n_s - 1)
        def _():
            a_ref[0] = ((acc_s[...] / l_s[...]) * gs_ref[0].astype(F32)).astype(BF16)

    return begin, [functools.partial(one_chunk, c0) for c0 in range(0, n_keys_rows, chunk)], end


def _fox_dec_geometry(q3, k_cache):
    n_b = q3.shape[0]
    p_len = k_cache.shape[1] // N_HEADS
    kb = min(DEC_KB, p_len)
    return n_b, kb, p_len // kb


def _fox_dec_specs(n_rows, kb, stream, block):
    tok = pl.BlockSpec((1, n_rows, HEAD_DIM), lambda *g: (stream(*g), 0, 0))
    cache = pl.BlockSpec((1, kb * N_HEADS, HEAD_DIM), lambda *g: (stream(*g), block(*g), 0))
    in_specs = [tok, tok, tok, cache, cache, tok,
                pl.BlockSpec((1, 1, LANES), lambda *g: (stream(*g), 0, 0)),
                pl.BlockSpec((1, 1, kb * N_HEADS), lambda *g: (stream(*g), 0, block(*g)))]
    scratch = [pltpu.VMEM((n_rows, HEAD_DIM), F32), pltpu.VMEM((n_rows, 1), F32),
               pltpu.VMEM((n_rows, 1), F32)]
    return in_specs, tok, scratch


def _fox_decode(q3, kn3, vn3, k_cache, v_cache, gs3, c_row, r_row):
    n_b, kb, nkb = _fox_dec_geometry(q3, k_cache)
    in_specs, out_spec, scratch = _fox_dec_specs(q3.shape[1], kb, lambda b, s: b, lambda b, s: s)
    return pl.pallas_call(
        _fox_dec_body,
        grid=(n_b, nkb),
        in_specs=in_specs,
        out_specs=out_spec,
        out_shape=jax.ShapeDtypeStruct(q3.shape, BF16),
        scratch_shapes=scratch,
        compiler_params=_cparams(("parallel", "arbitrary"), vmem_mib=56),
        name="fox_decode",
    )(q3, kn3, vn3, k_cache, v_cache, gs3, c_row, r_row)


def _post_body(x_ref, asb_ref, afx_ref, msb_ref, mfx_ref, wsb_ref, wfx_ref, wo_ref, y_ref):
    u_sb = _dot(asb_ref[...], wsb_ref[...])
    u_fx = _dot(afx_ref[...], wfx_ref[...])
    merged = msb_ref[...].astype(F32) * u_sb + mfx_ref[...].astype(F32) * u_fx
    y_ref[...] = x_ref[...] + _dot(merged.astype(BF16), wo_ref[...])


def _post(x2d, a_sb, a_fx, m_sig, w_sb, w_fx, w_o):
    m, d = x2d.shape
    w = a_sb.shape[1]
    tm = min(m, 512)
    row = lambda i: (i, 0)
    const = lambda i: (0, 0)
    resident = functools.partial(pl.BlockSpec, index_map=const, pipeline_mode=pl.Buffered(1))
    return pl.pallas_call(
        _post_body,
        grid=(m // tm,),
        in_specs=[pl.BlockSpec((tm, d), row), pl.BlockSpec((tm, w), row), pl.BlockSpec((tm, w), row),
                  pl.BlockSpec((tm, d), lambda i: (i, 0)), pl.BlockSpec((tm, d), lambda i: (i, 1)),
                  resident((w, d)), resident((w, d)), resident((d, d))],
        out_specs=pl.BlockSpec((tm, d), row),
        out_shape=jax.ShapeDtypeStruct((m, d), F32),
        compiler_params=_cparams(("parallel",), vmem_mib=56),
        name="post",
    )(x2d, a_sb, a_fx, m_sig, m_sig, w_sb, w_fx, w_o)


def _project_all(hp, hs, w, qnw, knw):
    seg = lambda c: c * W_ATT
    q_sb = _proj(hp, hs, w, seg(0), "scale")
    k_sb = _proj(hp, hs, w, seg(1), "kv")
    v_sb = _proj(hp, hs, w, seg(2), "kv")
    g_sb = _proj(hp, hs, w, seg(3), "silu")
    q_fx = _proj(hp, hs, w, seg(4), "qnorm", nw=qnw)
    k_fx = _proj(hp, hs, w, seg(5), "knorm", nw=knw)
    v_fx = _proj(hp, hs, w, seg(6), "kv")
    g_fx = _proj(hp, hs, w, seg(7), "silu")
    m_sig = _proj(hp, hs, w, seg(8) + N_HEADS, "sigmoid", n_tiles=2 * D_MODEL // W_ATT)
    calls = (q_sb, k_sb, v_sb, g_sb, q_fx, k_fx, v_fx, g_fx, m_sig)
    prompt = tuple(o for r in calls for o in r[:len(r) // 2])
    decode = tuple(o for r in calls for o in r[len(r) // 2:])
    return prompt, decode


def kernel(x_prompt, x_sample, cache_sb_k, cache_sb_v, cache_fox_k, cache_fox_v, cache_fox_logf,
           norm_w, w_in, b_forget, q_norm_w, k_norm_w, w_branch_sb, w_branch_fox, w_out):
    depth = norm_w.shape[0]
    assert depth == 1, "single-layer step"
    bsz, seq, d = x_prompt.shape
    assert bsz == 1
    n_dec, dec_seq, _ = x_sample.shape
    p_len = cache_sb_k.shape[2]
    n_main = 8 * W_ATT

    w_t = jnp.swapaxes(w_in, 1, 2)
    b_row = jnp.pad(b_forget[0].astype(F32)[None, :], ((0, 0), (0, LANES - N_HEADS)))
    qnw = jnp.tile(q_norm_w[0].astype(F32), N_HEADS)[None, :]
    knw = jnp.tile(k_norm_w[0].astype(F32), N_HEADS)[None, :]
    nw_row = norm_w[0].astype(F32)[None, :]
    w_sb = w_branch_sb[0].astype(BF16)
    w_fx = w_branch_fox[0].astype(BF16)
    w_o = w_out[0].astype(BF16)

    xp = x_prompt.reshape(seq, d)
    xs = x_sample.reshape(n_dec * dec_seq, d)
    zb = (1.02 * HEAD_DIM * QK_SCALE) * jnp.max(jnp.abs(q_norm_w[0])) * jnp.max(jnp.abs(k_norm_w[0]))
    zb = zb.astype(F32)
    hp, logf, f_col, f_row, q_bias, k_bias = _norm_logf_prompt(
        xp, nw_row, w_t, n_main, b_row, (zb * LOG2E).reshape(1, 1))
    hs = _rmsnorm(xs, nw_row)
    prompt_proj, decode_proj = _project_all(hp, hs, w_t, qnw, knw)

    (q_sb, k_sb, k_sb_b, v_sb, v_sb_b, g_sb, q_fx, k_fx, k_fx_b, v_fx, v_fx_b, g_fx,
     m_sig) = prompt_proj
    (sq_sb, sk_sb, sk_sb_b, sv_sb, sv_sb_b, sg_sb, sq_fx, sk_fx, sk_fx_b, sv_fx, sv_fx_b, sg_fx,
     sm_sig) = decode_proj

    past_logf_t = jnp.transpose(cache_fox_logf[0].astype(F32), (0, 2, 1)).reshape(n_dec * N_HEADS, p_len)
    s_logf, c_all, r_past = _logf_sample(hs, w_t, n_main, b_row, past_logf_t, dec_seq)
    n_th = dec_seq * N_HEADS
    c_row = jnp.transpose(c_all.reshape(N_HEADS, n_dec, dec_seq), (1, 2, 0)).reshape(n_dec, 1, n_th)
    r_row = jnp.transpose(r_past.reshape(n_dec, N_HEADS, p_len), (0, 2, 1)).reshape(n_dec, 1, p_len * N_HEADS)
    by_head = lambda a: a.reshape(n_dec, n_th, HEAD_DIM)
    kc_sb = cache_sb_k[0].reshape(n_dec, p_len * N_HEADS, HEAD_DIM)
    vc_sb = cache_sb_v[0].reshape(n_dec, p_len * N_HEADS, HEAD_DIM)
    kc_fx = cache_fox_k[0].reshape(n_dec, p_len * N_HEADS, HEAD_DIM)
    vc_fx = cache_fox_v[0].reshape(n_dec, p_len * N_HEADS, HEAD_DIM)

    a_sb, sa_fx = _sb_prompt(q_sb, k_sb_b, v_sb_b, g_sb,
                             fox_dec=(by_head(sq_fx), by_head(sk_fx_b), by_head(sv_fx_b), kc_fx, vc_fx,
                                      by_head(sg_fx), c_row, r_row))
    sa_fx = sa_fx.reshape(n_dec * dec_seq, W_ATT)
    a_fx = _fox_prompt(q_fx, k_fx_b, v_fx_b, f_col, f_row, g_fx, zb, q_bias, k_bias)
    sa_sb = _sb_decode(sq_sb, sk_sb_b, sv_sb_b, kc_sb, vc_sb, sg_sb, dec_seq)
    y_prompt = _post(xp, a_sb, a_fx, m_sig, w_sb, w_fx, w_o).reshape(bsz, seq, d)
    y_sample = _post(xs, sa_sb, sa_fx, sm_sig, w_sb, w_fx, w_o).reshape(n_dec, dec_seq, d)

    hd = (N_HEADS, HEAD_DIM)
    return (y_prompt, y_sample,
            k_sb.reshape(1, bsz, seq, *hd), v_sb.reshape(1, bsz, seq, *hd),
            k_fx.reshape(1, bsz, seq, *hd), v_fx.reshape(1, bsz, seq, *hd),
            logf.reshape(1, bsz, seq, N_HEADS),
            sk_sb.reshape(1, n_dec, dec_seq, *hd), sv_sb.reshape(1, n_dec, dec_seq, *hd),
            sk_fx.reshape(1, n_dec, dec_seq, *hd), sv_fx.reshape(1, n_dec, dec_seq, *hd),
            s_logf.reshape(1, n_dec, dec_seq, N_HEADS))
```

```python
import functools

import jax
import jax.numpy as jnp
from jax import lax
from jax.experimental import pallas as pl
from jax.experimental.pallas import tpu as pltpu

F32 = jnp.float32
BF16 = jnp.bfloat16

D_MODEL = 2048
N_HEADS = 8
HEAD_DIM = 128
W_ATT = N_HEADS * HEAD_DIM
RMS_EPS = 1e-6
QK_SCALE = HEAD_DIM ** -0.5
LOG2E = 1.4426950408889634
LANES = 128
SUBLANES = 8
SB_SKIP_LOG2 = -150.0
NEG_BIG = -1e30
MIB = 1024 * 1024


def _cparams(sem, vmem_mib=None):
    kw = dict(dimension_semantics=sem)
    if vmem_mib is not None:
        kw["vmem_limit_bytes"] = vmem_mib * MIB
    return pltpu.CompilerParams(**kw)


def _softplus_neg_abs(z):
    return jnp.log1p(jnp.exp(-jnp.abs(z)))


def _log_sigmoid(z):
    return jnp.minimum(z, 0.0) - _softplus_neg_abs(z)


def _split_bf16(x, n):
    parts = []
    r = x
    for _ in range(n - 1):
        p = r.astype(BF16)
        parts.append(p)
        r = r - p.astype(F32)
    parts.append(r.astype(BF16))
    return parts


def _dot(a, b):
    return jnp.dot(a, b, preferred_element_type=F32)


def _dot_nt(a, b):
    return lax.dot_general(a, b, (((1,), (1,)), ((), ())), preferred_element_type=F32)


def _dot_split_lhs(x, m, n):
    acc = None
    for p in _split_bf16(x, n):
        t = _dot(p, m)
        acc = t if acc is None else acc + t
    return acc


def _dot_split_rhs(m, x, n):
    acc = None
    for p in _split_bf16(x, n):
        t = _dot(m, p)
        acc = t if acc is None else acc + t
    return acc


def _tri(n, kind):
    r = lax.broadcasted_iota(jnp.int32, (n, n), 0)
    c = lax.broadcasted_iota(jnp.int32, (n, n), 1)
    if kind == "row_gt_col":
        m = r > c
    elif kind == "row_le_col":
        m = r <= c
    elif kind == "row_ge_col":
        m = r >= c
    else:
        raise ValueError(kind)
    return m.astype(BF16)


def _rmsnorm_body(x_ref, w_ref, o_ref):
    x = x_ref[...]
    ms = jnp.mean(x * x, axis=-1, keepdims=True)
    o_ref[...] = (x * lax.rsqrt(ms + RMS_EPS) * w_ref[...]).astype(o_ref.dtype)


def _rmsnorm(x2d, w_row):
    m, d = x2d.shape
    tm = min(m, 512)
    return pl.pallas_call(
        _rmsnorm_body,
        grid=(m // tm,),
        in_specs=[pl.BlockSpec((tm, d), lambda i: (i, 0)),
                  pl.BlockSpec((1, d), lambda i: (0, 0))],
        out_specs=pl.BlockSpec((tm, d), lambda i: (i, 0)),
        out_shape=jax.ShapeDtypeStruct((m, d), BF16),
        compiler_params=_cparams(("parallel",)),
        name="rmsnorm",
    )(x2d, w_row)


def _head_rmsnorm(acc, nw):
    parts = []
    for hh in range(N_HEADS):
        a = acc[:, hh * HEAD_DIM:(hh + 1) * HEAD_DIM]
        ms = jnp.mean(a * a, axis=-1, keepdims=True)
        parts.append(a * lax.rsqrt(ms + RMS_EPS))
    return jnp.concatenate(parts, axis=1) * nw


def _proj_epilogue(acc, kind, nw_ref, outs):
    if kind == "scale":
        outs[0][...] = (acc * (QK_SCALE * LOG2E)).astype(BF16)
    elif kind == "kv":
        outs[0][...] = acc
    elif kind == "silu":
        outs[0][...] = (acc * jax.nn.sigmoid(acc)).astype(BF16)
    elif kind == "sigmoid":
        outs[0][...] = jax.nn.sigmoid(acc).astype(BF16)
    elif kind == "qnorm":
        outs[0][...] = (_head_rmsnorm(acc, nw_ref[...]) * (QK_SCALE * LOG2E)).astype(BF16)
    elif kind == "knorm":
        outs[0][...] = _head_rmsnorm(acc, nw_ref[...])
    else:
        raise ValueError(kind)


def _proj_body(*refs, kind, n_out, shift):
    hp_ref, hs_ref, w_ref = refs[:3]
    pos = 3
    wn_ref = nw_ref = None
    if shift:
        wn_ref = refs[pos]
        pos += 1
    if kind in ("qnorm", "knorm"):
        nw_ref = refs[pos]
        pos += 1
    outs_p = refs[pos:pos + n_out]
    outs_s = refs[pos + n_out:pos + 2 * n_out]
    wb_s = refs[-1]
    i = pl.program_id(1)
    n_prompt = pl.num_programs(1) - 1

    @pl.when(i == 0)
    def _():
        if shift:
            wt = jnp.concatenate([w_ref[shift:, :], wn_ref[...]], axis=0)
        else:
            wt = w_ref[...]
        wb_s[...] = wt.T.astype(BF16)

    @pl.when(i < n_prompt)
    def _():
        _proj_epilogue(_dot(hp_ref[...], wb_s[...]), kind, nw_ref, outs_p)

    @pl.when(i == n_prompt)
    def _():
        _proj_epilogue(_dot(hs_ref[...], wb_s[...]), kind, nw_ref, outs_s)


def _proj(hp, hs, wt, col0, kind, nw=None, n_tiles=1):
    mp, d = hp.shape
    ms = hs.shape[0]
    tn = W_ATT
    dtypes = (F32,) if kind in ("kv", "knorm") else (BF16,)
    tm = min(mp, 1024)
    n_prompt = mp // tm
    blk0 = col0 // tn
    shift = col0 - blk0 * tn
    assert shift in (0, SUBLANES)
    grid = (n_tiles, n_prompt + 1)
    prow = lambda j, i: (jnp.minimum(i, n_prompt - 1), 0)
    w_mode = {} if n_tiles > 1 else dict(pipeline_mode=pl.Buffered(1))
    in_specs = [pl.BlockSpec((tm, d), prow),
                pl.BlockSpec((ms, d), lambda j, i: (0, 0)),
                pl.BlockSpec((None, tn, d), lambda j, i: (0, blk0 + j, 0), **w_mode)]
    args = [hp, hs, wt]
    if shift:
        per = tn // shift
        in_specs.append(pl.BlockSpec((None, shift, d), lambda j, i: (0, (blk0 + j + 1) * per, 0)))
        args.append(wt)
    if kind in ("qnorm", "knorm"):
        in_specs.append(pl.BlockSpec((1, tn), lambda j, i: (0, 0)))
        args.append(nw)
    n_cols = tn * n_tiles
    p_block = pl.BlockSpec((tm, tn), lambda j, i: (jnp.minimum(i, n_prompt - 1), j))
    s_block = pl.BlockSpec((ms, tn), lambda j, i: (0, j))
    out_shape = tuple(jax.ShapeDtypeStruct((mp, n_cols), t) for t in dtypes) + \
        tuple(jax.ShapeDtypeStruct((ms, n_cols), t) for t in dtypes)
    out_specs = (p_block,) * len(dtypes) + (s_block,) * len(dtypes)
    return pl.pallas_call(
        functools.partial(_proj_body, kind=kind, n_out=len(dtypes), shift=shift),
        grid=grid,
        in_specs=in_specs,
        out_specs=out_specs,
        out_shape=out_shape,
        scratch_shapes=[pltpu.VMEM((d, tn), BF16)],
        compiler_params=_cparams(("parallel", "arbitrary"), vmem_mib=56),
        name="proj_" + kind,
    )(*args)


def _forget_weight(wft_ref):
    wft = wft_ref[...]
    pad = jnp.zeros((LANES - wft.shape[0], wft.shape[1]), F32)
    return jnp.concatenate([wft, pad], axis=0).T.astype(BF16)


def _norm_logf_body(x_ref, nw_ref, wft_ref, brow_ref, l_ref, c2_ref,
                    h_ref, lf_ref, fcol_ref, frow_ref, qbias_ref, kbias_ref, ccol_s, wf_s):
    i = pl.program_id(0)
    tm = x_ref.shape[0]

    @pl.when(i == 0)
    def _():
        ccol_s[...] = jnp.zeros_like(ccol_s)
        wf_s[...] = _forget_weight(wft_ref)

    x = x_ref[...]
    ms = jnp.mean(x * x, axis=-1, keepdims=True)
    h = (x * lax.rsqrt(ms + RMS_EPS) * nw_ref[...]).astype(BF16)
    h_ref[...] = h
    lf = _log_sigmoid(_dot(h, wf_s[...]) + brow_ref[...])
    lf_ref[...] = lf[:, :N_HEADS]
    f_col = _dot_split_rhs(l_ref[...], lf, 3) + ccol_s[...]
    fcol_ref[...] = f_col[:, :N_HEADS]
    ccol_s[...] = f_col[tm - 1:tm, :]
    frow_ref[...] = f_col.T[:N_HEADS, :]
    lane = lax.broadcasted_iota(jnp.int32, f_col.shape, 1)
    f2 = f_col * LOG2E
    qbias_ref[...] = _split_bias_rows(f2 - c2_ref[...], lane)
    kbias_ref[...] = _split_bias_rows(-f2, lane)


def _norm_logf_prompt(x2d, nw_row, wt, f_row0, b_row, c2):
    m, d = x2d.shape
    tm = min(m, 512)
    l_mat = _tri(tm, "row_ge_col")
    const = lambda i: (0, 0)
    return pl.pallas_call(
        _norm_logf_body,
        grid=(m // tm,),
        in_specs=[pl.BlockSpec((tm, d), lambda i: (i, 0)),
                  pl.BlockSpec((1, d), const),
                  pl.BlockSpec((None, N_HEADS, d), lambda i: (0, f_row0 // N_HEADS, 0)),
                  pl.BlockSpec((1, LANES), const),
                  pl.BlockSpec((tm, tm), const),
                  pl.BlockSpec((1, 1), const)],
        out_specs=(pl.BlockSpec((tm, d), lambda i: (i, 0)),
                   pl.BlockSpec((tm, N_HEADS), lambda i: (i, 0)),
                   pl.BlockSpec((tm, N_HEADS), lambda i: (i, 0)),
                   pl.BlockSpec((N_HEADS, tm), lambda i: (0, i)),
                   pl.BlockSpec((tm, LANES), lambda i: (i, 0)),
                   pl.BlockSpec((tm, LANES), lambda i: (i, 0))),
        out_shape=(jax.ShapeDtypeStruct((m, d), BF16),
                   jax.ShapeDtypeStruct((m, N_HEADS), F32),
                   jax.ShapeDtypeStruct((m, N_HEADS), F32),
                   jax.ShapeDtypeStruct((N_HEADS, m), F32),
                   jax.ShapeDtypeStruct((m, LANES), BF16),
                   jax.ShapeDtypeStruct((m, LANES), BF16)),
        scratch_shapes=[pltpu.VMEM((1, LANES), F32), pltpu.VMEM((d, LANES), BF16)],
        compiler_params=_cparams(("arbitrary",)),
        name="norm_logf_prompt",
    )(x2d, nw_row, wt, b_row, l_mat, c2)


def _logf_sample_body(h_ref, wft_ref, brow_ref, bu_ref, x_ref, ms_ref, lf_ref, c_ref, r_ref, *, kb):
    lf = _log_sigmoid(_dot(h_ref[...], _forget_weight(wft_ref)) + brow_ref[...])
    lf_ref[...] = lf[:, :N_HEADS]
    lft = lf.T[:2 * SUBLANES, :]
    c_ref[...] = _dot_split_lhs(lft, bu_ref[...], 3)[:N_HEADS, :]
    n_blocks = x_ref.shape[1] // kb
    carry = jnp.zeros((x_ref.shape[0], 1), F32)
    for blk in range(n_blocks - 1, -1, -1):
        x = x_ref[:, blk * kb:(blk + 1) * kb]
        cum = _dot_split_lhs(x, ms_ref[...], 3)
        r_ref[:, blk * kb:(blk + 1) * kb] = cum + carry
        carry = carry + cum[:, 0:1] + x[:, 0:1]


def _logf_sample(h_s, wt, f_row0, b_row, past_logf_t, dec_seq):
    n_rows, d = h_s.shape
    n_bh, p_len = past_logf_t.shape
    kb = min(p_len, 512)
    r = lax.broadcasted_iota(jnp.int32, (n_rows, n_rows), 0)
    c = lax.broadcasted_iota(jnp.int32, (n_rows, n_rows), 1)
    bu = ((r // dec_seq == c // dec_seq) & (r <= c)).astype(BF16)
    ms = _tri(kb, "row_gt_col")
    whole = lambda shape: pl.BlockSpec(shape, lambda i: (0,) * len(shape))
    return pl.pallas_call(
        functools.partial(_logf_sample_body, kb=kb),
        grid=(1,),
        in_specs=[whole((n_rows, d)),
                  pl.BlockSpec((None, N_HEADS, d), lambda i: (0, f_row0 // N_HEADS, 0)),
                  whole((1, LANES)), whole((n_rows, n_rows)), whole((n_bh, p_len)), whole((kb, kb))],
        out_specs=(whole((n_rows, N_HEADS)), whole((N_HEADS, n_rows)), whole((n_bh, p_len))),
        out_shape=(jax.ShapeDtypeStruct((n_rows, N_HEADS), F32),
                   jax.ShapeDtypeStruct((N_HEADS, n_rows), F32),
                   jax.ShapeDtypeStruct((n_bh, p_len), F32)),
        compiler_params=_cparams(("arbitrary",), vmem_mib=32),
        name="logf_sample",
    )(h_s, wt, b_row, bu, past_logf_t, ms)


SB_TQ = 256


def _sb_block(q, k, v, m_mat, carry, mask):
    z = _dot_nt(q, k)
    lsn = jnp.minimum(-z, 0.0) - jnp.log(1.0 + jnp.exp2(-jnp.abs(z))) * LOG2E
    lsp = lsn + z
    if mask is not None:
        lsn = jnp.where(mask, lsn, 0.0)
    cum = _dot_split_lhs(lsn, m_mat, 2)
    if carry is not None:
        cum = cum + carry
    w = jnp.exp2(lsp + cum)
    if mask is not None:
        w = jnp.where(mask, w, 0.0)
    o = _dot(w.astype(BF16), v)
    new_carry = cum[:, 0:1] + lsn[:, 0:1]
    return o, new_carry


def _sb_fast_body(*refs):
    _sb_fast_step(pl.program_id(0), *refs)


def _sb_fast_step(i, q_ref, kd_ref, vd_ref, kp_ref, vp_ref, gs_ref, m_ref, a_ref, c_ref, after_head=None):
    tq = q_ref.shape[0]
    has_prev = i > 0
    row = lax.broadcasted_iota(jnp.int32, (tq, tq), 0)
    col = lax.broadcasted_iota(jnp.int32, (tq, tq), 1)
    dmask = col < row
    pmask = jnp.logical_and(has_prev, col >= 0)
    lane = lax.broadcasted_iota(jnp.int32, (tq, LANES), 1)
    m_mat = m_ref[...]
    cacc = jnp.full((tq, LANES), NEG_BIG, F32)
    for hh in range(N_HEADS):
        sl = slice(hh * HEAD_DIM, (hh + 1) * HEAD_DIM)
        q = q_ref[:, sl]
        od, cd = _sb_block(q, kd_ref[:, sl].astype(BF16), vd_ref[:, sl].astype(BF16), m_mat, None, dmask)
        op, cp = _sb_block(q, kp_ref[:, sl].astype(BF16), vp_ref[:, sl].astype(BF16), m_mat, cd, pmask)
        a_ref[:, sl] = ((od + op) * gs_ref[:, sl].astype(F32)).astype(BF16)
        cacc = jnp.where(lane == hh, cp, cacc)
        if after_head and hh in after_head:
            after_head[hh]()
    c_ref[...] = cacc


def _sb_rest_body(q_ref, k_ref, v_ref, gs_ref, cin_ref, afast_ref, m_ref, a_ref, o_s, c_s):
    i = pl.program_id(0)
    s = pl.program_id(1)
    j = i - 2 - s
    tq = q_ref.shape[0]

    @pl.when(s == 0)
    def _():
        o_s[...] = jnp.zeros_like(o_s)
        c_s[...] = cin_ref[...]

    active = jnp.logical_and(j >= 0, jnp.max(c_s[...]) > SB_SKIP_LOG2)

    @pl.when(active)
    def _():
        lane = lax.broadcasted_iota(jnp.int32, (tq, LANES), 1)
        m_mat = m_ref[...]
        c_all = c_s[...]
        cacc = c_all
        for hh in range(N_HEADS):
            sl = slice(hh * HEAD_DIM, (hh + 1) * HEAD_DIM)
            carry = jnp.sum(jnp.where(lane == hh, c_all, 0.0), axis=1, keepdims=True)
            o, cn = _sb_block(q_ref[:, sl], k_ref[:, sl].astype(BF16), v_ref[:, sl].astype(BF16), m_mat,
                              carry, None)
            o_s[:, sl] += o
            cacc = jnp.where(lane == hh, cn, cacc)
        c_s[...] = cacc

    @pl.when(s == pl.num_programs(1) - 1)
    def _():
        a_ref[...] = (afast_ref[...].astype(F32) + o_s[...] * gs_ref[...].astype(F32)).astype(BF16)


def _sb_prompt(q, k, v, gs, fox_dec=None):
    t, w = q.shape
    tq = min(SB_TQ, t)
    nq = t // tq
    m_mat = _tri(tq, "row_gt_col")
    blk = lambda f: pl.BlockSpec((tq, w), f)
    sb_in = [blk(lambda i: (i, 0)), blk(lambda i: (i, 0)), blk(lambda i: (i, 0)),
             blk(lambda i: (jnp.maximum(i - 1, 0), 0)), blk(lambda i: (jnp.maximum(i - 1, 0), 0)),
             blk(lambda i: (i, 0)),
             pl.BlockSpec((tq, tq), lambda i: (0, 0))]
    sb_args = (q, k, v, k, v, gs, m_mat)
    sb_out_specs = (blk(lambda i: (i, 0)), pl.BlockSpec((tq, LANES), lambda i: (i, 0)))
    sb_out_shape = (jax.ShapeDtypeStruct((t, w), BF16), jax.ShapeDtypeStruct((t, LANES), F32))
    a_dec = None
    if fox_dec is not None:
        n_b, kb, nkb = _fox_dec_geometry(fox_dec[0], fox_dec[3])
        if n_b * nkb != nq:
            a_dec = _fox_decode(*fox_dec)
            fox_dec = None
    if fox_dec is None:
        a_fast, carry = pl.pallas_call(
            _sb_fast_body, grid=(nq,), in_specs=sb_in, out_specs=sb_out_specs, out_shape=sb_out_shape,
            compiler_params=_cparams(("parallel",), vmem_mib=48), name="sb_prompt_fast",
        )(*sb_args)
    else:
        dec_in, dec_out, dec_scratch = _fox_dec_specs(fox_dec[0].shape[1], kb,
                                                      lambda i: i // nkb, lambda i: i % nkb)
        n_sb, n_dec = len(sb_in), len(dec_in)

        def fused_body(*refs):
            i = pl.program_id(0)
            outs = refs[n_sb + n_dec:n_sb + n_dec + 3]
            begin, chunks, end = _fox_dec_parts(i % nkb, nkb, *refs[n_sb:n_sb + n_dec], outs[2],
                                                *refs[n_sb + n_dec + 3:])
            slots = {}
            for ci, fn in enumerate(chunks):
                slots.setdefault(((ci + 1) * N_HEADS - 1) // len(chunks), []).append(fn)
            after = {hh: (lambda fns=fns: [fn() for fn in fns]) for hh, fns in slots.items()}
            begin()
            _sb_fast_step(i, *refs[:n_sb], outs[0], outs[1], after_head=after)
            end()

        a_fast, carry, a_dec = pl.pallas_call(
            fused_body, grid=(nq,), in_specs=sb_in + dec_in,
            out_specs=sb_out_specs + (dec_out,),
            out_shape=sb_out_shape + (jax.ShapeDtypeStruct(fox_dec[0].shape, BF16),),
            scratch_shapes=dec_scratch,
            compiler_params=_cparams(("arbitrary",), vmem_mib=56), name="sb_prompt_fox_decode",
        )(*sb_args, *fox_dec)
    if nq <= 2:
        return a_fast, a_dec

    def rest(a_fast, carry):
        kidx = lambda i, s: (jnp.maximum(i - 2 - s, 0), 0)
        return pl.pallas_call(
            _sb_rest_body,
            grid=(nq, nq - 2),
            in_specs=[blk(lambda i, s: (i, 0)), blk(kidx), blk(kidx), blk(lambda i, s: (i, 0)),
                      pl.BlockSpec((tq, LANES), lambda i, s: (i, 0)),
                      blk(lambda i, s: (i, 0)),
                      pl.BlockSpec((tq, tq), lambda i, s: (0, 0))],
            out_specs=blk(lambda i, s: (i, 0)),
            out_shape=jax.ShapeDtypeStruct((t, w), BF16),
            scratch_shapes=[pltpu.VMEM((tq, w), F32), pltpu.VMEM((tq, LANES), F32)],
            compiler_params=_cparams(("parallel", "arbitrary"), vmem_mib=48),
            name="sb_prompt_rest",
        )(q, k, v, gs, carry, a_fast, m_mat)

    need_rest = jnp.max(carry[2 * tq:, :]) > SB_SKIP_LOG2
    return lax.cond(need_rest, rest, lambda a, c: a, a_fast, carry), a_dec


FOX_T = 512
FOX_WIN_TILES = (1, 2, 3, 4, 5, 6)
FOX_ZB_MAX = 40.0
FOX_SKIP_LOG = -104.0


def _head_column(blk, hh):
    lane8 = lax.broadcasted_iota(jnp.int32, blk.shape, 1)
    return jnp.sum(jnp.where(lane8 == hh, blk, 0.0), axis=1, keepdims=True)


BIAS_ONE_LANE = 3 * N_HEADS


def _split_bias_rows(f, lane):
    hi = f.astype(BF16).astype(F32)
    r1 = f - hi
    mid = r1.astype(BF16).astype(F32)
    low = (r1 - mid).astype(BF16).astype(F32)
    out = jnp.where(lane < N_HEADS, hi,
                    jnp.where(lane < 2 * N_HEADS, pltpu.roll(mid, N_HEADS, axis=1),
                              jnp.where(lane < BIAS_ONE_LANE, pltpu.roll(low, 2 * N_HEADS, axis=1),
                                        jnp.where(lane == BIAS_ONE_LANE, 1.0, 0.0))))
    return out.astype(BF16)


def _bias_selectors():
    src = lax.broadcasted_iota(jnp.int32, (N_HEADS, LANES, LANES), 1)
    dst = lax.broadcasted_iota(jnp.int32, (N_HEADS, LANES, LANES), 2)
    head = lax.broadcasted_iota(jnp.int32, (N_HEADS, LANES, LANES), 0)

    def sel(bias_base, one_base):
        term = dst - bias_base
        takes_bias = (term >= 0) & (term < 3) & (src == term * N_HEADS + head)
        takes_one = (dst >= one_base) & (dst < one_base + 3) & (src == BIAS_ONE_LANE)
        return (takes_bias | takes_one).astype(BF16)

    return jnp.stack([sel(0, 3), sel(3, 0)])


def _fox_fast_body(live_ref, q_ref, k_ref, v_ref, qb_ref, kb_ref, sel_ref, gs_ref, a_ref, kaug_s):
    hh = pl.program_id(0)
    qb = pl.program_id(1)
    nq = pl.num_programs(1)
    tb = q_ref.shape[0]
    t = k_ref.shape[0]

    @pl.when(qb == 0)
    def _():
        def build(c, carry):
            start = pl.multiple_of(c * tb, tb)
            kaug_s[pl.ds(start, tb), :] = _dot(kb_ref[pl.ds(start, tb), :], sel_ref[1]).astype(BF16)
            return carry
        lax.fori_loop(0, nq, build, 0)

    q_start = pl.multiple_of(qb * tb, tb)
    q_end = q_start + tb
    q2 = jnp.concatenate([q_ref[...], _dot(qb_ref[...], sel_ref[0]).astype(BF16)], axis=1)
    row = lax.broadcasted_iota(jnp.int32, (tb, 1), 0)
    live = live_ref[hh * nq + qb]

    def run(tiles):
        span = min(tiles * tb, t)
        lane_w = lax.broadcasted_iota(jnp.int32, (span, LANES), 1)
        ones_blk = jnp.where(lane_w == 0, 1.0, 0.0).astype(BF16)
        col = lax.broadcasted_iota(jnp.int32, (1, span), 1)

        def window(j, acc):
            upper = q_end - j * span
            start = pl.multiple_of(jnp.maximum(upper - span, 0), tb)
            k2 = jnp.concatenate([k_ref[pl.ds(start, span), :].astype(BF16),
                                  kaug_s[pl.ds(start, span), :]], axis=1)
            p = jnp.exp2(_dot_nt(q2, k2))
            valid = jnp.logical_and(col - (q_start - start) <= row, col < upper - start)
            p = jnp.where(valid, p, 0.0).astype(BF16)
            v2 = jnp.concatenate([v_ref[pl.ds(start, span), :].astype(BF16), ones_blk], axis=1)
            return acc + _dot(p, v2)

        n_win = (live + (tiles - 1)) // tiles
        acc = lax.fori_loop(0, n_win, window, jnp.zeros((tb, 2 * HEAD_DIM), F32))
        o = acc[:, :HEAD_DIM] / acc[:, HEAD_DIM:HEAD_DIM + 1]
        a_ref[...] = (o * gs_ref[...].astype(F32)).astype(BF16)

    bounds = (0,) + FOX_WIN_TILES
    for lo_t, hi_t in zip(bounds[:-1], bounds[1:]):
        last = hi_t == FOX_WIN_TILES[-1]
        cond = live > lo_t if last else jnp.logical_and(live > lo_t, live <= hi_t)
        pl.when(cond)(functools.partial(run, hi_t))


def _fox_slow_body(q_ref, k_ref, v_ref, fcol_ref, frow_ref, gs_ref, a_ref):
    hh = pl.program_id(0)
    qb = pl.program_id(1)
    tb = q_ref.shape[0]
    q = q_ref[...]
    fq = _head_column(fcol_ref[...], hh) * LOG2E

    def scores(kb):
        start = pl.multiple_of(kb * tb, tb)
        k = k_ref[pl.ds(start, tb), :].astype(BF16)
        fk = frow_ref[0, :, pl.ds(start, tb)] * LOG2E
        return _dot_nt(q, k) + (fq - fk), v_ref[pl.ds(start, tb), :].astype(BF16)

    def update(s, v, carry):
        m, l, acc = carry
        m_new = jnp.maximum(m, jnp.max(s, axis=1, keepdims=True))
        alpha = jnp.exp2(m - m_new)
        p = jnp.exp2(s - m_new)
        l = alpha * l + jnp.sum(p, axis=1, keepdims=True)
        acc = alpha * acc + _dot(p.astype(BF16), v)
        return m_new, l, acc

    def body(kb, carry):
        s, v = scores(kb)
        return update(s, v, carry)

    init = (jnp.full((tb, 1), NEG_BIG, F32), jnp.zeros((tb, 1), F32), jnp.zeros((tb, HEAD_DIM), F32))
    carry = lax.fori_loop(0, qb, body, init)
    s, v = scores(qb)
    row = lax.broadcasted_iota(jnp.int32, (tb, tb), 0)
    col = lax.broadcasted_iota(jnp.int32, (tb, tb), 1)
    s = jnp.where(col <= row, s, NEG_BIG)
    _, l, acc = update(s, v, carry)
    a_ref[...] = ((acc / l) * gs_ref[...].astype(F32)).astype(BF16)


def _fox_prompt(q, k, v, f_col, f_row, gs, zb, q_bias, k_bias):
    t, w = q.shape
    tb = min(FOX_T, t)
    nq = t // tb
    out_shape = jax.ShapeDtypeStruct((t, w), BF16)

    def fast(q, k, v, f_col, f_row, gs, q_bias, k_bias):
        f_start = f_row[:, ::tb]
        f_end = f_row[:, tb - 1::tb]
        dead = (f_start[:, :, None] - f_end[:, None, :]) < FOX_SKIP_LOG
        lo = jnp.sum(dead, axis=2).astype(jnp.int32)
        live = (jnp.arange(1, nq + 1, dtype=jnp.int32)[None, :] - lo).reshape(-1)
        head_blk = lambda h, i, live: (i, h)
        head_all = lambda h, i, live: (0, h)
        grid_spec = pltpu.PrefetchScalarGridSpec(
            num_scalar_prefetch=1,
            grid=(N_HEADS, nq),
            in_specs=[pl.BlockSpec((tb, HEAD_DIM), head_blk),
                      pl.BlockSpec((t, HEAD_DIM), head_all),
                      pl.BlockSpec((t, HEAD_DIM), head_all),
                      pl.BlockSpec((tb, LANES), lambda h, i, live: (i, 0)),
                      pl.BlockSpec((t, LANES), lambda h, i, live: (0, 0), pipeline_mode=pl.Buffered(1)),
                      pl.BlockSpec((2, None, LANES, LANES), lambda h, i, live: (0, h, 0, 0)),
                      pl.BlockSpec((tb, HEAD_DIM), head_blk)],
            out_specs=pl.BlockSpec((tb, HEAD_DIM), head_blk),
            scratch_shapes=[pltpu.VMEM((t, LANES), BF16)])
        return pl.pallas_call(
            _fox_fast_body, grid_spec=grid_spec, out_shape=out_shape,
            compiler_params=_cparams(("parallel", "arbitrary"), vmem_mib=48),
            name="fox_prompt_fast",
        )(live, q, k, v, q_bias, k_bias, _bias_selectors(), gs)

    def slow(q, k, v, f_col, f_row, gs, q_bias, k_bias):
        f_row3 = f_row.reshape(N_HEADS, 1, t)
        return pl.pallas_call(
            _fox_slow_body,
            grid=(N_HEADS, nq),
            in_specs=[pl.BlockSpec((tb, HEAD_DIM), lambda h, i: (i, h)),
                      pl.BlockSpec((t, HEAD_DIM), lambda h, i: (0, h)),
                      pl.BlockSpec((t, HEAD_DIM), lambda h, i: (0, h)),
                      pl.BlockSpec((tb, N_HEADS), lambda h, i: (i, 0)),
                      pl.BlockSpec((1, 1, t), lambda h, i: (h, 0, 0)),
                      pl.BlockSpec((tb, HEAD_DIM), lambda h, i: (i, h))],
            out_specs=pl.BlockSpec((tb, HEAD_DIM), lambda h, i: (i, h)),
            out_shape=out_shape,
            compiler_params=_cparams(("parallel", "parallel"), vmem_mib=48),
            name="fox_prompt_slow",
        )(q, k, v, f_col, f_row3, gs)

    return lax.cond(zb <= FOX_ZB_MAX, fast, slow, q, k, v, f_col, f_row, gs, q_bias, k_bias)


DEC_KB = 2048
SB_DEC_WIN = 256


def _own_head_mask():
    row = lax.broadcasted_iota(jnp.int32, (SUBLANES, W_ATT), 0)
    col = lax.broadcasted_iota(jnp.int32, (SUBLANES, W_ATT), 1)
    return (col // HEAD_DIM) == row


def _expand_q(q_ref, qx_s):
    n_tok = q_ref.shape[0]
    qf = q_ref[...].astype(F32)
    own = _own_head_mask()
    for tkn in range(n_tok):
        rep = jnp.broadcast_to(qf[tkn:tkn + 1, :], (SUBLANES, W_ATT))
        qx_s[tkn * N_HEADS:(tkn + 1) * N_HEADS, :] = jnp.where(own, rep, 0.0)


def _pad_new(kn_ref, vn_ref, knp_s, vnp_s):
    n_tok = kn_ref.shape[0]
    knp_s[...] = jnp.zeros_like(knp_s)
    vnp_s[...] = jnp.zeros_like(vnp_s)
    knp_s[0:n_tok, :] = kn_ref[...]
    vnp_s[0:n_tok, :] = vn_ref[...]


def _gather_heads(c_ref, dst_s):
    n_keys = dst_s.shape[0]
    for hh in range(N_HEADS):
        dst_s[:, hh * HEAD_DIM:(hh + 1) * HEAD_DIM] = (
            c_ref[0, pl.ds(hh, n_keys, stride=N_HEADS), :].astype(BF16))


def _collapse_heads(o_full, out_s):
    n_tok = o_full.shape[0] // N_HEADS
    own = _own_head_mask()
    for tkn in range(n_tok):
        blk = jnp.where(own, o_full[tkn * N_HEADS:(tkn + 1) * N_HEADS, :], 0.0)
        out_s[tkn:tkn + 1, :] = jnp.sum(blk, axis=0, keepdims=True)


def _sb_dec_fast_body(q_ref, kn_ref, vn_ref, kc_ref, vc_ref, gs_ref, mn_ref, mc_ref, a_ref, c_ref,
                      qx_s, knp_s, vnp_s, kx_s, vx_s, out_s):
    n_rows = qx_s.shape[0]
    _expand_q(q_ref, qx_s)
    _pad_new(kn_ref, vn_ref, knp_s, vnp_s)
    qx = qx_s[...].astype(BF16)
    row = lax.broadcasted_iota(jnp.int32, (n_rows, LANES), 0)
    col = lax.broadcasted_iota(jnp.int32, (n_rows, LANES), 1)
    mask = col < row // N_HEADS
    o_new, carry = _sb_block(qx, knp_s[...], vnp_s[...], mn_ref[...], None, mask)
    _gather_heads(kc_ref, kx_s)
    _gather_heads(vc_ref, vx_s)
    o_win, carry = _sb_block(qx, kx_s[...], vx_s[...], mc_ref[...], carry, None)
    _collapse_heads(o_new + o_win, out_s)
    a_ref[...] = (out_s[...] * gs_ref[...].astype(F32)).astype(BF16)
    c_ref[...] = jnp.broadcast_to(carry, c_ref.shape)


def _sb_dec_rest_body(q_ref, kc_ref, vc_ref, gs_ref, cin_ref, afast_ref, mc_ref, a_ref,
                      qx_s, kx_s, vx_s, acc_s, c_s, out_s):
    s = pl.program_id(1)

    @pl.when(s == 0)
    def _():
        _expand_q(q_ref, qx_s)
        acc_s[...] = jnp.zeros_like(acc_s)
        c_s[...] = cin_ref[:, 0:1]

    @pl.when(jnp.max(c_s[...]) > SB_SKIP_LOG2)
    def _():
        _gather_heads(kc_ref, kx_s)
        _gather_heads(vc_ref, vx_s)
        o, carry = _sb_block(qx_s[...].astype(BF16), kx_s[...], vx_s[...], mc_ref[...], c_s[...], None)
        acc_s[...] += o
        c_s[...] = carry

    @pl.when(s == pl.num_programs(1) - 1)
    def _():
        _collapse_heads(acc_s[...], out_s)
        a_ref[...] = (afast_ref[...].astype(F32) + out_s[...] * gs_ref[...].astype(F32)).astype(BF16)


def _sb_decode(q, k_new, v_new, k_cache, v_cache, gs, dec_seq, host=None):
    n_rows_all, w = q.shape
    n_b = n_rows_all // dec_seq
    p_len = k_cache.shape[1] // N_HEADS
    win = min(SB_DEC_WIN, p_len)
    n_win = p_len // win
    n_rows = dec_seq * N_HEADS
    m_new = _tri(LANES, "row_gt_col")
    m_win = _tri(win, "row_gt_col")
    tok = lambda b: (b, 0)
    newest = pl.BlockSpec((1, win * N_HEADS, HEAD_DIM), lambda b: (b, n_win - 1, 0))
    rider = dict(
        body=_sb_dec_fast_body, n_steps=n_b,
        in_specs=[pl.BlockSpec((dec_seq, w), tok), pl.BlockSpec((dec_seq, w), tok),
                  pl.BlockSpec((dec_seq, w), tok),
                  newest, newest,
                  pl.BlockSpec((dec_seq, w), tok),
                  pl.BlockSpec((LANES, LANES), lambda b: (0, 0)),
                  pl.BlockSpec((win, win), lambda b: (0, 0))],
        args=(q, k_new, v_new, k_cache, v_cache, gs, m_new, m_win),
        out_specs=(pl.BlockSpec((dec_seq, w), tok), pl.BlockSpec((n_rows, LANES), tok)),
        out_shape=(jax.ShapeDtypeStruct((n_rows_all, w), BF16),
                   jax.ShapeDtypeStruct((n_b * n_rows, LANES), F32)),
        scratch=[pltpu.VMEM((n_rows, w), F32), pltpu.VMEM((LANES, w), BF16),
                 pltpu.VMEM((LANES, w), BF16), pltpu.VMEM((win, w), BF16),
                 pltpu.VMEM((win, w), BF16), pltpu.VMEM((dec_seq, w), F32)])
    hosted = None
    if host is not None:
        hosted, a_fast, carry = host(rider)
    else:
        a_fast, carry = pl.pallas_call(
            rider["body"], grid=(n_b,), in_specs=rider["in_specs"], out_specs=rider["out_specs"],
            out_shape=rider["out_shape"], scratch_shapes=rider["scratch"],
            compiler_params=_cparams(("parallel",), vmem_mib=48), name="sb_decode_fast",
        )(*rider["args"])
    if n_win <= 1:
        return a_fast, hosted

    def rest(a_fast, carry):
        tok2 = lambda b, s: (b, 0)
        older = pl.BlockSpec((1, win * N_HEADS, HEAD_DIM), lambda b, s: (b, n_win - 2 - s, 0))
        return pl.pallas_call(
            _sb_dec_rest_body,
            grid=(n_b, n_win - 1),
            in_specs=[pl.BlockSpec((dec_seq, w), tok2),
                      older, older,
                      pl.BlockSpec((dec_seq, w), tok2),
                      pl.BlockSpec((n_rows, LANES), tok2),
                      pl.BlockSpec((dec_seq, w), tok2),
                      pl.BlockSpec((win, win), lambda b, s: (0, 0))],
            out_specs=pl.BlockSpec((dec_seq, w), tok2),
            out_shape=jax.ShapeDtypeStruct((n_rows_all, w), BF16),
            scratch_shapes=[pltpu.VMEM((n_rows, w), F32), pltpu.VMEM((win, w), BF16),
                            pltpu.VMEM((win, w), BF16), pltpu.VMEM((n_rows, w), F32),
                            pltpu.VMEM((n_rows, 1), F32), pltpu.VMEM((dec_seq, w), F32)],
            compiler_params=_cparams(("parallel", "arbitrary"), vmem_mib=48),
            name="sb_decode_rest",
        )(q, k_cache, v_cache, gs, carry, a_fast, m_win)

    need_rest = jnp.max(carry) > SB_SKIP_LOG2
    return lax.cond(need_rest, rest, lambda a, c: a, a_fast, carry), hosted


def _fox_dec_body(*refs):
    _fox_dec_step(pl.program_id(1), pl.num_programs(1), *refs)


FOX_DEC_CHUNK = 1024


def _fox_dec_step(s, n_s, *refs):
    begin, chunks, end = _fox_dec_parts(s, n_s, *refs)
    begin()
    for chunk_fn in chunks:
        chunk_fn()
    end()


def _fox_dec_parts(s, n_s, q_ref, kn_ref, vn_ref, kc_ref, vc_ref, gs_ref, c_ref, r_ref, a_ref,
                   acc_s, m_s, l_s):
    n_rows = q_ref.shape[1]
    q = q_ref[0]
    own = lax.broadcasted_iota(jnp.int32, (SUBLANES, LANES), 0) == \
        lax.broadcasted_iota(jnp.int32, (SUBLANES, LANES), 1) % N_HEADS
    head_bias = jnp.where(own, 0.0, NEG_BIG)

    def update(sc, v):
        m = m_s[...]
        m_new = jnp.maximum(m, jnp.max(sc, axis=1, keepdims=True))
        alpha = jnp.exp2(m - m_new)
        p = jnp.exp2(sc - m_new)
        l_s[...] = alpha * l_s[...] + jnp.sum(p, axis=1, keepdims=True)
        acc_s[...] = alpha * acc_s[...] + _dot(p.astype(BF16), v)
        m_s[...] = m_new

    def add_key_bias(z, decay_row):
        n = decay_row.shape[1]
        tile = jnp.concatenate([head_bias] * (n // LANES), axis=1) + decay_row * LOG2E
        return (z.reshape(n_rows // SUBLANES, SUBLANES, n) + tile[None]).reshape(n_rows, n)

    def begin():
        @pl.when(s == 0)
        def _():
            m_s[...] = jnp.full_like(m_s, NEG_BIG)
            l_s[...] = jnp.zeros_like(l_s)
            acc_s[...] = jnp.zeros_like(acc_s)
            row = lax.broadcasted_iota(jnp.int32, (n_rows, LANES), 0)
            col = lax.broadcasted_iota(jnp.int32, (n_rows, LANES), 1)
            sc = add_key_bias(_dot_nt(q, kn_ref[0]), -c_ref[0])
            sc = jnp.where(col // N_HEADS <= row // N_HEADS, sc, NEG_BIG)
            update(sc, vn_ref[0])

    n_keys_rows = kc_ref.shape[1]
    chunk = min(FOX_DEC_CHUNK * N_HEADS, n_keys_rows)

    def one_chunk(c0):
        k2 = kc_ref[0, c0:c0 + chunk, :].astype(BF16)
        v2 = vc_ref[0, c0:c0 + chunk, :].astype(BF16)
        update(add_key_bias(_dot_nt(q, k2), r_ref[0, :, c0:c0 + chunk]), v2)

    def end():
        @pl.when(s == n_s - 1)
        def _():
            a_ref[0] = ((acc_s[...] / l_s[...]) * gs_ref[0].astype(F32)).astype(BF16)

    return begin, [functools.partial(one_chunk, c0) for c0 in range(0, n_keys_rows, chunk)], end


def _fox_dec_geometry(q3, k_cache):
    n_b = q3.shape[0]
    p_len = k_cache.shape[1] // N_HEADS
    kb = min(DEC_KB, p_len)
    return n_b, kb, p_len // kb


def _fox_dec_specs(n_rows, kb, stream, block):
    tok = pl.BlockSpec((1, n_rows, HEAD_DIM), lambda *g: (stream(*g), 0, 0))
    cache = pl.BlockSpec((1, kb * N_HEADS, HEAD_DIM), lambda *g: (stream(*g), block(*g), 0))
    in_specs = [tok, tok, tok, cache, cache, tok,
                pl.BlockSpec((1, 1, LANES), lambda *g: (stream(*g), 0, 0)),
                pl.BlockSpec((1, 1, kb * N_HEADS), lambda *g: (stream(*g), 0, block(*g)))]
    scratch = [pltpu.VMEM((n_rows, HEAD_DIM), F32), pltpu.VMEM((n_rows, 1), F32),
               pltpu.VMEM((n_rows, 1), F32)]
    return in_specs, tok, scratch


def _fox_decode(q3, kn3, vn3, k_cache, v_cache, gs3, c_row, r_row):
    n_b, kb, nkb = _fox_dec_geometry(q3, k_cache)
    in_specs, out_spec, scratch = _fox_dec_specs(q3.shape[1], kb, lambda b, s: b, lambda b, s: s)
    return pl.pallas_call(
        _fox_dec_body,
        grid=(n_b, nkb),
        in_specs=in_specs,
        out_specs=out_spec,
        out_shape=jax.ShapeDtypeStruct(q3.shape, BF16),
        scratch_shapes=scratch,
        compiler_params=_cparams(("parallel", "arbitrary"), vmem_mib=56),
        name="fox_decode",
    )(q3, kn3, vn3, k_cache, v_cache, gs3, c_row, r_row)


POST_TM = 512


def _post_body(x_ref, asb_ref, afx_ref, msb_ref, mfx_ref, wsb_ref, wfx_ref, wo_ref, y_ref):
    u_sb = _dot(asb_ref[...], wsb_ref[...])
    u_fx = _dot(afx_ref[...], wfx_ref[...])
    merged = msb_ref[...].astype(F32) * u_sb + mfx_ref[...].astype(F32) * u_fx
    y_ref[...] = x_ref[...] + _dot(merged.astype(BF16), wo_ref[...])


def _post(x2d, a_sb, a_fx, m_sig, w_sb, w_fx, w_o, rider=None):
    m, d = x2d.shape
    w = a_sb.shape[1]
    tm = min(m, POST_TM)
    row = lambda i: (i, 0)
    const = lambda i: (0, 0)
    resident = functools.partial(pl.BlockSpec, index_map=const, pipeline_mode=pl.Buffered(1))
    in_specs = [pl.BlockSpec((tm, d), row), pl.BlockSpec((tm, w), row), pl.BlockSpec((tm, w), row),
                pl.BlockSpec((tm, d), lambda i: (i, 0)), pl.BlockSpec((tm, d), lambda i: (i, 1)),
                resident((w, d)), resident((w, d)), resident((d, d))]
    args = (x2d, a_sb, a_fx, m_sig, m_sig, w_sb, w_fx, w_o)
    y_spec = pl.BlockSpec((tm, d), row)
    y_shape = jax.ShapeDtypeStruct((m, d), F32)
    if rider is None:
        return pl.pallas_call(
            _post_body, grid=(m // tm,), in_specs=in_specs, out_specs=y_spec, out_shape=y_shape,
            compiler_params=_cparams(("parallel",), vmem_mib=56), name="post",
        )(*args)
    assert rider["n_steps"] == m // tm
    n_in, n_rin, n_rout = len(in_specs), len(rider["in_specs"]), len(rider["out_specs"])

    def body(*refs):
        rider_out = refs[n_in + n_rin + 1:n_in + n_rin + 1 + n_rout]
        _post_body(*refs[:n_in], refs[n_in + n_rin])
        rider["body"](*refs[n_in:n_in + n_rin], *rider_out, *refs[n_in + n_rin + 1 + n_rout:])

    return pl.pallas_call(
        body, grid=(m // tm,), in_specs=in_specs + list(rider["in_specs"]),
        out_specs=(y_spec,) + tuple(rider["out_specs"]),
        out_shape=(y_shape,) + tuple(rider["out_shape"]),
        scratch_shapes=rider["scratch"],
        compiler_params=_cparams(("arbitrary",), vmem_mib=58), name="post_with_rider",
    )(*args, *rider["args"])


def _project_all(hp, hs, w, qnw, knw):
    seg = lambda c: c * W_ATT
    q_sb = _proj(hp, hs, w, seg(0), "scale")
    k_sb = _proj(hp, hs, w, seg(1), "kv")
    v_sb = _proj(hp, hs, w, seg(2), "kv")
    g_sb = _proj(hp, hs, w, seg(3), "silu")
    q_fx = _proj(hp, hs, w, seg(4), "qnorm", nw=qnw)
    k_fx = _proj(hp, hs, w, seg(5), "knorm", nw=knw)
    v_fx = _proj(hp, hs, w, seg(6), "kv")
    g_fx = _proj(hp, hs, w, seg(7), "silu")
    m_sig = _proj(hp, hs, w, seg(8) + N_HEADS, "sigmoid", n_tiles=2 * D_MODEL // W_ATT)
    calls = (q_sb, k_sb, v_sb, g_sb, q_fx, k_fx, v_fx, g_fx, m_sig)
    prompt = tuple(o for r in calls for o in r[:len(r) // 2])
    decode = tuple(o for r in calls for o in r[len(r) // 2:])
    return prompt, decode


def kernel(x_prompt, x_sample, cache_sb_k, cache_sb_v, cache_fox_k, cache_fox_v, cache_fox_logf,
           norm_w, w_in, b_forget, q_norm_w, k_norm_w, w_branch_sb, w_branch_fox, w_out):
    depth = norm_w.shape[0]
    assert depth == 1, "single-layer step"
    bsz, seq, d = x_prompt.shape
    assert bsz == 1
    n_dec, dec_seq, _ = x_sample.shape
    p_len = cache_sb_k.shape[2]
    n_main = 8 * W_ATT

    w_t = jnp.swapaxes(w_in, 1, 2)
    b_row = jnp.pad(b_forget[0].astype(F32)[None, :], ((0, 0), (0, LANES - N_HEADS)))
    qnw = jnp.tile(q_norm_w[0].astype(F32), N_HEADS)[None, :]
    knw = jnp.tile(k_norm_w[0].astype(F32), N_HEADS)[None, :]
    nw_row = norm_w[0].astype(F32)[None, :]
    w_sb = w_branch_sb[0].astype(BF16)
    w_fx = w_branch_fox[0].astype(BF16)
    w_o = w_out[0].astype(BF16)

    xp = x_prompt.reshape(seq, d)
    xs = x_sample.reshape(n_dec * dec_seq, d)
    zb = (1.02 * HEAD_DIM * QK_SCALE) * jnp.max(jnp.abs(q_norm_w[0])) * jnp.max(jnp.abs(k_norm_w[0]))
    zb = zb.astype(F32)
    hp, logf, f_col, f_row, q_bias, k_bias = _norm_logf_prompt(
        xp, nw_row, w_t, n_main, b_row, (zb * LOG2E).reshape(1, 1))
    hs = _rmsnorm(xs, nw_row)
    prompt_proj, decode_proj = _project_all(hp, hs, w_t, qnw, knw)

    q_sb, k_sb, v_sb, g_sb, q_fx, k_fx, v_fx, g_fx, m_sig = prompt_proj
    sq_sb, sk_sb, sv_sb, sg_sb, sq_fx, sk_fx, sv_fx, sg_fx, sm_sig = decode_proj
    k_sb_b, v_sb_b, k_fx_b, v_fx_b = k_sb, v_sb, k_fx, v_fx
    sk_sb_b, sv_sb_b, sk_fx_b, sv_fx_b = (a.astype(BF16) for a in (sk_sb, sv_sb, sk_fx, sv_fx))

    past_logf_t = jnp.transpose(cache_fox_logf[0].astype(F32), (0, 2, 1)).reshape(n_dec * N_HEADS, p_len)
    s_logf, c_all, r_past = _logf_sample(hs, w_t, n_main, b_row, past_logf_t, dec_seq)
    n_th = dec_seq * N_HEADS
    c_row = jnp.transpose(c_all.reshape(N_HEADS, n_dec, dec_seq), (1, 2, 0)).reshape(n_dec, 1, n_th)
    r_row = jnp.transpose(r_past.reshape(n_dec, N_HEADS, p_len), (0, 2, 1)).reshape(n_dec, 1, p_len * N_HEADS)
    by_head = lambda a: a.reshape(n_dec, n_th, HEAD_DIM)
    kc_sb = cache_sb_k[0].reshape(n_dec, p_len * N_HEADS, HEAD_DIM)
    vc_sb = cache_sb_v[0].reshape(n_dec, p_len * N_HEADS, HEAD_DIM)
    kc_fx = cache_fox_k[0].reshape(n_dec, p_len * N_HEADS, HEAD_DIM)
    vc_fx = cache_fox_v[0].reshape(n_dec, p_len * N_HEADS, HEAD_DIM)

    a_sb, sa_fx = _sb_prompt(q_sb, k_sb_b, v_sb_b, g_sb,
                             fox_dec=(by_head(sq_fx), by_head(sk_fx_b), by_head(sv_fx_b), kc_fx, vc_fx,
                                      by_head(sg_fx), c_row, r_row))
    sa_fx = sa_fx.reshape(n_dec * dec_seq, W_ATT)
    a_fx = _fox_prompt(q_fx, k_fx_b, v_fx_b, f_col, f_row, g_fx, zb, q_bias, k_bias)
    post_steps = seq // min(seq, POST_TM)

    def host(rider):
        y, a_fast, carry = _post(xp, a_sb, a_fx, m_sig, w_sb, w_fx, w_o, rider=rider)
        return y, a_fast, carry

    sa_sb, y_prompt = _sb_decode(sq_sb, sk_sb_b, sv_sb_b, kc_sb, vc_sb, sg_sb, dec_seq,
                                 host=host if post_steps == n_dec else None)
    if y_prompt is None:
        y_prompt = _post(xp, a_sb, a_fx, m_sig, w_sb, w_fx, w_o)
    y_prompt = y_prompt.reshape(bsz, seq, d)
    y_sample = _post(xs, sa_sb, sa_fx, sm_sig, w_sb, w_fx, w_o).reshape(n_dec, dec_seq, d)

    hd = (N_HEADS, HEAD_DIM)
    return (y_prompt, y_sample,
            k_sb.reshape(1, bsz, seq, *hd), v_sb.reshape(1, bsz, seq, *hd),
            k_fx.reshape(1, bsz, seq, *hd), v_fx.reshape(1, bsz, seq, *hd),
            logf.reshape(1, bsz, seq, N_HEADS),
            sk_sb.reshape(1, n_dec, dec_seq, *hd), sv_sb.reshape(1, n_dec, dec_seq, *hd),
            sk_fx.reshape(1, n_dec, dec_seq, *hd), sv_fx.reshape(1, n_dec, dec_seq, *hd),
            s_logf.reshape(1, n_dec, dec_seq, N_HEADS))
```

```python
import functools

import jax
import jax.numpy as jnp
from jax import lax
from jax.experimental import pallas as pl
from jax.experimental.pallas import tpu as pltpu

F32 = jnp.float32
BF16 = jnp.bfloat16

D_MODEL = 2048
N_HEADS = 8
HEAD_DIM = 128
W_ATT = N_HEADS * HEAD_DIM
RMS_EPS = 1e-6
QK_SCALE = HEAD_DIM ** -0.5
LOG2E = 1.4426950408889634
LANES = 128
SUBLANES = 8
SB_SKIP_LOG2 = -150.0
NEG_BIG = -1e30
MIB = 1024 * 1024


def _cparams(sem, vmem_mib=None):
    kw = dict(dimension_semantics=sem)
    if vmem_mib is not None:
        kw["vmem_limit_bytes"] = vmem_mib * MIB
    return pltpu.CompilerParams(**kw)


def _softplus_neg_abs(z):
    return jnp.log1p(jnp.exp(-jnp.abs(z)))


def _log_sigmoid(z):
    return jnp.minimum(z, 0.0) - _softplus_neg_abs(z)


def _split_bf16(x, n):
    parts = []
    r = x
    for _ in range(n - 1):
        p = r.astype(BF16)
        parts.append(p)
        r = r - p.astype(F32)
    parts.append(r.astype(BF16))
    return parts


def _dot(a, b):
    return jnp.dot(a, b, preferred_element_type=F32)


def _dot_nt(a, b):
    return lax.dot_general(a, b, (((1,), (1,)), ((), ())), preferred_element_type=F32)


def _dot_split_lhs(x, m, n):
    acc = None
    for p in _split_bf16(x, n):
        t = _dot(p, m)
        acc = t if acc is None else acc + t
    return acc


def _dot_split_rhs(m, x, n):
    acc = None
    for p in _split_bf16(x, n):
        t = _dot(m, p)
        acc = t if acc is None else acc + t
    return acc


def _tri(n, kind):
    r = lax.broadcasted_iota(jnp.int32, (n, n), 0)
    c = lax.broadcasted_iota(jnp.int32, (n, n), 1)
    if kind == "row_gt_col":
        m = r > c
    elif kind == "row_le_col":
        m = r <= c
    elif kind == "row_ge_col":
        m = r >= c
    else:
        raise ValueError(kind)
    return m.astype(BF16)


def _rmsnorm_body(x_ref, w_ref, o_ref):
    x = x_ref[...]
    ms = jnp.mean(x * x, axis=-1, keepdims=True)
    o_ref[...] = (x * lax.rsqrt(ms + RMS_EPS) * w_ref[...]).astype(o_ref.dtype)


def _rmsnorm(x2d, w_row):
    m, d = x2d.shape
    tm = min(m, 512)
    return pl.pallas_call(
        _rmsnorm_body,
        grid=(m // tm,),
        in_specs=[pl.BlockSpec((tm, d), lambda i: (i, 0)),
                  pl.BlockSpec((1, d), lambda i: (0, 0))],
        out_specs=pl.BlockSpec((tm, d), lambda i: (i, 0)),
        out_shape=jax.ShapeDtypeStruct((m, d), BF16),
        compiler_params=_cparams(("parallel",)),
        name="rmsnorm",
    )(x2d, w_row)


def _head_rmsnorm(acc, nw):
    parts = []
    for hh in range(N_HEADS):
        a = acc[:, hh * HEAD_DIM:(hh + 1) * HEAD_DIM]
        ms = jnp.mean(a * a, axis=-1, keepdims=True)
        parts.append(a * lax.rsqrt(ms + RMS_EPS))
    return jnp.concatenate(parts, axis=1) * nw


def _proj_epilogue(acc, kind, nw_ref, outs):
    if kind == "scale":
        outs[0][...] = (acc * (QK_SCALE * LOG2E)).astype(BF16)
    elif kind == "kv":
        outs[0][...] = acc
        outs[1][...] = acc.astype(BF16)
    elif kind == "silu":
        outs[0][...] = (acc * jax.nn.sigmoid(acc)).astype(BF16)
    elif kind == "sigmoid":
        outs[0][...] = jax.nn.sigmoid(acc).astype(BF16)
    elif kind == "qnorm":
        outs[0][...] = (_head_rmsnorm(acc, nw_ref[...]) * (QK_SCALE * LOG2E)).astype(BF16)
    elif kind == "knorm":
        y = _head_rmsnorm(acc, nw_ref[...])
        outs[0][...] = y
        outs[1][...] = y.astype(BF16)
    else:
        raise ValueError(kind)


def _proj_body(*refs, kind, n_out, shift):
    hp_ref, hs_ref, w_ref = refs[:3]
    pos = 3
    wn_ref = nw_ref = None
    if shift:
        wn_ref = refs[pos]
        pos += 1
    if kind in ("qnorm", "knorm"):
        nw_ref = refs[pos]
        pos += 1
    outs_p = refs[pos:pos + n_out]
    outs_s = refs[pos + n_out:pos + 2 * n_out]
    wb_s = refs[-1]
    i = pl.program_id(1)
    n_prompt = pl.num_programs(1) - 1

    @pl.when(i == 0)
    def _():
        if shift:
            wt = jnp.concatenate([w_ref[shift:, :], wn_ref[...]], axis=0)
        else:
            wt = w_ref[...]
        wb_s[...] = wt.T.astype(BF16)

    @pl.when(i < n_prompt)
    def _():
        _proj_epilogue(_dot(hp_ref[...], wb_s[...]), kind, nw_ref, outs_p)

    @pl.when(i == n_prompt)
    def _():
        _proj_epilogue(_dot(hs_ref[...], wb_s[...]), kind, nw_ref, outs_s)


def _proj(hp, hs, wt, col0, kind, nw=None, n_tiles=1):
    mp, d = hp.shape
    ms = hs.shape[0]
    tn = W_ATT
    dtypes = (F32, BF16) if kind in ("kv", "knorm") else (BF16,)
    tm = min(mp, 1024)
    n_prompt = mp // tm
    blk0 = col0 // tn
    shift = col0 - blk0 * tn
    assert shift in (0, SUBLANES)
    grid = (n_tiles, n_prompt + 1)
    prow = lambda j, i: (jnp.minimum(i, n_prompt - 1), 0)
    w_mode = {} if n_tiles > 1 else dict(pipeline_mode=pl.Buffered(1))
    in_specs = [pl.BlockSpec((tm, d), prow),
                pl.BlockSpec((ms, d), lambda j, i: (0, 0)),
                pl.BlockSpec((None, tn, d), lambda j, i: (0, blk0 + j, 0), **w_mode)]
    args = [hp, hs, wt]
    if shift:
        per = tn // shift
        in_specs.append(pl.BlockSpec((None, shift, d), lambda j, i: (0, (blk0 + j + 1) * per, 0)))
        args.append(wt)
    if kind in ("qnorm", "knorm"):
        in_specs.append(pl.BlockSpec((1, tn), lambda j, i: (0, 0)))
        args.append(nw)
    n_cols = tn * n_tiles
    p_block = pl.BlockSpec((tm, tn), lambda j, i: (jnp.minimum(i, n_prompt - 1), j))
    s_block = pl.BlockSpec((ms, tn), lambda j, i: (0, j))
    out_shape = tuple(jax.ShapeDtypeStruct((mp, n_cols), t) for t in dtypes) + \
        tuple(jax.ShapeDtypeStruct((ms, n_cols), t) for t in dtypes)
    out_specs = (p_block,) * len(dtypes) + (s_block,) * len(dtypes)
    return pl.pallas_call(
        functools.partial(_proj_body, kind=kind, n_out=len(dtypes), shift=shift),
        grid=grid,
        in_specs=in_specs,
        out_specs=out_specs,
        out_shape=out_shape,
        scratch_shapes=[pltpu.VMEM((d, tn), BF16)],
        compiler_params=_cparams(("parallel", "arbitrary"), vmem_mib=56),
        name="proj_" + kind,
    )(*args)


def _forget_weight(wft_ref):
    wft = wft_ref[...]
    pad = jnp.zeros((LANES - wft.shape[0], wft.shape[1]), F32)
    return jnp.concatenate([wft, pad], axis=0).T.astype(BF16)


def _norm_logf_body(x_ref, nw_ref, wft_ref, brow_ref, l_ref, c2_ref,
                    h_ref, lf_ref, fcol_ref, frow_ref, qbias_ref, kbias_ref, ccol_s, wf_s):
    i = pl.program_id(0)
    tm = x_ref.shape[0]

    @pl.when(i == 0)
    def _():
        ccol_s[...] = jnp.zeros_like(ccol_s)
        wf_s[...] = _forget_weight(wft_ref)

    x = x_ref[...]
    ms = jnp.mean(x * x, axis=-1, keepdims=True)
    h = (x * lax.rsqrt(ms + RMS_EPS) * nw_ref[...]).astype(BF16)
    h_ref[...] = h
    lf = _log_sigmoid(_dot(h, wf_s[...]) + brow_ref[...])
    lf_ref[...] = lf[:, :N_HEADS]
    f_col = _dot_split_rhs(l_ref[...], lf, 3) + ccol_s[...]
    fcol_ref[...] = f_col[:, :N_HEADS]
    ccol_s[...] = f_col[tm - 1:tm, :]
    frow_ref[...] = f_col.T[:N_HEADS, :]
    lane = lax.broadcasted_iota(jnp.int32, f_col.shape, 1)
    f2 = f_col * LOG2E
    qbias_ref[...] = _split_bias_rows(f2 - c2_ref[...], lane)
    kbias_ref[...] = _split_bias_rows(-f2, lane)


def _norm_logf_prompt(x2d, nw_row, wt, f_row0, b_row, c2):
    m, d = x2d.shape
    tm = min(m, 512)
    l_mat = _tri(tm, "row_ge_col")
    const = lambda i: (0, 0)
    return pl.pallas_call(
        _norm_logf_body,
        grid=(m // tm,),
        in_specs=[pl.BlockSpec((tm, d), lambda i: (i, 0)),
                  pl.BlockSpec((1, d), const),
                  pl.BlockSpec((None, N_HEADS, d), lambda i: (0, f_row0 // N_HEADS, 0)),
                  pl.BlockSpec((1, LANES), const),
                  pl.BlockSpec((tm, tm), const),
                  pl.BlockSpec((1, 1), const)],
        out_specs=(pl.BlockSpec((tm, d), lambda i: (i, 0)),
                   pl.BlockSpec((tm, N_HEADS), lambda i: (i, 0)),
                   pl.BlockSpec((tm, N_HEADS), lambda i: (i, 0)),
                   pl.BlockSpec((N_HEADS, tm), lambda i: (0, i)),
                   pl.BlockSpec((tm, LANES), lambda i: (i, 0)),
                   pl.BlockSpec((tm, LANES), lambda i: (i, 0))),
        out_shape=(jax.ShapeDtypeStruct((m, d), BF16),
                   jax.ShapeDtypeStruct((m, N_HEADS), F32),
                   jax.ShapeDtypeStruct((m, N_HEADS), F32),
                   jax.ShapeDtypeStruct((N_HEADS, m), F32),
                   jax.ShapeDtypeStruct((m, LANES), BF16),
                   jax.ShapeDtypeStruct((m, LANES), BF16)),
        scratch_shapes=[pltpu.VMEM((1, LANES), F32), pltpu.VMEM((d, LANES), BF16)],
        compiler_params=_cparams(("arbitrary",)),
        name="norm_logf_prompt",
    )(x2d, nw_row, wt, b_row, l_mat, c2)


def _logf_sample_body(h_ref, wft_ref, brow_ref, bu_ref, x_ref, ms_ref, lf_ref, c_ref, r_ref, *, kb):
    lf = _log_sigmoid(_dot(h_ref[...], _forget_weight(wft_ref)) + brow_ref[...])
    lf_ref[...] = lf[:, :N_HEADS]
    lft = lf.T[:2 * SUBLANES, :]
    c_ref[...] = _dot_split_lhs(lft, bu_ref[...], 3)[:N_HEADS, :]
    n_blocks = x_ref.shape[1] // kb
    carry = jnp.zeros((x_ref.shape[0], 1), F32)
    for blk in range(n_blocks - 1, -1, -1):
        x = x_ref[:, blk * kb:(blk + 1) * kb]
        cum = _dot_split_lhs(x, ms_ref[...], 3)
        r_ref[:, blk * kb:(blk + 1) * kb] = cum + carry
        carry = carry + cum[:, 0:1] + x[:, 0:1]


def _logf_sample(h_s, wt, f_row0, b_row, past_logf_t, dec_seq):
    n_rows, d = h_s.shape
    n_bh, p_len = past_logf_t.shape
    kb = min(p_len, 512)
    r = lax.broadcasted_iota(jnp.int32, (n_rows, n_rows), 0)
    c = lax.broadcasted_iota(jnp.int32, (n_rows, n_rows), 1)
    bu = ((r // dec_seq == c // dec_seq) & (r <= c)).astype(BF16)
    ms = _tri(kb, "row_gt_col")
    whole = lambda shape: pl.BlockSpec(shape, lambda i: (0,) * len(shape))
    return pl.pallas_call(
        functools.partial(_logf_sample_body, kb=kb),
        grid=(1,),
        in_specs=[whole((n_rows, d)),
                  pl.BlockSpec((None, N_HEADS, d), lambda i: (0, f_row0 // N_HEADS, 0)),
                  whole((1, LANES)), whole((n_rows, n_rows)), whole((n_bh, p_len)), whole((kb, kb))],
        out_specs=(whole((n_rows, N_HEADS)), whole((N_HEADS, n_rows)), whole((n_bh, p_len))),
        out_shape=(jax.ShapeDtypeStruct((n_rows, N_HEADS), F32),
                   jax.ShapeDtypeStruct((N_HEADS, n_rows), F32),
                   jax.ShapeDtypeStruct((n_bh, p_len), F32)),
        compiler_params=_cparams(("arbitrary",), vmem_mib=32),
        name="logf_sample",
    )(h_s, wt, b_row, bu, past_logf_t, ms)


SB_TQ = 256


def _sb_block(q, k, v, m_mat, carry, mask):
    z = _dot_nt(q, k)
    lsn = jnp.minimum(-z, 0.0) - jnp.log(1.0 + jnp.exp2(-jnp.abs(z))) * LOG2E
    lsp = lsn + z
    if mask is not None:
        lsn = jnp.where(mask, lsn, 0.0)
    cum = _dot_split_lhs(lsn, m_mat, 2)
    if carry is not None:
        cum = cum + carry
    w = jnp.exp2(lsp + cum)
    if mask is not None:
        w = jnp.where(mask, w, 0.0)
    o = _dot(w.astype(BF16), v)
    new_carry = cum[:, 0:1] + lsn[:, 0:1]
    return o, new_carry


def _sb_fast_body(*refs):
    _sb_fast_step(pl.program_id(0), *refs)


def _sb_fast_step(i, q_ref, kd_ref, vd_ref, kp_ref, vp_ref, gs_ref, m_ref, a_ref, c_ref, after_head=None):
    tq = q_ref.shape[0]
    has_prev = i > 0
    row = lax.broadcasted_iota(jnp.int32, (tq, tq), 0)
    col = lax.broadcasted_iota(jnp.int32, (tq, tq), 1)
    dmask = col < row
    pmask = jnp.logical_and(has_prev, col >= 0)
    lane = lax.broadcasted_iota(jnp.int32, (tq, LANES), 1)
    m_mat = m_ref[...]
    cacc = jnp.full((tq, LANES), NEG_BIG, F32)
    for hh in range(N_HEADS):
        sl = slice(hh * HEAD_DIM, (hh + 1) * HEAD_DIM)
        q = q_ref[:, sl]
        od, cd = _sb_block(q, kd_ref[:, sl], vd_ref[:, sl], m_mat, None, dmask)
        op, cp = _sb_block(q, kp_ref[:, sl], vp_ref[:, sl], m_mat, cd, pmask)
        a_ref[:, sl] = ((od + op) * gs_ref[:, sl].astype(F32)).astype(BF16)
        cacc = jnp.where(lane == hh, cp, cacc)
        if after_head and hh in after_head:
            after_head[hh]()
    c_ref[...] = cacc


def _sb_rest_body(q_ref, k_ref, v_ref, gs_ref, cin_ref, afast_ref, m_ref, a_ref, o_s, c_s):
    i = pl.program_id(0)
    s = pl.program_id(1)
    j = i - 2 - s
    tq = q_ref.shape[0]

    @pl.when(s == 0)
    def _():
        o_s[...] = jnp.zeros_like(o_s)
        c_s[...] = cin_ref[...]

    active = jnp.logical_and(j >= 0, jnp.max(c_s[...]) > SB_SKIP_LOG2)

    @pl.when(active)
    def _():
        lane = lax.broadcasted_iota(jnp.int32, (tq, LANES), 1)
        m_mat = m_ref[...]
        c_all = c_s[...]
        cacc = c_all
        for hh in range(N_HEADS):
            sl = slice(hh * HEAD_DIM, (hh + 1) * HEAD_DIM)
            carry = jnp.sum(jnp.where(lane == hh, c_all, 0.0), axis=1, keepdims=True)
            o, cn = _sb_block(q_ref[:, sl], k_ref[:, sl], v_ref[:, sl], m_mat, carry, None)
            o_s[:, sl] += o
            cacc = jnp.where(lane == hh, cn, cacc)
        c_s[...] = cacc

    @pl.when(s == pl.num_programs(1) - 1)
    def _():
        a_ref[...] = (afast_ref[...].astype(F32) + o_s[...] * gs_ref[...].astype(F32)).astype(BF16)


def _sb_prompt(q, k, v, gs, fox_dec=None):
    t, w = q.shape
    tq = min(SB_TQ, t)
    nq = t // tq
    m_mat = _tri(tq, "row_gt_col")
    blk = lambda f: pl.BlockSpec((tq, w), f)
    sb_in = [blk(lambda i: (i, 0)), blk(lambda i: (i, 0)), blk(lambda i: (i, 0)),
             blk(lambda i: (jnp.maximum(i - 1, 0), 0)), blk(lambda i: (jnp.maximum(i - 1, 0), 0)),
             blk(lambda i: (i, 0)),
             pl.BlockSpec((tq, tq), lambda i: (0, 0))]
    sb_args = (q, k, v, k, v, gs, m_mat)
    sb_out_specs = (blk(lambda i: (i, 0)), pl.BlockSpec((tq, LANES), lambda i: (i, 0)))
    sb_out_shape = (jax.ShapeDtypeStruct((t, w), BF16), jax.ShapeDtypeStruct((t, LANES), F32))
    a_dec = None
    if fox_dec is not None:
        n_b, kb, nkb = _fox_dec_geometry(fox_dec[0], fox_dec[3])
        if n_b * nkb != nq:
            a_dec = _fox_decode(*fox_dec)
            fox_dec = None
    if fox_dec is None:
        a_fast, carry = pl.pallas_call(
            _sb_fast_body, grid=(nq,), in_specs=sb_in, out_specs=sb_out_specs, out_shape=sb_out_shape,
            compiler_params=_cparams(("parallel",), vmem_mib=48), name="sb_prompt_fast",
        )(*sb_args)
    else:
        dec_in, dec_out, dec_scratch = _fox_dec_specs(fox_dec[0].shape[1], kb,
                                                      lambda i: i // nkb, lambda i: i % nkb)
        n_sb, n_dec = len(sb_in), len(dec_in)

        def fused_body(*refs):
            i = pl.program_id(0)
            outs = refs[n_sb + n_dec:n_sb + n_dec + 3]
            begin, chunks, end = _fox_dec_parts(i % nkb, nkb, *refs[n_sb:n_sb + n_dec], outs[2],
                                                *refs[n_sb + n_dec + 3:])
            slots = {}
            for ci, fn in enumerate(chunks):
                slots.setdefault(((ci + 1) * N_HEADS - 1) // len(chunks), []).append(fn)
            after = {hh: (lambda fns=fns: [fn() for fn in fns]) for hh, fns in slots.items()}
            begin()
            _sb_fast_step(i, *refs[:n_sb], outs[0], outs[1], after_head=after)
            end()

        a_fast, carry, a_dec = pl.pallas_call(
            fused_body, grid=(nq,), in_specs=sb_in + dec_in,
            out_specs=sb_out_specs + (dec_out,),
            out_shape=sb_out_shape + (jax.ShapeDtypeStruct(fox_dec[0].shape, BF16),),
            scratch_shapes=dec_scratch,
            compiler_params=_cparams(("arbitrary",), vmem_mib=56), name="sb_prompt_fox_decode",
        )(*sb_args, *fox_dec)
    if nq <= 2:
        return a_fast, a_dec

    def rest(a_fast, carry):
        kidx = lambda i, s: (jnp.maximum(i - 2 - s, 0), 0)
        return pl.pallas_call(
            _sb_rest_body,
            grid=(nq, nq - 2),
            in_specs=[blk(lambda i, s: (i, 0)), blk(kidx), blk(kidx), blk(lambda i, s: (i, 0)),
                      pl.BlockSpec((tq, LANES), lambda i, s: (i, 0)),
                      blk(lambda i, s: (i, 0)),
                      pl.BlockSpec((tq, tq), lambda i, s: (0, 0))],
            out_specs=blk(lambda i, s: (i, 0)),
            out_shape=jax.ShapeDtypeStruct((t, w), BF16),
            scratch_shapes=[pltpu.VMEM((tq, w), F32), pltpu.VMEM((tq, LANES), F32)],
            compiler_params=_cparams(("parallel", "arbitrary"), vmem_mib=48),
            name="sb_prompt_rest",
        )(q, k, v, gs, carry, a_fast, m_mat)

    need_rest = jnp.max(carry[2 * tq:, :]) > SB_SKIP_LOG2
    return lax.cond(need_rest, rest, lambda a, c: a, a_fast, carry), a_dec


FOX_T = 512
FOX_WIN_TILES = (1, 2, 3, 4, 5, 6)
FOX_ZB_MAX = 40.0
FOX_SKIP_LOG = -104.0


def _head_column(blk, hh):
    lane8 = lax.broadcasted_iota(jnp.int32, blk.shape, 1)
    return jnp.sum(jnp.where(lane8 == hh, blk, 0.0), axis=1, keepdims=True)


BIAS_ONE_LANE = 3 * N_HEADS


def _split_bias_rows(f, lane):
    hi = f.astype(BF16).astype(F32)
    r1 = f - hi
    mid = r1.astype(BF16).astype(F32)
    low = (r1 - mid).astype(BF16).astype(F32)
    out = jnp.where(lane < N_HEADS, hi,
                    jnp.where(lane < 2 * N_HEADS, pltpu.roll(mid, N_HEADS, axis=1),
                              jnp.where(lane < BIAS_ONE_LANE, pltpu.roll(low, 2 * N_HEADS, axis=1),
                                        jnp.where(lane == BIAS_ONE_LANE, 1.0, 0.0))))
    return out.astype(BF16)


def _bias_selectors():
    src = lax.broadcasted_iota(jnp.int32, (N_HEADS, LANES, LANES), 1)
    dst = lax.broadcasted_iota(jnp.int32, (N_HEADS, LANES, LANES), 2)
    head = lax.broadcasted_iota(jnp.int32, (N_HEADS, LANES, LANES), 0)

    def sel(bias_base, one_base):
        term = dst - bias_base
        takes_bias = (term >= 0) & (term < 3) & (src == term * N_HEADS + head)
        takes_one = (dst >= one_base) & (dst < one_base + 3) & (src == BIAS_ONE_LANE)
        return (takes_bias | takes_one).astype(BF16)

    return jnp.stack([sel(0, 3), sel(3, 0)])


def _fox_fast_body(live_ref, q_ref, k_ref, v_ref, qb_ref, kb_ref, sel_ref, gs_ref, a_ref, kaug_s):
    hh = pl.program_id(0)
    qb = pl.program_id(1)
    nq = pl.num_programs(1)
    tb = q_ref.shape[0]
    t = k_ref.shape[0]

    @pl.when(qb == 0)
    def _():
        def build(c, carry):
            start = pl.multiple_of(c * tb, tb)
            kaug_s[pl.ds(start, tb), :] = _dot(kb_ref[pl.ds(start, tb), :], sel_ref[1]).astype(BF16)
            return carry
        lax.fori_loop(0, nq, build, 0)

    q_start = pl.multiple_of(qb * tb, tb)
    q_end = q_start + tb
    q2 = jnp.concatenate([q_ref[...], _dot(qb_ref[...], sel_ref[0]).astype(BF16)], axis=1)
    row = lax.broadcasted_iota(jnp.int32, (tb, 1), 0)
    live = live_ref[hh * nq + qb]

    def run(tiles, single):
        span = min(tiles * tb, t)
        lane_w = lax.broadcasted_iota(jnp.int32, (span, LANES), 1)
        ones_blk = jnp.where(lane_w == 0, 1.0, 0.0).astype(BF16)
        col = lax.broadcasted_iota(jnp.int32, (1, span), 1)

        def window(j, acc):
            upper = q_end - j * span
            start = pl.multiple_of(jnp.maximum(upper - span, 0), tb)
            k2 = jnp.concatenate([k_ref[pl.ds(start, span), :], kaug_s[pl.ds(start, span), :]], axis=1)
            p = jnp.exp2(_dot_nt(q2, k2))
            valid = jnp.logical_and(col - (q_start - start) <= row, col < upper - start)
            p = jnp.where(valid, p, 0.0).astype(BF16)
            v2 = jnp.concatenate([v_ref[pl.ds(start, span), :], ones_blk], axis=1)
            return acc + _dot(p, v2)

        acc = jnp.zeros((tb, 2 * HEAD_DIM), F32)
        if single:
            acc = window(0, acc)
        else:
            acc = lax.fori_loop(0, (live + (tiles - 1)) // tiles, window, acc)
        o = acc[:, :HEAD_DIM] / acc[:, HEAD_DIM:HEAD_DIM + 1]
        a_ref[...] = (o * gs_ref[...].astype(F32)).astype(BF16)

    bounds = (0,) + FOX_WIN_TILES
    for lo_t, hi_t in zip(bounds[:-1], bounds[1:]):
        pl.when(jnp.logical_and(live > lo_t, live <= hi_t))(functools.partial(run, hi_t, True))
    pl.when(live > FOX_WIN_TILES[-1])(functools.partial(run, FOX_WIN_TILES[-1], False))


def _fox_slow_body(q_ref, k_ref, v_ref, fcol_ref, frow_ref, gs_ref, a_ref):
    hh = pl.program_id(0)
    qb = pl.program_id(1)
    tb = q_ref.shape[0]
    q = q_ref[...]
    fq = _head_column(fcol_ref[...], hh) * LOG2E

    def scores(kb):
        start = pl.multiple_of(kb * tb, tb)
        k = k_ref[pl.ds(start, tb), :]
        fk = frow_ref[0, :, pl.ds(start, tb)] * LOG2E
        return _dot_nt(q, k) + (fq - fk), v_ref[pl.ds(start, tb), :]

    def update(s, v, carry):
        m, l, acc = carry
        m_new = jnp.maximum(m, jnp.max(s, axis=1, keepdims=True))
        alpha = jnp.exp2(m - m_new)
        p = jnp.exp2(s - m_new)
        l = alpha * l + jnp.sum(p, axis=1, keepdims=True)
        acc = alpha * acc + _dot(p.astype(BF16), v)
        return m_new, l, acc

    def body(kb, carry):
        s, v = scores(kb)
        return update(s, v, carry)

    init = (jnp.full((tb, 1), NEG_BIG, F32), jnp.zeros((tb, 1), F32), jnp.zeros((tb, HEAD_DIM), F32))
    carry = lax.fori_loop(0, qb, body, init)
    s, v = scores(qb)
    row = lax.broadcasted_iota(jnp.int32, (tb, tb), 0)
    col = lax.broadcasted_iota(jnp.int32, (tb, tb), 1)
    s = jnp.where(col <= row, s, NEG_BIG)
    _, l, acc = update(s, v, carry)
    a_ref[...] = ((acc / l) * gs_ref[...].astype(F32)).astype(BF16)


def _fox_prompt(q, k, v, f_col, f_row, gs, zb, q_bias, k_bias):
    t, w = q.shape
    tb = min(FOX_T, t)
    nq = t // tb
    out_shape = jax.ShapeDtypeStruct((t, w), BF16)

    def fast(q, k, v, f_col, f_row, gs, q_bias, k_bias):
        f_start = f_row[:, ::tb]
        f_end = f_row[:, tb - 1::tb]
        dead = (f_start[:, :, None] - f_end[:, None, :]) < FOX_SKIP_LOG
        lo = jnp.sum(dead, axis=2).astype(jnp.int32)
        live = (jnp.arange(1, nq + 1, dtype=jnp.int32)[None, :] - lo).reshape(-1)
        head_blk = lambda h, i, live: (i, h)
        head_all = lambda h, i, live: (0, h)
        grid_spec = pltpu.PrefetchScalarGridSpec(
            num_scalar_prefetch=1,
            grid=(N_HEADS, nq),
            in_specs=[pl.BlockSpec((tb, HEAD_DIM), head_blk),
                      pl.BlockSpec((t, HEAD_DIM), head_all),
                      pl.BlockSpec((t, HEAD_DIM), head_all),
                      pl.BlockSpec((tb, LANES), lambda h, i, live: (i, 0)),
                      pl.BlockSpec((t, LANES), lambda h, i, live: (0, 0), pipeline_mode=pl.Buffered(1)),
                      pl.BlockSpec((2, None, LANES, LANES), lambda h, i, live: (0, h, 0, 0)),
                      pl.BlockSpec((tb, HEAD_DIM), head_blk)],
            out_specs=pl.BlockSpec((tb, HEAD_DIM), head_blk),
            scratch_shapes=[pltpu.VMEM((t, LANES), BF16)])
        return pl.pallas_call(
            _fox_fast_body, grid_spec=grid_spec, out_shape=out_shape,
            compiler_params=_cparams(("parallel", "arbitrary"), vmem_mib=48),
            name="fox_prompt_fast",
        )(live, q, k, v, q_bias, k_bias, _bias_selectors(), gs)

    def slow(q, k, v, f_col, f_row, gs, q_bias, k_bias):
        f_row3 = f_row.reshape(N_HEADS, 1, t)
        return pl.pallas_call(
            _fox_slow_body,
            grid=(N_HEADS, nq),
            in_specs=[pl.BlockSpec((tb, HEAD_DIM), lambda h, i: (i, h)),
                      pl.BlockSpec((t, HEAD_DIM), lambda h, i: (0, h)),
                      pl.BlockSpec((t, HEAD_DIM), lambda h, i: (0, h)),
                      pl.BlockSpec((tb, N_HEADS), lambda h, i: (i, 0)),
                      pl.BlockSpec((1, 1, t), lambda h, i: (h, 0, 0)),
                      pl.BlockSpec((tb, HEAD_DIM), lambda h, i: (i, h))],
            out_specs=pl.BlockSpec((tb, HEAD_DIM), lambda h, i: (i, h)),
            out_shape=out_shape,
            compiler_params=_cparams(("parallel", "parallel"), vmem_mib=48),
            name="fox_prompt_slow",
        )(q, k, v, f_col, f_row3, gs)

    return lax.cond(zb <= FOX_ZB_MAX, fast, slow, q, k, v, f_col, f_row, gs, q_bias, k_bias)


DEC_KB = 2048
SB_DEC_WIN = 256


def _own_head_mask():
    row = lax.broadcasted_iota(jnp.int32, (SUBLANES, W_ATT), 0)
    col = lax.broadcasted_iota(jnp.int32, (SUBLANES, W_ATT), 1)
    return (col // HEAD_DIM) == row


def _expand_q(q_ref, qx_s):
    n_tok = q_ref.shape[0]
    qf = q_ref[...].astype(F32)
    own = _own_head_mask()
    for tkn in range(n_tok):
        rep = jnp.broadcast_to(qf[tkn:tkn + 1, :], (SUBLANES, W_ATT))
        qx_s[tkn * N_HEADS:(tkn + 1) * N_HEADS, :] = jnp.where(own, rep, 0.0)


def _pad_new(kn_ref, vn_ref, knp_s, vnp_s):
    n_tok = kn_ref.shape[0]
    knp_s[...] = jnp.zeros_like(knp_s)
    vnp_s[...] = jnp.zeros_like(vnp_s)
    knp_s[0:n_tok, :] = kn_ref[...]
    vnp_s[0:n_tok, :] = vn_ref[...]


def _gather_heads(c_ref, dst_s):
    n_keys = dst_s.shape[0]
    for hh in range(N_HEADS):
        dst_s[:, hh * HEAD_DIM:(hh + 1) * HEAD_DIM] = (
            c_ref[0, pl.ds(hh, n_keys, stride=N_HEADS), :].astype(BF16))


def _collapse_heads(o_full, out_s):
    n_tok = o_full.shape[0] // N_HEADS
    own = _own_head_mask()
    for tkn in range(n_tok):
        blk = jnp.where(own, o_full[tkn * N_HEADS:(tkn + 1) * N_HEADS, :], 0.0)
        out_s[tkn:tkn + 1, :] = jnp.sum(blk, axis=0, keepdims=True)


def _sb_dec_fast_body(q_ref, kn_ref, vn_ref, kc_ref, vc_ref, gs_ref, mn_ref, mc_ref, a_ref, c_ref,
                      qx_s, knp_s, vnp_s, kx_s, vx_s, out_s):
    n_rows = qx_s.shape[0]
    _expand_q(q_ref, qx_s)
    _pad_new(kn_ref, vn_ref, knp_s, vnp_s)
    qx = qx_s[...].astype(BF16)
    row = lax.broadcasted_iota(jnp.int32, (n_rows, LANES), 0)
    col = lax.broadcasted_iota(jnp.int32, (n_rows, LANES), 1)
    mask = col < row // N_HEADS
    o_new, carry = _sb_block(qx, knp_s[...], vnp_s[...], mn_ref[...], None, mask)
    _gather_heads(kc_ref, kx_s)
    _gather_heads(vc_ref, vx_s)
    o_win, carry = _sb_block(qx, kx_s[...], vx_s[...], mc_ref[...], carry, None)
    _collapse_heads(o_new + o_win, out_s)
    a_ref[...] = (out_s[...] * gs_ref[...].astype(F32)).astype(BF16)
    c_ref[...] = jnp.broadcast_to(carry, c_ref.shape)


def _sb_dec_rest_body(q_ref, kc_ref, vc_ref, gs_ref, cin_ref, afast_ref, mc_ref, a_ref,
                      qx_s, kx_s, vx_s, acc_s, c_s, out_s):
    s = pl.program_id(1)

    @pl.when(s == 0)
    def _():
        _expand_q(q_ref, qx_s)
        acc_s[...] = jnp.zeros_like(acc_s)
        c_s[...] = cin_ref[:, 0:1]

    @pl.when(jnp.max(c_s[...]) > SB_SKIP_LOG2)
    def _():
        _gather_heads(kc_ref, kx_s)
        _gather_heads(vc_ref, vx_s)
        o, carry = _sb_block(qx_s[...].astype(BF16), kx_s[...], vx_s[...], mc_ref[...], c_s[...], None)
        acc_s[...] += o
        c_s[...] = carry

    @pl.when(s == pl.num_programs(1) - 1)
    def _():
        _collapse_heads(acc_s[...], out_s)
        a_ref[...] = (afast_ref[...].astype(F32) + out_s[...] * gs_ref[...].astype(F32)).astype(BF16)


def _sb_decode(q, k_new, v_new, k_cache, v_cache, gs, dec_seq, host=None):
    n_rows_all, w = q.shape
    n_b = n_rows_all // dec_seq
    p_len = k_cache.shape[1] // N_HEADS
    win = min(SB_DEC_WIN, p_len)
    n_win = p_len // win
    n_rows = dec_seq * N_HEADS
    m_new = _tri(LANES, "row_gt_col")
    m_win = _tri(win, "row_gt_col")
    tok = lambda b: (b, 0)
    newest = pl.BlockSpec((1, win * N_HEADS, HEAD_DIM), lambda b: (b, n_win - 1, 0))
    rider = dict(
        body=_sb_dec_fast_body, n_steps=n_b,
        in_specs=[pl.BlockSpec((dec_seq, w), tok), pl.BlockSpec((dec_seq, w), tok),
                  pl.BlockSpec((dec_seq, w), tok),
                  newest, newest,
                  pl.BlockSpec((dec_seq, w), tok),
                  pl.BlockSpec((LANES, LANES), lambda b: (0, 0)),
                  pl.BlockSpec((win, win), lambda b: (0, 0))],
        args=(q, k_new, v_new, k_cache, v_cache, gs, m_new, m_win),
        out_specs=(pl.BlockSpec((dec_seq, w), tok), pl.BlockSpec((n_rows, LANES), tok)),
        out_shape=(jax.ShapeDtypeStruct((n_rows_all, w), BF16),
                   jax.ShapeDtypeStruct((n_b * n_rows, LANES), F32)),
        scratch=[pltpu.VMEM((n_rows, w), F32), pltpu.VMEM((LANES, w), BF16),
                 pltpu.VMEM((LANES, w), BF16), pltpu.VMEM((win, w), BF16),
                 pltpu.VMEM((win, w), BF16), pltpu.VMEM((dec_seq, w), F32)])
    hosted = None
    if host is not None:
        hosted, a_fast, carry = host(rider)
    else:
        a_fast, carry = pl.pallas_call(
            rider["body"], grid=(n_b,), in_specs=rider["in_specs"], out_specs=rider["out_specs"],
            out_shape=rider["out_shape"], scratch_shapes=rider["scratch"],
            compiler_params=_cparams(("parallel",), vmem_mib=48), name="sb_decode_fast",
        )(*rider["args"])
    if n_win <= 1:
        return a_fast, hosted

    def rest(a_fast, carry):
        tok2 = lambda b, s: (b, 0)
        older = pl.BlockSpec((1, win * N_HEADS, HEAD_DIM), lambda b, s: (b, n_win - 2 - s, 0))
        return pl.pallas_call(
            _sb_dec_rest_body,
            grid=(n_b, n_win - 1),
            in_specs=[pl.BlockSpec((dec_seq, w), tok2),
                      older, older,
                      pl.BlockSpec((dec_seq, w), tok2),
                      pl.BlockSpec((n_rows, LANES), tok2),
                      pl.BlockSpec((dec_seq, w), tok2),
                      pl.BlockSpec((win, win), lambda b, s: (0, 0))],
            out_specs=pl.BlockSpec((dec_seq, w), tok2),
            out_shape=jax.ShapeDtypeStruct((n_rows_all, w), BF16),
            scratch_shapes=[pltpu.VMEM((n_rows, w), F32), pltpu.VMEM((win, w), BF16),
                            pltpu.VMEM((win, w), BF16), pltpu.VMEM((n_rows, w), F32),
                            pltpu.VMEM((n_rows, 1), F32), pltpu.VMEM((dec_seq, w), F32)],
            compiler_params=_cparams(("parallel", "arbitrary"), vmem_mib=48),
            name="sb_decode_rest",
        )(q, k_cache, v_cache, gs, carry, a_fast, m_win)

    need_rest = jnp.max(carry) > SB_SKIP_LOG2
    return lax.cond(need_rest, rest, lambda a, c: a, a_fast, carry), hosted


def _fox_dec_body(*refs):
    _fox_dec_step(pl.program_id(1), pl.num_programs(1), *refs)


FOX_DEC_CHUNK = 1024


def _fox_dec_step(s, n_s, *refs):
    begin, chunks, end = _fox_dec_parts(s, n_s, *refs)
    begin()
    for chunk_fn in chunks:
        chunk_fn()
    end()


def _fox_dec_parts(s, n_s, q_ref, kn_ref, vn_ref, kc_ref, vc_ref, gs_ref, c_ref, r_ref, a_ref,
                   acc_s, m_s, l_s):
    n_rows = q_ref.shape[1]
    q = q_ref[0]
    own = lax.broadcasted_iota(jnp.int32, (SUBLANES, LANES), 0) == \
        lax.broadcasted_iota(jnp.int32, (SUBLANES, LANES), 1) % N_HEADS
    head_bias = jnp.where(own, 0.0, NEG_BIG)

    def update(sc, v):
        m = m_s[...]
        m_new = jnp.maximum(m, jnp.max(sc, axis=1, keepdims=True))
        alpha = jnp.exp2(m - m_new)
        p = jnp.exp2(sc - m_new)
        l_s[...] = alpha * l_s[...] + jnp.sum(p, axis=1, keepdims=True)
        acc_s[...] = alpha * acc_s[...] + _dot(p.astype(BF16), v)
        m_s[...] = m_new

    def add_key_bias(z, decay_row):
        n = decay_row.shape[1]
        tile = jnp.concatenate([head_bias] * (n // LANES), axis=1) + decay_row * LOG2E
        return (z.reshape(n_rows // SUBLANES, SUBLANES, n) + tile[None]).reshape(n_rows, n)

    def begin():
        @pl.when(s == 0)
        def _():
            m_s[...] = jnp.full_like(m_s, NEG_BIG)
            l_s[...] = jnp.zeros_like(l_s)
            acc_s[...] = jnp.zeros_like(acc_s)
            row = lax.broadcasted_iota(jnp.int32, (n_rows, LANES), 0)
            col = lax.broadcasted_iota(jnp.int32, (n_rows, LANES), 1)
            sc = add_key_bias(_dot_nt(q, kn_ref[0]), -c_ref[0])
            sc = jnp.where(col // N_HEADS <= row // N_HEADS, sc, NEG_BIG)
            update(sc, vn_ref[0])

    n_keys_rows = kc_ref.shape[1]
    chunk = min(FOX_DEC_CHUNK * N_HEADS, n_keys_rows)

    def one_chunk(c0):
        k2 = kc_ref[0, c0:c0 + chunk, :].astype(BF16)
        v2 = vc_ref[0, c0:c0 + chunk, :].astype(BF16)
        update(add_key_bias(_dot_nt(q, k2), r_ref[0, :, c0:c0 + chunk]), v2)

    def end():
        @pl.when(s == n_s - 1)
        def _():
            a_ref[0] = ((acc_s[...] / l_s[...]) * gs_ref[0].astype(F32)).astype(BF16)

    return begin, [functools.partial(one_chunk, c0) for c0 in range(0, n_keys_rows, chunk)], end


def _fox_dec_geometry(q3, k_cache):
    n_b = q3.shape[0]
    p_len = k_cache.shape[1] // N_HEADS
    kb = min(DEC_KB, p_len)
    return n_b, kb, p_len // kb


def _fox_dec_specs(n_rows, kb, stream, block):
    tok = pl.BlockSpec((1, n_rows, HEAD_DIM), lambda *g: (stream(*g), 0, 0))
    cache = pl.BlockSpec((1, kb * N_HEADS, HEAD_DIM), lambda *g: (stream(*g), block(*g), 0))
    in_specs = [tok, tok, tok, cache, cache, tok,
                pl.BlockSpec((1, 1, LANES), lambda *g: (stream(*g), 0, 0)),
                pl.BlockSpec((1, 1, kb * N_HEADS), lambda *g: (stream(*g), 0, block(*g)))]
    scratch = [pltpu.VMEM((n_rows, HEAD_DIM), F32), pltpu.VMEM((n_rows, 1), F32),
               pltpu.VMEM((n_rows, 1), F32)]
    return in_specs, tok, scratch


def _fox_decode(q3, kn3, vn3, k_cache, v_cache, gs3, c_row, r_row):
    n_b, kb, nkb = _fox_dec_geometry(q3, k_cache)
    in_specs, out_spec, scratch = _fox_dec_specs(q3.shape[1], kb, lambda b, s: b, lambda b, s: s)
    return pl.pallas_call(
        _fox_dec_body,
        grid=(n_b, nkb),
        in_specs=in_specs,
        out_specs=out_spec,
        out_shape=jax.ShapeDtypeStruct(q3.shape, BF16),
        scratch_shapes=scratch,
        compiler_params=_cparams(("parallel", "arbitrary"), vmem_mib=56),
        name="fox_decode",
    )(q3, kn3, vn3, k_cache, v_cache, gs3, c_row, r_row)


POST_TM = 512


def _post_body(x_ref, asb_ref, afx_ref, msb_ref, mfx_ref, wsb_ref, wfx_ref, wo_ref, y_ref):
    u_sb = _dot(asb_ref[...], wsb_ref[...])
    u_fx = _dot(afx_ref[...], wfx_ref[...])
    merged = msb_ref[...].astype(F32) * u_sb + mfx_ref[...].astype(F32) * u_fx
    y_ref[...] = x_ref[...] + _dot(merged.astype(BF16), wo_ref[...])


def _post(x2d, a_sb, a_fx, m_sig, w_sb, w_fx, w_o, rider=None):
    m, d = x2d.shape
    w = a_sb.shape[1]
    tm = min(m, POST_TM)
    row = lambda i: (i, 0)
    const = lambda i: (0, 0)
    resident = functools.partial(pl.BlockSpec, index_map=const, pipeline_mode=pl.Buffered(1))
    in_specs = [pl.BlockSpec((tm, d), row), pl.BlockSpec((tm, w), row), pl.BlockSpec((tm, w), row),
                pl.BlockSpec((tm, d), lambda i: (i, 0)), pl.BlockSpec((tm, d), lambda i: (i, 1)),
                resident((w, d)), resident((w, d)), resident((d, d))]
    args = (x2d, a_sb, a_fx, m_sig, m_sig, w_sb, w_fx, w_o)
    y_spec = pl.BlockSpec((tm, d), row)
    y_shape = jax.ShapeDtypeStruct((m, d), F32)
    if rider is None:
        return pl.pallas_call(
            _post_body, grid=(m // tm,), in_specs=in_specs, out_specs=y_spec, out_shape=y_shape,
            compiler_params=_cparams(("parallel",), vmem_mib=56), name="post",
        )(*args)
    assert rider["n_steps"] == m // tm
    n_in, n_rin, n_rout = len(in_specs), len(rider["in_specs"]), len(rider["out_specs"])

    def body(*refs):
        rider_out = refs[n_in + n_rin + 1:n_in + n_rin + 1 + n_rout]
        _post_body(*refs[:n_in], refs[n_in + n_rin])
        rider["body"](*refs[n_in:n_in + n_rin], *rider_out, *refs[n_in + n_rin + 1 + n_rout:])

    return pl.pallas_call(
        body, grid=(m // tm,), in_specs=in_specs + list(rider["in_specs"]),
        out_specs=(y_spec,) + tuple(rider["out_specs"]),
        out_shape=(y_shape,) + tuple(rider["out_shape"]),
        scratch_shapes=rider["scratch"],
        compiler_params=_cparams(("arbitrary",), vmem_mib=58), name="post_with_rider",
    )(*args, *rider["args"])


def _project_all(hp, hs, w, qnw, knw):
    seg = lambda c: c * W_ATT
    q_sb = _proj(hp, hs, w, seg(0), "scale")
    k_sb = _proj(hp, hs, w, seg(1), "kv")
    v_sb = _proj(hp, hs, w, seg(2), "kv")
    g_sb = _proj(hp, hs, w, seg(3), "silu")
    q_fx = _proj(hp, hs, w, seg(4), "qnorm", nw=qnw)
    k_fx = _proj(hp, hs, w, seg(5), "knorm", nw=knw)
    v_fx = _proj(hp, hs, w, seg(6), "kv")
    g_fx = _proj(hp, hs, w, seg(7), "silu")
    m_sig = _proj(hp, hs, w, seg(8) + N_HEADS, "sigmoid", n_tiles=2 * D_MODEL // W_ATT)
    calls = (q_sb, k_sb, v_sb, g_sb, q_fx, k_fx, v_fx, g_fx, m_sig)
    prompt = tuple(o for r in calls for o in r[:len(r) // 2])
    decode = tuple(o for r in calls for o in r[len(r) // 2:])
    return prompt, decode


def kernel(x_prompt, x_sample, cache_sb_k, cache_sb_v, cache_fox_k, cache_fox_v, cache_fox_logf,
           norm_w, w_in, b_forget, q_norm_w, k_norm_w, w_branch_sb, w_branch_fox, w_out):
    depth = norm_w.shape[0]
    assert depth == 1, "single-layer step"
    bsz, seq, d = x_prompt.shape
    assert bsz == 1
    n_dec, dec_seq, _ = x_sample.shape
    p_len = cache_sb_k.shape[2]
    n_main = 8 * W_ATT

    w_t = jnp.swapaxes(w_in, 1, 2)
    b_row = jnp.pad(b_forget[0].astype(F32)[None, :], ((0, 0), (0, LANES - N_HEADS)))
    qnw = jnp.tile(q_norm_w[0].astype(F32), N_HEADS)[None, :]
    knw = jnp.tile(k_norm_w[0].astype(F32), N_HEADS)[None, :]
    nw_row = norm_w[0].astype(F32)[None, :]
    w_sb = w_branch_sb[0].astype(BF16)
    w_fx = w_branch_fox[0].astype(BF16)
    w_o = w_out[0].astype(BF16)

    xp = x_prompt.reshape(seq, d)
    xs = x_sample.reshape(n_dec * dec_seq, d)
    zb = (1.02 * HEAD_DIM * QK_SCALE) * jnp.max(jnp.abs(q_norm_w[0])) * jnp.max(jnp.abs(k_norm_w[0]))
    zb = zb.astype(F32)
    hp, logf, f_col, f_row, q_bias, k_bias = _norm_logf_prompt(
        xp, nw_row, w_t, n_main, b_row, (zb * LOG2E).reshape(1, 1))
    hs = _rmsnorm(xs, nw_row)
    prompt_proj, decode_proj = _project_all(hp, hs, w_t, qnw, knw)

    (q_sb, k_sb, k_sb_b, v_sb, v_sb_b, g_sb, q_fx, k_fx, k_fx_b, v_fx, v_fx_b, g_fx,
     m_sig) = prompt_proj
    (sq_sb, sk_sb, sk_sb_b, sv_sb, sv_sb_b, sg_sb, sq_fx, sk_fx, sk_fx_b, sv_fx, sv_fx_b, sg_fx,
     sm_sig) = decode_proj

    past_logf_t = jnp.transpose(cache_fox_logf[0].astype(F32), (0, 2, 1)).reshape(n_dec * N_HEADS, p_len)
    s_logf, c_all, r_past = _logf_sample(hs, w_t, n_main, b_row, past_logf_t, dec_seq)
    n_th = dec_seq * N_HEADS
    c_row = jnp.transpose(c_all.reshape(N_HEADS, n_dec, dec_seq), (1, 2, 0)).reshape(n_dec, 1, n_th)
    r_row = jnp.transpose(r_past.reshape(n_dec, N_HEADS, p_len), (0, 2, 1)).reshape(n_dec, 1, p_len * N_HEADS)
    by_head = lambda a: a.reshape(n_dec, n_th, HEAD_DIM)
    kc_sb = cache_sb_k[0].reshape(n_dec, p_len * N_HEADS, HEAD_DIM)
    vc_sb = cache_sb_v[0].reshape(n_dec, p_len * N_HEADS, HEAD_DIM)
    kc_fx = cache_fox_k[0].reshape(n_dec, p_len * N_HEADS, HEAD_DIM)
    vc_fx = cache_fox_v[0].reshape(n_dec, p_len * N_HEADS, HEAD_DIM)

    a_sb, sa_fx = _sb_prompt(q_sb, k_sb_b, v_sb_b, g_sb,
                             fox_dec=(by_head(sq_fx), by_head(sk_fx_b), by_head(sv_fx_b), kc_fx, vc_fx,
                                      by_head(sg_fx), c_row, r_row))
    sa_fx = sa_fx.reshape(n_dec * dec_seq, W_ATT)
    a_fx = _fox_prompt(q_fx, k_fx_b, v_fx_b, f_col, f_row, g_fx, zb, q_bias, k_bias)
    post_steps = seq // min(seq, POST_TM)

    def host(rider):
        y, a_fast, carry = _post(xp, a_sb, a_fx, m_sig, w_sb, w_fx, w_o, rider=rider)
        return y, a_fast, carry

    sa_sb, y_prompt = _sb_decode(sq_sb, sk_sb_b, sv_sb_b, kc_sb, vc_sb, sg_sb, dec_seq,
                                 host=host if post_steps == n_dec else None)
    if y_prompt is None:
        y_prompt = _post(xp, a_sb, a_fx, m_sig, w_sb, w_fx, w_o)
    y_prompt = y_prompt.reshape(bsz, seq, d)
    y_sample = _post(xs, sa_sb, sa_fx, sm_sig, w_sb, w_fx, w_o).reshape(n_dec, dec_seq, d)

    hd = (N_HEADS, HEAD_DIM)
    return (y_prompt, y_sample,
            k_sb.reshape(1, bsz, seq, *hd), v_sb.reshape(1, bsz, seq, *hd),
            k_fx.reshape(1, bsz, seq, *hd), v_fx.reshape(1, bsz, seq, *hd),
            logf.reshape(1, bsz, seq, N_HEADS),
            sk_sb.reshape(1, n_dec, dec_seq, *hd), sv_sb.reshape(1, n_dec, dec_seq, *hd),
            sk_fx.reshape(1, n_dec, dec_seq, *hd), sv_fx.reshape(1, n_dec, dec_seq, *hd),
            s_logf.reshape(1, n_dec, dec_seq, N_HEADS))
```

```python
import functools

import jax
import jax.numpy as jnp
from jax import lax
from jax.experimental import pallas as pl
from jax.experimental.pallas import tpu as pltpu

F32 = jnp.float32
BF16 = jnp.bfloat16

D_MODEL = 2048
N_HEADS = 8
HEAD_DIM = 128
W_ATT = N_HEADS * HEAD_DIM
RMS_EPS = 1e-6
QK_SCALE = HEAD_DIM ** -0.5
LOG2E = 1.4426950408889634
LANES = 128
SUBLANES = 8
SB_SKIP_LOG2 = -150.0
NEG_BIG = -1e30
MIB = 1024 * 1024


def _cparams(sem, vmem_mib=None):
    kw = dict(dimension_semantics=sem)
    if vmem_mib is not None:
        kw["vmem_limit_bytes"] = vmem_mib * MIB
    return pltpu.CompilerParams(**kw)


def _softplus_neg_abs(z):
    return jnp.log1p(jnp.exp(-jnp.abs(z)))


def _log_sigmoid(z):
    return jnp.minimum(z, 0.0) - _softplus_neg_abs(z)


def _split_bf16(x, n):
    parts = []
    r = x
    for _ in range(n - 1):
        p = r.astype(BF16)
        parts.append(p)
        r = r - p.astype(F32)
    parts.append(r.astype(BF16))
    return parts


def _dot(a, b):
    return jnp.dot(a, b, preferred_element_type=F32)


def _dot_nt(a, b):
    return lax.dot_general(a, b, (((1,), (1,)), ((), ())), preferred_element_type=F32)


def _dot_split_lhs(x, m, n):
    acc = None
    for p in _split_bf16(x, n):
        t = _dot(p, m)
        acc = t if acc is None else acc + t
    return acc


def _dot_split_rhs(m, x, n):
    acc = None
    for p in _split_bf16(x, n):
        t = _dot(m, p)
        acc = t if acc is None else acc + t
    return acc


def _tri(n, kind):
    r = lax.broadcasted_iota(jnp.int32, (n, n), 0)
    c = lax.broadcasted_iota(jnp.int32, (n, n), 1)
    if kind == "row_gt_col":
        m = r > c
    elif kind == "row_le_col":
        m = r <= c
    elif kind == "row_ge_col":
        m = r >= c
    else:
        raise ValueError(kind)
    return m.astype(BF16)


def _rmsnorm_body(x_ref, w_ref, o_ref):
    x = x_ref[...]
    ms = jnp.mean(x * x, axis=-1, keepdims=True)
    o_ref[...] = (x * lax.rsqrt(ms + RMS_EPS) * w_ref[...]).astype(o_ref.dtype)


def _rmsnorm(x2d, w_row):
    m, d = x2d.shape
    tm = min(m, 512)
    return pl.pallas_call(
        _rmsnorm_body,
        grid=(m // tm,),
        in_specs=[pl.BlockSpec((tm, d), lambda i: (i, 0)),
                  pl.BlockSpec((1, d), lambda i: (0, 0))],
        out_specs=pl.BlockSpec((tm, d), lambda i: (i, 0)),
        out_shape=jax.ShapeDtypeStruct((m, d), BF16),
        compiler_params=_cparams(("parallel",)),
        name="rmsnorm",
    )(x2d, w_row)


def _head_rmsnorm(acc, nw):
    parts = []
    for hh in range(N_HEADS):
        a = acc[:, hh * HEAD_DIM:(hh + 1) * HEAD_DIM]
        ms = jnp.mean(a * a, axis=-1, keepdims=True)
        parts.append(a * lax.rsqrt(ms + RMS_EPS))
    return jnp.concatenate(parts, axis=1) * nw


def _proj_epilogue(acc, kind, nw_ref, outs):
    if kind == "scale":
        outs[0][...] = (acc * (QK_SCALE * LOG2E)).astype(BF16)
    elif kind == "kv":
        outs[0][...] = acc
        outs[1][...] = acc.astype(BF16)
    elif kind == "silu":
        outs[0][...] = (acc * jax.nn.sigmoid(acc)).astype(BF16)
    elif kind == "sigmoid":
        outs[0][...] = jax.nn.sigmoid(acc).astype(BF16)
    elif kind == "qnorm":
        outs[0][...] = (_head_rmsnorm(acc, nw_ref[...]) * (QK_SCALE * LOG2E)).astype(BF16)
    elif kind == "knorm":
        y = _head_rmsnorm(acc, nw_ref[...])
        outs[0][...] = y
        outs[1][...] = y.astype(BF16)
    else:
        raise ValueError(kind)


def _proj_body(*refs, kind, n_out, shift):
    hp_ref, hs_ref, w_ref = refs[:3]
    pos = 3
    wn_ref = nw_ref = None
    if shift:
        wn_ref = refs[pos]
        pos += 1
    if kind in ("qnorm", "knorm"):
        nw_ref = refs[pos]
        pos += 1
    outs_p = refs[pos:pos + n_out]
    outs_s = refs[pos + n_out:pos + 2 * n_out]
    wb_s = refs[-1]
    i = pl.program_id(1)
    n_prompt = pl.num_programs(1) - 1

    @pl.when(i == 0)
    def _():
        if shift:
            wt = jnp.concatenate([w_ref[shift:, :], wn_ref[...]], axis=0)
        else:
            wt = w_ref[...]
        wb_s[...] = wt.T.astype(BF16)

    @pl.when(i < n_prompt)
    def _():
        _proj_epilogue(_dot(hp_ref[...], wb_s[...]), kind, nw_ref, outs_p)

    @pl.when(i == n_prompt)
    def _():
        _proj_epilogue(_dot(hs_ref[...], wb_s[...]), kind, nw_ref, outs_s)


def _proj(hp, hs, wt, col0, kind, nw=None, n_tiles=1):
    mp, d = hp.shape
    ms = hs.shape[0]
    tn = W_ATT
    dtypes = (F32, BF16) if kind in ("kv", "knorm") else (BF16,)
    tm = min(mp, 1024)
    n_prompt = mp // tm
    blk0 = col0 // tn
    shift = col0 - blk0 * tn
    assert shift in (0, SUBLANES)
    grid = (n_tiles, n_prompt + 1)
    prow = lambda j, i: (jnp.minimum(i, n_prompt - 1), 0)
    w_mode = {} if n_tiles > 1 else dict(pipeline_mode=pl.Buffered(1))
    in_specs = [pl.BlockSpec((tm, d), prow),
                pl.BlockSpec((ms, d), lambda j, i: (0, 0)),
                pl.BlockSpec((None, tn, d), lambda j, i: (0, blk0 + j, 0), **w_mode)]
    args = [hp, hs, wt]
    if shift:
        per = tn // shift
        in_specs.append(pl.BlockSpec((None, shift, d), lambda j, i: (0, (blk0 + j + 1) * per, 0)))
        args.append(wt)
    if kind in ("qnorm", "knorm"):
        in_specs.append(pl.BlockSpec((1, tn), lambda j, i: (0, 0)))
        args.append(nw)
    n_cols = tn * n_tiles
    p_block = pl.BlockSpec((tm, tn), lambda j, i: (jnp.minimum(i, n_prompt - 1), j))
    s_block = pl.BlockSpec((ms, tn), lambda j, i: (0, j))
    out_shape = tuple(jax.ShapeDtypeStruct((mp, n_cols), t) for t in dtypes) + \
        tuple(jax.ShapeDtypeStruct((ms, n_cols), t) for t in dtypes)
    out_specs = (p_block,) * len(dtypes) + (s_block,) * len(dtypes)
    return pl.pallas_call(
        functools.partial(_proj_body, kind=kind, n_out=len(dtypes), shift=shift),
        grid=grid,
        in_specs=in_specs,
        out_specs=out_specs,
        out_shape=out_shape,
        scratch_shapes=[pltpu.VMEM((d, tn), BF16)],
        compiler_params=_cparams(("parallel", "arbitrary"), vmem_mib=56),
        name="proj_" + kind,
    )(*args)


def _forget_weight(wft_ref):
    wft = wft_ref[...]
    pad = jnp.zeros((LANES - wft.shape[0], wft.shape[1]), F32)
    return jnp.concatenate([wft, pad], axis=0).T.astype(BF16)


def _norm_logf_body(x_ref, nw_ref, wft_ref, brow_ref, l_ref, c2_ref,
                    h_ref, lf_ref, fcol_ref, frow_ref, qbias_ref, kbias_ref, ccol_s, wf_s):
    i = pl.program_id(0)
    tm = x_ref.shape[0]

    @pl.when(i == 0)
    def _():
        ccol_s[...] = jnp.zeros_like(ccol_s)
        wf_s[...] = _forget_weight(wft_ref)

    x = x_ref[...]
    ms = jnp.mean(x * x, axis=-1, keepdims=True)
    h = (x * lax.rsqrt(ms + RMS_EPS) * nw_ref[...]).astype(BF16)
    h_ref[...] = h
    lf = _log_sigmoid(_dot(h, wf_s[...]) + brow_ref[...])
    lf_ref[...] = lf[:, :N_HEADS]
    f_col = _dot_split_rhs(l_ref[...], lf, 3) + ccol_s[...]
    fcol_ref[...] = f_col[:, :N_HEADS]
    ccol_s[...] = f_col[tm - 1:tm, :]
    frow_ref[...] = f_col.T[:N_HEADS, :]
    lane = lax.broadcasted_iota(jnp.int32, f_col.shape, 1)
    f2 = f_col * LOG2E
    qbias_ref[...] = _split_bias_rows(f2 - c2_ref[...], lane)
    kbias_ref[...] = _split_bias_rows(-f2, lane)


def _norm_logf_prompt(x2d, nw_row, wt, f_row0, b_row, c2):
    m, d = x2d.shape
    tm = min(m, 512)
    l_mat = _tri(tm, "row_ge_col")
    const = lambda i: (0, 0)
    return pl.pallas_call(
        _norm_logf_body,
        grid=(m // tm,),
        in_specs=[pl.BlockSpec((tm, d), lambda i: (i, 0)),
                  pl.BlockSpec((1, d), const),
                  pl.BlockSpec((None, N_HEADS, d), lambda i: (0, f_row0 // N_HEADS, 0)),
                  pl.BlockSpec((1, LANES), const),
                  pl.BlockSpec((tm, tm), const),
                  pl.BlockSpec((1, 1), const)],
        out_specs=(pl.BlockSpec((tm, d), lambda i: (i, 0)),
                   pl.BlockSpec((tm, N_HEADS), lambda i: (i, 0)),
                   pl.BlockSpec((tm, N_HEADS), lambda i: (i, 0)),
                   pl.BlockSpec((N_HEADS, tm), lambda i: (0, i)),
                   pl.BlockSpec((tm, LANES), lambda i: (i, 0)),
                   pl.BlockSpec((tm, LANES), lambda i: (i, 0))),
        out_shape=(jax.ShapeDtypeStruct((m, d), BF16),
                   jax.ShapeDtypeStruct((m, N_HEADS), F32),
                   jax.ShapeDtypeStruct((m, N_HEADS), F32),
                   jax.ShapeDtypeStruct((N_HEADS, m), F32),
                   jax.ShapeDtypeStruct((m, LANES), BF16),
                   jax.ShapeDtypeStruct((m, LANES), BF16)),
        scratch_shapes=[pltpu.VMEM((1, LANES), F32), pltpu.VMEM((d, LANES), BF16)],
        compiler_params=_cparams(("arbitrary",)),
        name="norm_logf_prompt",
    )(x2d, nw_row, wt, b_row, l_mat, c2)


def _logf_sample_body(h_ref, wft_ref, brow_ref, bu_ref, x_ref, ms_ref, lf_ref, c_ref, r_ref, *, kb):
    lf = _log_sigmoid(_dot(h_ref[...], _forget_weight(wft_ref)) + brow_ref[...])
    lf_ref[...] = lf[:, :N_HEADS]
    lft = lf.T[:2 * SUBLANES, :]
    c_ref[...] = _dot_split_lhs(lft, bu_ref[...], 3)[:N_HEADS, :]
    n_blocks = x_ref.shape[1] // kb
    carry = jnp.zeros((x_ref.shape[0], 1), F32)
    for blk in range(n_blocks - 1, -1, -1):
        x = x_ref[:, blk * kb:(blk + 1) * kb]
        cum = _dot_split_lhs(x, ms_ref[...], 3)
        r_ref[:, blk * kb:(blk + 1) * kb] = cum + carry
        carry = carry + cum[:, 0:1] + x[:, 0:1]


def _logf_sample(h_s, wt, f_row0, b_row, past_logf_t, dec_seq):
    n_rows, d = h_s.shape
    n_bh, p_len = past_logf_t.shape
    kb = min(p_len, 512)
    r = lax.broadcasted_iota(jnp.int32, (n_rows, n_rows), 0)
    c = lax.broadcasted_iota(jnp.int32, (n_rows, n_rows), 1)
    bu = ((r // dec_seq == c // dec_seq) & (r <= c)).astype(BF16)
    ms = _tri(kb, "row_gt_col")
    whole = lambda shape: pl.BlockSpec(shape, lambda i: (0,) * len(shape))
    return pl.pallas_call(
        functools.partial(_logf_sample_body, kb=kb),
        grid=(1,),
        in_specs=[whole((n_rows, d)),
                  pl.BlockSpec((None, N_HEADS, d), lambda i: (0, f_row0 // N_HEADS, 0)),
                  whole((1, LANES)), whole((n_rows, n_rows)), whole((n_bh, p_len)), whole((kb, kb))],
        out_specs=(whole((n_rows, N_HEADS)), whole((N_HEADS, n_rows)), whole((n_bh, p_len))),
        out_shape=(jax.ShapeDtypeStruct((n_rows, N_HEADS), F32),
                   jax.ShapeDtypeStruct((N_HEADS, n_rows), F32),
                   jax.ShapeDtypeStruct((n_bh, p_len), F32)),
        compiler_params=_cparams(("arbitrary",), vmem_mib=32),
        name="logf_sample",
    )(h_s, wt, b_row, bu, past_logf_t, ms)


SB_TQ = 256


def _sb_block(q, k, v, m_mat, carry, mask):
    z = _dot_nt(q, k)
    lsn = jnp.minimum(-z, 0.0) - jnp.log(1.0 + jnp.exp2(-jnp.abs(z))) * LOG2E
    lsp = lsn + z
    if mask is not None:
        lsn = jnp.where(mask, lsn, 0.0)
    cum = _dot_split_lhs(lsn, m_mat, 2)
    if carry is not None:
        cum = cum + carry
    w = jnp.exp2(lsp + cum)
    if mask is not None:
        w = jnp.where(mask, w, 0.0)
    o = _dot(w.astype(BF16), v)
    new_carry = cum[:, 0:1] + lsn[:, 0:1]
    return o, new_carry


def _sb_fast_body(*refs):
    _sb_fast_step(pl.program_id(0), *refs)


def _sb_fast_step(i, q_ref, kd_ref, vd_ref, kp_ref, vp_ref, gs_ref, m_ref, a_ref, c_ref, after_head=None):
    tq = q_ref.shape[0]
    has_prev = i > 0
    row = lax.broadcasted_iota(jnp.int32, (tq, tq), 0)
    col = lax.broadcasted_iota(jnp.int32, (tq, tq), 1)
    dmask = col < row
    pmask = jnp.logical_and(has_prev, col >= 0)
    lane = lax.broadcasted_iota(jnp.int32, (tq, LANES), 1)
    m_mat = m_ref[...]
    cacc = jnp.full((tq, LANES), NEG_BIG, F32)
    for hh in range(N_HEADS):
        sl = slice(hh * HEAD_DIM, (hh + 1) * HEAD_DIM)
        q = q_ref[:, sl]
        od, cd = _sb_block(q, kd_ref[:, sl], vd_ref[:, sl], m_mat, None, dmask)
        op, cp = _sb_block(q, kp_ref[:, sl], vp_ref[:, sl], m_mat, cd, pmask)
        a_ref[:, sl] = ((od + op) * gs_ref[:, sl].astype(F32)).astype(BF16)
        cacc = jnp.where(lane == hh, cp, cacc)
        if after_head and hh in after_head:
            after_head[hh]()
    c_ref[...] = cacc


def _sb_rest_body(q_ref, k_ref, v_ref, gs_ref, cin_ref, afast_ref, m_ref, a_ref, o_s, c_s):
    i = pl.program_id(0)
    s = pl.program_id(1)
    j = i - 2 - s
    tq = q_ref.shape[0]

    @pl.when(s == 0)
    def _():
        o_s[...] = jnp.zeros_like(o_s)
        c_s[...] = cin_ref[...]

    active = jnp.logical_and(j >= 0, jnp.max(c_s[...]) > SB_SKIP_LOG2)

    @pl.when(active)
    def _():
        lane = lax.broadcasted_iota(jnp.int32, (tq, LANES), 1)
        m_mat = m_ref[...]
        c_all = c_s[...]
        cacc = c_all
        for hh in range(N_HEADS):
            sl = slice(hh * HEAD_DIM, (hh + 1) * HEAD_DIM)
            carry = jnp.sum(jnp.where(lane == hh, c_all, 0.0), axis=1, keepdims=True)
            o, cn = _sb_block(q_ref[:, sl], k_ref[:, sl], v_ref[:, sl], m_mat, carry, None)
            o_s[:, sl] += o
            cacc = jnp.where(lane == hh, cn, cacc)
        c_s[...] = cacc

    @pl.when(s == pl.num_programs(1) - 1)
    def _():
        a_ref[...] = (afast_ref[...].astype(F32) + o_s[...] * gs_ref[...].astype(F32)).astype(BF16)


def _sb_prompt(q, k, v, gs, fox_dec=None):
    t, w = q.shape
    tq = min(SB_TQ, t)
    nq = t // tq
    m_mat = _tri(tq, "row_gt_col")
    blk = lambda f: pl.BlockSpec((tq, w), f)
    sb_in = [blk(lambda i: (i, 0)), blk(lambda i: (i, 0)), blk(lambda i: (i, 0)),
             blk(lambda i: (jnp.maximum(i - 1, 0), 0)), blk(lambda i: (jnp.maximum(i - 1, 0), 0)),
             blk(lambda i: (i, 0)),
             pl.BlockSpec((tq, tq), lambda i: (0, 0))]
    sb_args = (q, k, v, k, v, gs, m_mat)
    sb_out_specs = (blk(lambda i: (i, 0)), pl.BlockSpec((tq, LANES), lambda i: (i, 0)))
    sb_out_shape = (jax.ShapeDtypeStruct((t, w), BF16), jax.ShapeDtypeStruct((t, LANES), F32))
    a_dec = None
    if fox_dec is not None:
        n_b, kb, nkb = _fox_dec_geometry(fox_dec[0], fox_dec[3])
        if n_b * nkb != nq:
            a_dec = _fox_decode(*fox_dec)
            fox_dec = None
    if fox_dec is None:
        a_fast, carry = pl.pallas_call(
            _sb_fast_body, grid=(nq,), in_specs=sb_in, out_specs=sb_out_specs, out_shape=sb_out_shape,
            compiler_params=_cparams(("parallel",), vmem_mib=48), name="sb_prompt_fast",
        )(*sb_args)
    else:
        dec_in, dec_out, dec_scratch = _fox_dec_specs(fox_dec[0].shape[1], kb,
                                                      lambda i: i // nkb, lambda i: i % nkb)
        n_sb, n_dec = len(sb_in), len(dec_in)

        def fused_body(*refs):
            i = pl.program_id(0)
            outs = refs[n_sb + n_dec:n_sb + n_dec + 3]
            begin, chunks, end = _fox_dec_parts(i % nkb, nkb, *refs[n_sb:n_sb + n_dec], outs[2],
                                                *refs[n_sb + n_dec + 3:])
            slots = {}
            for ci, fn in enumerate(chunks):
                slots.setdefault(((ci + 1) * N_HEADS - 1) // len(chunks), []).append(fn)
            after = {hh: (lambda fns=fns: [fn() for fn in fns]) for hh, fns in slots.items()}
            begin()
            _sb_fast_step(i, *refs[:n_sb], outs[0], outs[1], after_head=after)
            end()

        a_fast, carry, a_dec = pl.pallas_call(
            fused_body, grid=(nq,), in_specs=sb_in + dec_in,
            out_specs=sb_out_specs + (dec_out,),
            out_shape=sb_out_shape + (jax.ShapeDtypeStruct(fox_dec[0].shape, BF16),),
            scratch_shapes=dec_scratch,
            compiler_params=_cparams(("arbitrary",), vmem_mib=56), name="sb_prompt_fox_decode",
        )(*sb_args, *fox_dec)
    if nq <= 2:
        return a_fast, a_dec

    def rest(a_fast, carry):
        kidx = lambda i, s: (jnp.maximum(i - 2 - s, 0), 0)
        return pl.pallas_call(
            _sb_rest_body,
            grid=(nq, nq - 2),
            in_specs=[blk(lambda i, s: (i, 0)), blk(kidx), blk(kidx), blk(lambda i, s: (i, 0)),
                      pl.BlockSpec((tq, LANES), lambda i, s: (i, 0)),
                      blk(lambda i, s: (i, 0)),
                      pl.BlockSpec((tq, tq), lambda i, s: (0, 0))],
            out_specs=blk(lambda i, s: (i, 0)),
            out_shape=jax.ShapeDtypeStruct((t, w), BF16),
            scratch_shapes=[pltpu.VMEM((tq, w), F32), pltpu.VMEM((tq, LANES), F32)],
            compiler_params=_cparams(("parallel", "arbitrary"), vmem_mib=48),
            name="sb_prompt_rest",
        )(q, k, v, gs, carry, a_fast, m_mat)

    need_rest = jnp.max(carry[2 * tq:, :]) > SB_SKIP_LOG2
    return lax.cond(need_rest, rest, lambda a, c: a, a_fast, carry), a_dec


FOX_T = 512
FOX_WIN_TILES = (1, 2, 3, 4, 5, 6)
FOX_ZB_MAX = 40.0
FOX_SKIP_LOG = -104.0


def _head_column(blk, hh):
    lane8 = lax.broadcasted_iota(jnp.int32, blk.shape, 1)
    return jnp.sum(jnp.where(lane8 == hh, blk, 0.0), axis=1, keepdims=True)


BIAS_ONE_LANE = 3 * N_HEADS


def _split_bias_rows(f, lane):
    hi = f.astype(BF16).astype(F32)
    r1 = f - hi
    mid = r1.astype(BF16).astype(F32)
    low = (r1 - mid).astype(BF16).astype(F32)
    out = jnp.where(lane < N_HEADS, hi,
                    jnp.where(lane < 2 * N_HEADS, pltpu.roll(mid, N_HEADS, axis=1),
                              jnp.where(lane < BIAS_ONE_LANE, pltpu.roll(low, 2 * N_HEADS, axis=1),
                                        jnp.where(lane == BIAS_ONE_LANE, 1.0, 0.0))))
    return out.astype(BF16)


def _bias_selectors():
    src = lax.broadcasted_iota(jnp.int32, (N_HEADS, LANES, LANES), 1)
    dst = lax.broadcasted_iota(jnp.int32, (N_HEADS, LANES, LANES), 2)
    head = lax.broadcasted_iota(jnp.int32, (N_HEADS, LANES, LANES), 0)

    def sel(bias_base, one_base):
        term = dst - bias_base
        takes_bias = (term >= 0) & (term < 3) & (src == term * N_HEADS + head)
        takes_one = (dst >= one_base) & (dst < one_base + 3) & (src == BIAS_ONE_LANE)
        return (takes_bias | takes_one).astype(BF16)

    return jnp.stack([sel(0, 3), sel(3, 0)])


def _fox_fast_body(live_ref, q_ref, k_ref, v_ref, qb_ref, kb_ref, sel_ref, gs_ref, a_ref, kaug_s):
    hh = pl.program_id(0)
    qb = pl.program_id(1)
    nq = pl.num_programs(1)
    tb = q_ref.shape[0]
    t = k_ref.shape[0]

    @pl.when(qb == 0)
    def _():
        kaug_s[...] = _dot(kb_ref[...], sel_ref[1]).astype(BF16)

    q_start = pl.multiple_of(qb * tb, tb)
    q_end = q_start + tb
    q2 = jnp.concatenate([q_ref[...], _dot(qb_ref[...], sel_ref[0]).astype(BF16)], axis=1)
    row = lax.broadcasted_iota(jnp.int32, (tb, 1), 0)
    live = live_ref[hh * nq + qb]

    def run(tiles, single):
        span = min(tiles * tb, t)
        lane_w = lax.broadcasted_iota(jnp.int32, (span, LANES), 1)
        ones_blk = jnp.where(lane_w == 0, 1.0, 0.0).astype(BF16)
        col = lax.broadcasted_iota(jnp.int32, (1, span), 1)

        def window(j, acc):
            upper = q_end - j * span
            start = pl.multiple_of(jnp.maximum(upper - span, 0), tb)
            k2 = jnp.concatenate([k_ref[pl.ds(start, span), :], kaug_s[pl.ds(start, span), :]], axis=1)
            p = jnp.exp2(_dot_nt(q2, k2))
            valid = jnp.logical_and(col - (q_start - start) <= row, col < upper - start)
            p = jnp.where(valid, p, 0.0).astype(BF16)
            v2 = jnp.concatenate([v_ref[pl.ds(start, span), :], ones_blk], axis=1)
            return acc + _dot(p, v2)

        acc = jnp.zeros((tb, 2 * HEAD_DIM), F32)
        if single:
            acc = window(0, acc)
        else:
            acc = lax.fori_loop(0, (live + (tiles - 1)) // tiles, window, acc)
        o = acc[:, :HEAD_DIM] / acc[:, HEAD_DIM:HEAD_DIM + 1]
        a_ref[...] = (o * gs_ref[...].astype(F32)).astype(BF16)

    bounds = (0,) + FOX_WIN_TILES
    for lo_t, hi_t in zip(bounds[:-1], bounds[1:]):
        pl.when(jnp.logical_and(live > lo_t, live <= hi_t))(functools.partial(run, hi_t, True))
    pl.when(live > FOX_WIN_TILES[-1])(functools.partial(run, FOX_WIN_TILES[-1], False))


def _fox_slow_body(q_ref, k_ref, v_ref, fcol_ref, frow_ref, gs_ref, a_ref):
    hh = pl.program_id(0)
    qb = pl.program_id(1)
    tb = q_ref.shape[0]
    q = q_ref[...]
    fq = _head_column(fcol_ref[...], hh) * LOG2E

    def scores(kb):
        start = pl.multiple_of(kb * tb, tb)
        k = k_ref[pl.ds(start, tb), :]
        fk = frow_ref[0, :, pl.ds(start, tb)] * LOG2E
        return _dot_nt(q, k) + (fq - fk), v_ref[pl.ds(start, tb), :]

    def update(s, v, carry):
        m, l, acc = carry
        m_new = jnp.maximum(m, jnp.max(s, axis=1, keepdims=True))
        alpha = jnp.exp2(m - m_new)
        p = jnp.exp2(s - m_new)
        l = alpha * l + jnp.sum(p, axis=1, keepdims=True)
        acc = alpha * acc + _dot(p.astype(BF16), v)
        return m_new, l, acc

    def body(kb, carry):
        s, v = scores(kb)
        return update(s, v, carry)

    init = (jnp.full((tb, 1), NEG_BIG, F32), jnp.zeros((tb, 1), F32), jnp.zeros((tb, HEAD_DIM), F32))
    carry = lax.fori_loop(0, qb, body, init)
    s, v = scores(qb)
    row = lax.broadcasted_iota(jnp.int32, (tb, tb), 0)
    col = lax.broadcasted_iota(jnp.int32, (tb, tb), 1)
    s = jnp.where(col <= row, s, NEG_BIG)
    _, l, acc = update(s, v, carry)
    a_ref[...] = ((acc / l) * gs_ref[...].astype(F32)).astype(BF16)


def _fox_prompt(q, k, v, f_col, f_row, gs, zb, q_bias, k_bias):
    t, w = q.shape
    tb = min(FOX_T, t)
    nq = t // tb
    out_shape = jax.ShapeDtypeStruct((t, w), BF16)

    def fast(q, k, v, f_col, f_row, gs, q_bias, k_bias):
        f_start = f_row[:, ::tb]
        f_end = f_row[:, tb - 1::tb]
        dead = (f_start[:, :, None] - f_end[:, None, :]) < FOX_SKIP_LOG
        lo = jnp.sum(dead, axis=2).astype(jnp.int32)
        live = (jnp.arange(1, nq + 1, dtype=jnp.int32)[None, :] - lo).reshape(-1)
        head_blk = lambda h, i, live: (i, h)
        head_all = lambda h, i, live: (0, h)
        grid_spec = pltpu.PrefetchScalarGridSpec(
            num_scalar_prefetch=1,
            grid=(N_HEADS, nq),
            in_specs=[pl.BlockSpec((tb, HEAD_DIM), head_blk),
                      pl.BlockSpec((t, HEAD_DIM), head_all),
                      pl.BlockSpec((t, HEAD_DIM), head_all),
                      pl.BlockSpec((tb, LANES), lambda h, i, live: (i, 0)),
                      pl.BlockSpec((t, LANES), lambda h, i, live: (0, 0), pipeline_mode=pl.Buffered(1)),
                      pl.BlockSpec((2, None, LANES, LANES), lambda h, i, live: (0, h, 0, 0)),
                      pl.BlockSpec((tb, HEAD_DIM), head_blk)],
            out_specs=pl.BlockSpec((tb, HEAD_DIM), head_blk),
            scratch_shapes=[pltpu.VMEM((t, LANES), BF16)])
        return pl.pallas_call(
            _fox_fast_body, grid_spec=grid_spec, out_shape=out_shape,
            compiler_params=_cparams(("parallel", "arbitrary"), vmem_mib=48),
            name="fox_prompt_fast",
        )(live, q, k, v, q_bias, k_bias, _bias_selectors(), gs)

    def slow(q, k, v, f_col, f_row, gs, q_bias, k_bias):
        f_row3 = f_row.reshape(N_HEADS, 1, t)
        return pl.pallas_call(
            _fox_slow_body,
            grid=(N_HEADS, nq),
            in_specs=[pl.BlockSpec((tb, HEAD_DIM), lambda h, i: (i, h)),
                      pl.BlockSpec((t, HEAD_DIM), lambda h, i: (0, h)),
                      pl.BlockSpec((t, HEAD_DIM), lambda h, i: (0, h)),
                      pl.BlockSpec((tb, N_HEADS), lambda h, i: (i, 0)),
                      pl.BlockSpec((1, 1, t), lambda h, i: (h, 0, 0)),
                      pl.BlockSpec((tb, HEAD_DIM), lambda h, i: (i, h))],
            out_specs=pl.BlockSpec((tb, HEAD_DIM), lambda h, i: (i, h)),
            out_shape=out_shape,
            compiler_params=_cparams(("parallel", "parallel"), vmem_mib=48),
            name="fox_prompt_slow",
        )(q, k, v, f_col, f_row3, gs)

    return lax.cond(zb <= FOX_ZB_MAX, fast, slow, q, k, v, f_col, f_row, gs, q_bias, k_bias)


DEC_KB = 2048
SB_DEC_WIN = 256


def _own_head_mask():
    row = lax.broadcasted_iota(jnp.int32, (SUBLANES, W_ATT), 0)
    col = lax.broadcasted_iota(jnp.int32, (SUBLANES, W_ATT), 1)
    return (col // HEAD_DIM) == row


def _expand_q(q_ref, qx_s):
    n_tok = q_ref.shape[0]
    qf = q_ref[...].astype(F32)
    own = _own_head_mask()
    for tkn in range(n_tok):
        rep = jnp.broadcast_to(qf[tkn:tkn + 1, :], (SUBLANES, W_ATT))
        qx_s[tkn * N_HEADS:(tkn + 1) * N_HEADS, :] = jnp.where(own, rep, 0.0)


def _pad_new(kn_ref, vn_ref, knp_s, vnp_s):
    n_tok = kn_ref.shape[0]
    knp_s[...] = jnp.zeros_like(knp_s)
    vnp_s[...] = jnp.zeros_like(vnp_s)
    knp_s[0:n_tok, :] = kn_ref[...]
    vnp_s[0:n_tok, :] = vn_ref[...]


def _gather_heads(c_ref, dst_s):
    n_keys = dst_s.shape[0]
    for hh in range(N_HEADS):
        dst_s[:, hh * HEAD_DIM:(hh + 1) * HEAD_DIM] = (
            c_ref[0, pl.ds(hh, n_keys, stride=N_HEADS), :].astype(BF16))


def _collapse_heads(o_full, out_s):
    n_tok = o_full.shape[0] // N_HEADS
    own = _own_head_mask()
    for tkn in range(n_tok):
        blk = jnp.where(own, o_full[tkn * N_HEADS:(tkn + 1) * N_HEADS, :], 0.0)
        out_s[tkn:tkn + 1, :] = jnp.sum(blk, axis=0, keepdims=True)


def _sb_dec_fast_body(q_ref, kn_ref, vn_ref, kc_ref, vc_ref, gs_ref, mn_ref, mc_ref, a_ref, c_ref,
                      qx_s, knp_s, vnp_s, kx_s, vx_s, out_s):
    n_rows = qx_s.shape[0]
    _expand_q(q_ref, qx_s)
    _pad_new(kn_ref, vn_ref, knp_s, vnp_s)
    qx = qx_s[...].astype(BF16)
    row = lax.broadcasted_iota(jnp.int32, (n_rows, LANES), 0)
    col = lax.broadcasted_iota(jnp.int32, (n_rows, LANES), 1)
    mask = col < row // N_HEADS
    o_new, carry = _sb_block(qx, knp_s[...], vnp_s[...], mn_ref[...], None, mask)
    _gather_heads(kc_ref, kx_s)
    _gather_heads(vc_ref, vx_s)
    o_win, carry = _sb_block(qx, kx_s[...], vx_s[...], mc_ref[...], carry, None)
    _collapse_heads(o_new + o_win, out_s)
    a_ref[...] = (out_s[...] * gs_ref[...].astype(F32)).astype(BF16)
    c_ref[...] = jnp.broadcast_to(carry, c_ref.shape)


def _sb_dec_rest_body(q_ref, kc_ref, vc_ref, gs_ref, cin_ref, afast_ref, mc_ref, a_ref,
                      qx_s, kx_s, vx_s, acc_s, c_s, out_s):
    s = pl.program_id(1)

    @pl.when(s == 0)
    def _():
        _expand_q(q_ref, qx_s)
        acc_s[...] = jnp.zeros_like(acc_s)
        c_s[...] = cin_ref[:, 0:1]

    @pl.when(jnp.max(c_s[...]) > SB_SKIP_LOG2)
    def _():
        _gather_heads(kc_ref, kx_s)
        _gather_heads(vc_ref, vx_s)
        o, carry = _sb_block(qx_s[...].astype(BF16), kx_s[...], vx_s[...], mc_ref[...], c_s[...], None)
        acc_s[...] += o
        c_s[...] = carry

    @pl.when(s == pl.num_programs(1) - 1)
    def _():
        _collapse_heads(acc_s[...], out_s)
        a_ref[...] = (afast_ref[...].astype(F32) + out_s[...] * gs_ref[...].astype(F32)).astype(BF16)


def _sb_decode(q, k_new, v_new, k_cache, v_cache, gs, dec_seq, host=None):
    n_rows_all, w = q.shape
    n_b = n_rows_all // dec_seq
    p_len = k_cache.shape[1] // N_HEADS
    win = min(SB_DEC_WIN, p_len)
    n_win = p_len // win
    n_rows = dec_seq * N_HEADS
    m_new = _tri(LANES, "row_gt_col")
    m_win = _tri(win, "row_gt_col")
    tok = lambda b: (b, 0)
    newest = pl.BlockSpec((1, win * N_HEADS, HEAD_DIM), lambda b: (b, n_win - 1, 0))
    rider = dict(
        body=_sb_dec_fast_body, n_steps=n_b,
        in_specs=[pl.BlockSpec((dec_seq, w), tok), pl.BlockSpec((dec_seq, w), tok),
                  pl.BlockSpec((dec_seq, w), tok),
                  newest, newest,
                  pl.BlockSpec((dec_seq, w), tok),
                  pl.BlockSpec((LANES, LANES), lambda b: (0, 0)),
                  pl.BlockSpec((win, win), lambda b: (0, 0))],
        args=(q, k_new, v_new, k_cache, v_cache, gs, m_new, m_win),
        out_specs=(pl.BlockSpec((dec_seq, w), tok), pl.BlockSpec((n_rows, LANES), tok)),
        out_shape=(jax.ShapeDtypeStruct((n_rows_all, w), BF16),
                   jax.ShapeDtypeStruct((n_b * n_rows, LANES), F32)),
        scratch=[pltpu.VMEM((n_rows, w), F32), pltpu.VMEM((LANES, w), BF16),
                 pltpu.VMEM((LANES, w), BF16), pltpu.VMEM((win, w), BF16),
                 pltpu.VMEM((win, w), BF16), pltpu.VMEM((dec_seq, w), F32)])
    hosted = None
    if host is not None:
        hosted, a_fast, carry = host(rider)
    else:
        a_fast, carry = pl.pallas_call(
            rider["body"], grid=(n_b,), in_specs=rider["in_specs"], out_specs=rider["out_specs"],
            out_shape=rider["out_shape"], scratch_shapes=rider["scratch"],
            compiler_params=_cparams(("parallel",), vmem_mib=48), name="sb_decode_fast",
        )(*rider["args"])
    if n_win <= 1:
        return a_fast, hosted

    def rest(a_fast, carry):
        tok2 = lambda b, s: (b, 0)
        older = pl.BlockSpec((1, win * N_HEADS, HEAD_DIM), lambda b, s: (b, n_win - 2 - s, 0))
        return pl.pallas_call(
            _sb_dec_rest_body,
            grid=(n_b, n_win - 1),
            in_specs=[pl.BlockSpec((dec_seq, w), tok2),
                      older, older,
                      pl.BlockSpec((dec_seq, w), tok2),
                      pl.BlockSpec((n_rows, LANES), tok2),
                      pl.BlockSpec((dec_seq, w), tok2),
                      pl.BlockSpec((win, win), lambda b, s: (0, 0))],
            out_specs=pl.BlockSpec((dec_seq, w), tok2),
            out_shape=jax.ShapeDtypeStruct((n_rows_all, w), BF16),
            scratch_shapes=[pltpu.VMEM((n_rows, w), F32), pltpu.VMEM((win, w), BF16),
                            pltpu.VMEM((win, w), BF16), pltpu.VMEM((n_rows, w), F32),
                            pltpu.VMEM((n_rows, 1), F32), pltpu.VMEM((dec_seq, w), F32)],
            compiler_params=_cparams(("parallel", "arbitrary"), vmem_mib=48),
            name="sb_decode_rest",
        )(q, k_cache, v_cache, gs, carry, a_fast, m_win)

    need_rest = jnp.max(carry) > SB_SKIP_LOG2
    return lax.cond(need_rest, rest, lambda a, c: a, a_fast, carry), hosted


def _fox_dec_body(*refs):
    _fox_dec_step(pl.program_id(1), pl.num_programs(1), *refs)


FOX_DEC_CHUNK = 1024


def _fox_dec_step(s, n_s, *refs):
    begin, chunks, end = _fox_dec_parts(s, n_s, *refs)
    begin()
    for chunk_fn in chunks:
        chunk_fn()
    end()


def _fox_dec_parts(s, n_s, q_ref, kn_ref, vn_ref, kc_ref, vc_ref, gs_ref, c_ref, r_ref, a_ref,
                   acc_s, m_s, l_s):
    n_rows = q_ref.shape[1]
    q = q_ref[0]
    own = lax.broadcasted_iota(jnp.int32, (SUBLANES, LANES), 0) == \
        lax.broadcasted_iota(jnp.int32, (SUBLANES, LANES), 1) % N_HEADS
    head_bias = jnp.where(own, 0.0, NEG_BIG)

    def update(sc, v):
        m = m_s[...]
        m_new = jnp.maximum(m, jnp.max(sc, axis=1, keepdims=True))
        alpha = jnp.exp2(m - m_new)
        p = jnp.exp2(sc - m_new)
        l_s[...] = alpha * l_s[...] + jnp.sum(p, axis=1, keepdims=True)
        acc_s[...] = alpha * acc_s[...] + _dot(p.astype(BF16), v)
        m_s[...] = m_new

    def add_key_bias(z, decay_row):
        n = decay_row.shape[1]
        tile = jnp.concatenate([head_bias] * (n // LANES), axis=1) + decay_row * LOG2E
        return (z.reshape(n_rows // SUBLANES, SUBLANES, n) + tile[None]).reshape(n_rows, n)

    def begin():
        @pl.when(s == 0)
        def _():
            m_s[...] = jnp.full_like(m_s, NEG_BIG)
            l_s[...] = jnp.zeros_like(l_s)
            acc_s[...] = jnp.zeros_like(acc_s)
            row = lax.broadcasted_iota(jnp.int32, (n_rows, LANES), 0)
            col = lax.broadcasted_iota(jnp.int32, (n_rows, LANES), 1)
            sc = add_key_bias(_dot_nt(q, kn_ref[0]), -c_ref[0])
            sc = jnp.where(col // N_HEADS <= row // N_HEADS, sc, NEG_BIG)
            update(sc, vn_ref[0])

    n_keys_rows = kc_ref.shape[1]
    chunk = min(FOX_DEC_CHUNK * N_HEADS, n_keys_rows)

    def one_chunk(c0):
        k2 = kc_ref[0, c0:c0 + chunk, :].astype(BF16)
        v2 = vc_ref[0, c0:c0 + chunk, :].astype(BF16)
        update(add_key_bias(_dot_nt(q, k2), r_ref[0, :, c0:c0 + chunk]), v2)

    def end():
        @pl.when(s == n_s - 1)
        def _():
            a_ref[0] = ((acc_s[...] / l_s[...]) * gs_ref[0].astype(F32)).astype(BF16)

    return begin, [functools.partial(one_chunk, c0) for c0 in range(0, n_keys_rows, chunk)], end


def _fox_dec_geometry(q3, k_cache):
    n_b = q3.shape[0]
    p_len = k_cache.shape[1] // N_HEADS
    kb = min(DEC_KB, p_len)
    return n_b, kb, p_len // kb


def _fox_dec_specs(n_rows, kb, stream, block):
    tok = pl.BlockSpec((1, n_rows, HEAD_DIM), lambda *g: (stream(*g), 0, 0))
    cache = pl.BlockSpec((1, kb * N_HEADS, HEAD_DIM), lambda *g: (stream(*g), block(*g), 0))
    in_specs = [tok, tok, tok, cache, cache, tok,
                pl.BlockSpec((1, 1, LANES), lambda *g: (stream(*g), 0, 0)),
                pl.BlockSpec((1, 1, kb * N_HEADS), lambda *g: (stream(*g), 0, block(*g)))]
    scratch = [pltpu.VMEM((n_rows, HEAD_DIM), F32), pltpu.VMEM((n_rows, 1), F32),
               pltpu.VMEM((n_rows, 1), F32)]
    return in_specs, tok, scratch


def _fox_decode(q3, kn3, vn3, k_cache, v_cache, gs3, c_row, r_row):
    n_b, kb, nkb = _fox_dec_geometry(q3, k_cache)
    in_specs, out_spec, scratch = _fox_dec_specs(q3.shape[1], kb, lambda b, s: b, lambda b, s: s)
    return pl.pallas_call(
        _fox_dec_body,
        grid=(n_b, nkb),
        in_specs=in_specs,
        out_specs=out_spec,
        out_shape=jax.ShapeDtypeStruct(q3.shape, BF16),
        scratch_shapes=scratch,
        compiler_params=_cparams(("parallel", "arbitrary"), vmem_mib=56),
        name="fox_decode",
    )(q3, kn3, vn3, k_cache, v_cache, gs3, c_row, r_row)


POST_TM = 512


def _post_body(x_ref, asb_ref, afx_ref, msb_ref, mfx_ref, wsb_ref, wfx_ref, wo_ref, y_ref):
    u_sb = _dot(asb_ref[...], wsb_ref[...])
    u_fx = _dot(afx_ref[...], wfx_ref[...])
    merged = msb_ref[...].astype(F32) * u_sb + mfx_ref[...].astype(F32) * u_fx
    y_ref[...] = x_ref[...] + _dot(merged.astype(BF16), wo_ref[...])


def _post(x2d, a_sb, a_fx, m_sig, w_sb, w_fx, w_o, rider=None):
    m, d = x2d.shape
    w = a_sb.shape[1]
    tm = min(m, POST_TM)
    row = lambda i: (i, 0)
    const = lambda i: (0, 0)
    resident = functools.partial(pl.BlockSpec, index_map=const, pipeline_mode=pl.Buffered(1))
    in_specs = [pl.BlockSpec((tm, d), row), pl.BlockSpec((tm, w), row), pl.BlockSpec((tm, w), row),
                pl.BlockSpec((tm, d), lambda i: (i, 0)), pl.BlockSpec((tm, d), lambda i: (i, 1)),
                resident((w, d)), resident((w, d)), resident((d, d))]
    args = (x2d, a_sb, a_fx, m_sig, m_sig, w_sb, w_fx, w_o)
    y_spec = pl.BlockSpec((tm, d), row)
    y_shape = jax.ShapeDtypeStruct((m, d), F32)
    if rider is None:
        return pl.pallas_call(
            _post_body, grid=(m // tm,), in_specs=in_specs, out_specs=y_spec, out_shape=y_shape,
            compiler_params=_cparams(("parallel",), vmem_mib=56), name="post",
        )(*args)
    assert rider["n_steps"] == m // tm
    n_in, n_rin, n_rout = len(in_specs), len(rider["in_specs"]), len(rider["out_specs"])

    def body(*refs):
        rider_out = refs[n_in + n_rin + 1:n_in + n_rin + 1 + n_rout]
        _post_body(*refs[:n_in], refs[n_in + n_rin])
        rider["body"](*refs[n_in:n_in + n_rin], *rider_out, *refs[n_in + n_rin + 1 + n_rout:])

    return pl.pallas_call(
        body, grid=(m // tm,), in_specs=in_specs + list(rider["in_specs"]),
        out_specs=(y_spec,) + tuple(rider["out_specs"]),
        out_shape=(y_shape,) + tuple(rider["out_shape"]),
        scratch_shapes=rider["scratch"],
        compiler_params=_cparams(("arbitrary",), vmem_mib=58), name="post_with_rider",
    )(*args, *rider["args"])


def _project_all(hp, hs, w, qnw, knw):
    seg = lambda c: c * W_ATT
    q_sb = _proj(hp, hs, w, seg(0), "scale")
    k_sb = _proj(hp, hs, w, seg(1), "kv")
    v_sb = _proj(hp, hs, w, seg(2), "kv")
    g_sb = _proj(hp, hs, w, seg(3), "silu")
    q_fx = _proj(hp, hs, w, seg(4), "qnorm", nw=qnw)
    k_fx = _proj(hp, hs, w, seg(5), "knorm", nw=knw)
    v_fx = _proj(hp, hs, w, seg(6), "kv")
    g_fx = _proj(hp, hs, w, seg(7), "silu")
    m_sig = _proj(hp, hs, w, seg(8) + N_HEADS, "sigmoid", n_tiles=2 * D_MODEL // W_ATT)
    calls = (q_sb, k_sb, v_sb, g_sb, q_fx, k_fx, v_fx, g_fx, m_sig)
    prompt = tuple(o for r in calls for o in r[:len(r) // 2])
    decode = tuple(o for r in calls for o in r[len(r) // 2:])
    return prompt, decode


def kernel(x_prompt, x_sample, cache_sb_k, cache_sb_v, cache_fox_k, cache_fox_v, cache_fox_logf,
           norm_w, w_in, b_forget, q_norm_w, k_norm_w, w_branch_sb, w_branch_fox, w_out):
    depth = norm_w.shape[0]
    assert depth == 1, "single-layer step"
    bsz, seq, d = x_prompt.shape
    assert bsz == 1
    n_dec, dec_seq, _ = x_sample.shape
    p_len = cache_sb_k.shape[2]
    n_main = 8 * W_ATT

    w_t = jnp.swapaxes(w_in, 1, 2)
    b_row = jnp.pad(b_forget[0].astype(F32)[None, :], ((0, 0), (0, LANES - N_HEADS)))
    qnw = jnp.tile(q_norm_w[0].astype(F32), N_HEADS)[None, :]
    knw = jnp.tile(k_norm_w[0].astype(F32), N_HEADS)[None, :]
    nw_row = norm_w[0].astype(F32)[None, :]
    w_sb = w_branch_sb[0].astype(BF16)
    w_fx = w_branch_fox[0].astype(BF16)
    w_o = w_out[0].astype(BF16)

    xp = x_prompt.reshape(seq, d)
    xs = x_sample.reshape(n_dec * dec_seq, d)
    zb = (1.02 * HEAD_DIM * QK_SCALE) * jnp.max(jnp.abs(q_norm_w[0])) * jnp.max(jnp.abs(k_norm_w[0]))
    zb = zb.astype(F32)
    hp, logf, f_col, f_row, q_bias, k_bias = _norm_logf_prompt(
        xp, nw_row, w_t, n_main, b_row, (zb * LOG2E).reshape(1, 1))
    hs = _rmsnorm(xs, nw_row)
    prompt_proj, decode_proj = _project_all(hp, hs, w_t, qnw, knw)

    (q_sb, k_sb, k_sb_b, v_sb, v_sb_b, g_sb, q_fx, k_fx, k_fx_b, v_fx, v_fx_b, g_fx,
     m_sig) = prompt_proj
    (sq_sb, sk_sb, sk_sb_b, sv_sb, sv_sb_b, sg_sb, sq_fx, sk_fx, sk_fx_b, sv_fx, sv_fx_b, sg_fx,
     sm_sig) = decode_proj

    past_logf_t = jnp.transpose(cache_fox_logf[0].astype(F32), (0, 2, 1)).reshape(n_dec * N_HEADS, p_len)
    s_logf, c_all, r_past = _logf_sample(hs, w_t, n_main, b_row, past_logf_t, dec_seq)
    n_th = dec_seq * N_HEADS
    c_row = jnp.transpose(c_all.reshape(N_HEADS, n_dec, dec_seq), (1, 2, 0)).reshape(n_dec, 1, n_th)
    r_row = jnp.transpose(r_past.reshape(n_dec, N_HEADS, p_len), (0, 2, 1)).reshape(n_dec, 1, p_len * N_HEADS)
    by_head = lambda a: a.reshape(n_dec, n_th, HEAD_DIM)
    kc_sb = cache_sb_k[0].reshape(n_dec, p_len * N_HEADS, HEAD_DIM)
    vc_sb = cache_sb_v[0].reshape(n_dec, p_len * N_HEADS, HEAD_DIM)
    kc_fx = cache_fox_k[0].reshape(n_dec, p_len * N_HEADS, HEAD_DIM)
    vc_fx = cache_fox_v[0].reshape(n_dec, p_len * N_HEADS, HEAD_DIM)

    a_sb, sa_fx = _sb_prompt(q_sb, k_sb_b, v_sb_b, g_sb,
                             fox_dec=(by_head(sq_fx), by_head(sk_fx_b), by_head(sv_fx_b), kc_fx, vc_fx,
                                      by_head(sg_fx), c_row, r_row))
    sa_fx = sa_fx.reshape(n_dec * dec_seq, W_ATT)
    a_fx = _fox_prompt(q_fx, k_fx_b, v_fx_b, f_col, f_row, g_fx, zb, q_bias, k_bias)
    post_steps = seq // min(seq, POST_TM)

    def host(rider):
        y, a_fast, carry = _post(xp, a_sb, a_fx, m_sig, w_sb, w_fx, w_o, rider=rider)
        return y, a_fast, carry

    sa_sb, y_prompt = _sb_decode(sq_sb, sk_sb_b, sv_sb_b, kc_sb, vc_sb, sg_sb, dec_seq,
                                 host=host if post_steps == n_dec else None)
    if y_prompt is None:
        y_prompt = _post(xp, a_sb, a_fx, m_sig, w_sb, w_fx, w_o)
    y_prompt = y_prompt.reshape(bsz, seq, d)
    y_sample = _post(xs, sa_sb, sa_fx, sm_sig, w_sb, w_fx, w_o).reshape(n_dec, dec_seq, d)

    hd = (N_HEADS, HEAD_DIM)
    return (y_prompt, y_sample,
            k_sb.reshape(1, bsz, seq, *hd), v_sb.reshape(1, bsz, seq, *hd),
            k_fx.reshape(1, bsz, seq, *hd), v_fx.reshape(1, bsz, seq, *hd),
            logf.reshape(1, bsz, seq, N_HEADS),
            sk_sb.reshape(1, n_dec, dec_seq, *hd), sv_sb.reshape(1, n_dec, dec_seq, *hd),
            sk_fx.reshape(1, n_dec, dec_seq, *hd), sv_fx.reshape(1, n_dec, dec_seq, *hd),
            s_logf.reshape(1, n_dec, dec_seq, N_HEADS))
```
